```python
import math
import jax
import jax.numpy as jnp
from jax import lax
import numpy as np

D_MODEL = 1024
BATCH = 32
SEQ = 256
DEPTH = 1
DEC_BATCH = 4
DEC_SEQ = 2048
PAST_LEN = 256

GRID_W = 64
POS_BASE = 10000.0
EPS = 1e-6
CONV_W = 5
N_MOD = 6
SSD_HEADS = 16
SSD_HEAD_DIM = 64
SSD_WIDTH = SSD_HEADS * SSD_HEAD_DIM
SSD_GROUPS = 2
SSD_STATE = 64
SSD_CHUNK = 128
SSD_XBC = SSD_WIDTH + 2 * SSD_GROUPS * SSD_STATE
DN_HEADS = 8
DN_HEAD_DIM = 128
DN_WIDTH = DN_HEADS * DN_HEAD_DIM
DN_CHUNK = 64
MIX_WIDTH = SSD_WIDTH + DN_WIDTH
IN_COLS = SSD_WIDTH + SSD_XBC + 2 * SSD_HEADS + 3 * DN_WIDTH + DN_WIDTH + 2 * DN_HEADS + 2 * DN_HEADS
N_EXPERT_GROUPS = 4
EXPERTS_PER_GROUP = 8
N_EXPERTS = N_EXPERT_GROUPS * EXPERTS_PER_GROUP
TOP_K = 2
D_FF_EXPERT = 512

kernel_name = 'hybrid_ssd_deltanet_hmoe_diffusion_step'


def rmsnorm(x, w):
    xf = x.astype(jnp.float32)
    y = xf * lax.rsqrt(jnp.mean(xf * xf, axis=-1, keepdims=True) + EPS)
    return (y * w.astype(jnp.float32)).astype(x.dtype)


def l2norm(x):
    xf = x.astype(jnp.float32)
    return xf * lax.rsqrt(jnp.sum(xf * xf, axis=-1, keepdims=True) + EPS)


def rev(t):
    return jnp.flip(t, axis=1)


def grid_sincos_2d(n_tokens):
    rows = n_tokens // GRID_W
    quarter = D_MODEL // 4
    omega = 1.0 / (POS_BASE ** (jnp.arange(quarter, dtype=jnp.float32) / quarter))
    r = jnp.broadcast_to(jnp.arange(rows, dtype=jnp.float32)[:, None], (rows, GRID_W)).reshape(-1)
    col = jnp.broadcast_to(jnp.arange(GRID_W, dtype=jnp.float32)[None, :], (rows, GRID_W)).reshape(-1)
    ar = r[:, None] * omega
    ac = col[:, None] * omega
    return jnp.concatenate([jnp.sin(ar), jnp.cos(ar), jnp.sin(ac), jnp.cos(ac)], axis=-1)


def centred_dwconv(x, w, b):
    ch = x.shape[-1]
    y = lax.conv_general_dilated(x, w[:, None, :].astype(x.dtype), window_strides=(1,),
                                 padding=((CONV_W // 2, CONV_W // 2),),
                                 dimension_numbers=('NWC', 'WIO', 'NWC'),
                                 feature_group_count=ch)
    return y + b.astype(x.dtype)


def ssd_chunked(x, dt, a_neg, bmat, cmat, h0):
    f32 = jnp.float32
    bsz, seqlen, nh, hp = x.shape
    ng, ns = bmat.shape[2], bmat.shape[3]
    hg = nh // ng
    q = SSD_CHUNK
    nc = seqlen // q
    x = x.astype(f32).reshape(bsz, nc, q, ng, hg, hp)
    dt = dt.astype(f32).reshape(bsz, nc, q, ng, hg)
    bmat = bmat.astype(f32).reshape(bsz, nc, q, ng, ns)
    cmat = cmat.astype(f32).reshape(bsz, nc, q, ng, ns)
    acum = jnp.cumsum(dt * a_neg.astype(f32).reshape(ng, hg), axis=2)
    lower = jnp.tril(jnp.ones((q, q), bool))
    seg = acum[:, :, :, None] - acum[:, :, None]
    decay = jnp.exp(jnp.where(lower[:, :, None, None], seg, -jnp.inf))
    cb = jnp.einsum('bcign,bcjgn->bcijg', cmat, bmat)
    xdt = x * dt[..., None]
    y_diag = jnp.einsum('bcijgh,bcjghp->bcighp', cb[..., None] * decay, xdt)
    end_decay = jnp.exp(acum[:, :, -1:] - acum)
    states = jnp.einsum('bcjgn,bcjghp->bcghpn', bmat, xdt * end_decay[..., None])
    chunk_decay = jnp.exp(acum[:, :, -1])

    def step(h, inp):
        s_c, d_c = inp
        return h * d_c[..., None, None] + s_c, h

    h_last, h_prev = lax.scan(step, h0.astype(f32).reshape(bsz, ng, hg, hp, ns),
                              (jnp.moveaxis(states, 1, 0), jnp.moveaxis(chunk_decay, 1, 0)))
    h_prev = jnp.moveaxis(h_prev, 0, 1)
    y_off = jnp.einsum('bcign,bcghpn->bcighp', cmat, h_prev) * jnp.exp(acum)[..., None]
    y = (y_diag + y_off).reshape(bsz, seqlen, nh, hp)
    return y, h_last.reshape(bsz, nh, hp, ns)


def gated_delta_chunked(q, k, v, g, beta, s0):
    f32 = jnp.float32
    bsz, seqlen, nh, dk = q.shape
    dv = v.shape[-1]
    c = DN_CHUNK
    nc = seqlen // c

    def chunks(t):
        return t.astype(f32).reshape(bsz, nc, c, nh, -1).transpose(0, 1, 3, 2, 4)

    q = chunks(q) * (dk ** -0.5)
    k = chunks(k)
    v = chunks(v)
    g = g.astype(f32).reshape(bsz, nc, c, nh).transpose(0, 1, 3, 2)
    beta = beta.astype(f32).reshape(bsz, nc, c, nh).transpose(0, 1, 3, 2)
    gc = jnp.cumsum(g, axis=-1)
    incl = jnp.tril(jnp.ones((c, c), bool))
    strict = jnp.tril(jnp.ones((c, c), bool), -1)
    decay = jnp.exp(jnp.where(incl, gc[..., :, None] - gc[..., None, :], -jnp.inf))
    kk = jnp.einsum('bnhik,bnhjk->bnhij', k, k)
    t_mat = jnp.where(strict, kk * decay * beta[..., None], 0.0) + jnp.eye(c, dtype=f32)
    rhs = jnp.concatenate([v * beta[..., None], k * (beta * jnp.exp(gc))[..., None]], axis=-1)
    sol = lax.linalg.triangular_solve(t_mat, rhs, left_side=True, lower=True, unit_diagonal=True)
    u = sol[..., :dv]
    w = sol[..., dv:]
    qk = jnp.einsum('bnhik,bnhjk->bnhij', q, k) * decay
    q_dec = q * jnp.exp(gc)[..., None]
    k_dec = k * jnp.exp(gc[..., -1:] - gc)[..., None]
    cdec = jnp.exp(gc[..., -1])

    def step(s, inp):
        qk_i, qd_i, kd_i, u_i, w_i, cd_i = inp
        v_new = u_i - jnp.einsum('bhck,bhkv->bhcv', w_i, s)
        o = jnp.einsum('bhck,bhkv->bhcv', qd_i, s) + jnp.einsum('bhij,bhjv->bhiv', qk_i, v_new)
        s = s * cd_i[..., None, None] + jnp.einsum('bhck,bhcv->bhkv', kd_i, v_new)
        return s, o

    xs = (jnp.moveaxis(qk, 1, 0), jnp.moveaxis(q_dec, 1, 0), jnp.moveaxis(k_dec, 1, 0),
          jnp.moveaxis(u, 1, 0), jnp.moveaxis(w, 1, 0), jnp.moveaxis(cdec, 1, 0))
    s_last, o = lax.scan(step, s0.astype(f32), xs)
    o = jnp.moveaxis(o, 0, 1).transpose(0, 1, 3, 2, 4).reshape(bsz, seqlen, nh, dv)
    return o, s_last


def hybrid_mixer(h, lp, init):
    f32 = jnp.float32
    bsz, seqlen, _ = h.shape
    proj = h @ lp['w_in']
    cuts = [SSD_WIDTH, SSD_XBC, 2 * SSD_HEADS, 3 * DN_WIDTH, DN_WIDTH, 2 * DN_HEADS]
    z_s, xbc, dt_s, qkv, z_d, a_d, b_d = jnp.split(proj, list(np.cumsum(cuts)), axis=-1)

    xbc = jax.nn.silu(centred_dwconv(xbc, lp['conv_ssd_w'], lp['conv_ssd_b']))
    xs, bm, cm = jnp.split(xbc, [SSD_WIDTH, SSD_WIDTH + SSD_GROUPS * SSD_STATE], axis=-1)
    xs = xs.reshape(bsz, seqlen, SSD_HEADS, SSD_HEAD_DIM)
    bm = bm.reshape(bsz, seqlen, SSD_GROUPS, SSD_STATE)
    cm = cm.reshape(bsz, seqlen, SSD_GROUPS, SSD_STATE)
    dt = jax.nn.softplus(dt_s.astype(f32).reshape(bsz, seqlen, 2, SSD_HEADS) + lp['ssd_dt_bias'].astype(f32))
    a_neg = -jnp.exp(lp['ssd_a_log'].astype(f32))
    y_f, s_f = ssd_chunked(xs, dt[:, :, 0], a_neg[0], bm, cm, init[0])
    y_b, s_b = ssd_chunked(rev(xs), rev(dt[:, :, 1]), a_neg[1], rev(bm), rev(cm), init[1])
    y = y_f + rev(y_b) + lp['ssd_d'].astype(f32)[:, None] * xs.astype(f32)
    y = y.reshape(bsz, seqlen, SSD_WIDTH) * jax.nn.silu(z_s.astype(f32))
    y_ssd = rmsnorm(y, lp['ssd_norm_w']).astype(h.dtype)

    qkv = jax.nn.silu(centred_dwconv(qkv, lp['conv_dn_w'], lp['conv_dn_b']))
    q, k, v = jnp.split(qkv, 3, axis=-1)
    q = l2norm(q.reshape(bsz, seqlen, DN_HEADS, DN_HEAD_DIM))
    k = l2norm(k.reshape(bsz, seqlen, DN_HEADS, DN_HEAD_DIM))
    v = v.reshape(bsz, seqlen, DN_HEADS, DN_HEAD_DIM)
    beta = jax.nn.sigmoid(b_d.astype(f32).reshape(bsz, seqlen, 2, DN_HEADS))
    g = -jnp.exp(lp['dn_a_log'].astype(f32)) * jax.nn.softplus(
        a_d.astype(f32).reshape(bsz, seqlen, 2, DN_HEADS) + lp['dn_dt_bias'].astype(f32))
    o_f, d_f = gated_delta_chunked(q, k, v, g[:, :, 0], beta[:, :, 0], init[2])
    o_b, d_b = gated_delta_chunked(rev(q), rev(k), rev(v), rev(g[:, :, 1]), rev(beta[:, :, 1]), init[3])
    o = rmsnorm(o_f + rev(o_b), lp['dn_norm_w']) * jax.nn.silu(
        z_d.astype(f32).reshape(bsz, seqlen, DN_HEADS, DN_HEAD_DIM))
    y_dn = o.reshape(bsz, seqlen, DN_WIDTH).astype(h.dtype)

    out = jnp.concatenate([y_ssd, y_dn], axis=-1) @ lp['w_out']
    return out, (s_f, s_b, d_f, d_b)


def hier_moe(h, lp):
    bsz, seqlen, d = h.shape
    t = h.reshape(bsz * seqlen, d)
    g_logits = (t @ lp['w_router_group'] + lp['b_router_group']).astype(jnp.float32)
    g_w, g_idx = lax.top_k(jax.nn.softmax(g_logits, axis=-1), 1)
    e_logits = (t @ lp['w_router_expert'] + lp['b_router_expert']).astype(jnp.float32)
    e_logits = e_logits.reshape(-1, N_EXPERT_GROUPS, EXPERTS_PER_GROUP)
    e_logits = jnp.einsum('tge,tg->te', e_logits,
                          jax.nn.one_hot(g_idx[:, 0], N_EXPERT_GROUPS, dtype=jnp.float32))
    e_w, e_idx = lax.top_k(jax.nn.softmax(e_logits, axis=-1), TOP_K)
    gate = g_w * e_w / jnp.sum(e_w, axis=-1, keepdims=True)
    ids = g_idx * EXPERTS_PER_GROUP + e_idx
    comb = jnp.einsum('tk,tke->te', gate,
                      jax.nn.one_hot(ids, N_EXPERTS, dtype=jnp.float32)).astype(h.dtype)
    out = jnp.zeros_like(t)
    for grp in range(N_EXPERT_GROUPS):
        sl = slice(grp * EXPERTS_PER_GROUP, (grp + 1) * EXPERTS_PER_GROUP)
        a = jnp.einsum('td,edf->tef', t, lp['w_gate'][sl])
        u = jnp.einsum('td,edf->tef', t, lp['w_up'][sl])
        act = jax.nn.silu(a) * u * comb[:, sl, None]
        out = out + jnp.einsum('tef,efd->td', act, lp['w_down'][sl])
    return out.reshape(bsz, seqlen, d)


def adaln_mod(cond, lp):
    m = jax.nn.silu(cond) @ lp['w_ada'] + lp['b_ada']
    return m.reshape(cond.shape[0], N_MOD, 1, D_MODEL)


def trunk_layer(x, mod, lp, init):
    mod = mod.astype(x.dtype)
    sh1, sc1, gt1, sh2, sc2, gt2 = (mod[:, i] for i in range(N_MOD))
    h = rmsnorm(x, lp['norm1_w']) * (1 + sc1) + sh1
    m, states = hybrid_mixer(h, lp, init)
    x = x + gt1 * m
    h = rmsnorm(x, lp['norm2_w']) * (1 + sc2) + sh2
    x = x + gt2 * hier_moe(h, lp)
    return x, states


def setup_inputs(seed: int = 0) -> dict:
    key = jax.random.key(seed)
    k = jax.random.split(key, 34)
    f32 = jnp.float32

    def nrm(kk, shape, scale):
        return scale * jax.random.normal(kk, shape, f32)

    def dt_bias(kk, shape):
        dt = jnp.exp(jax.random.uniform(kk, shape, f32, minval=math.log(1e-3), maxval=math.log(1e-1)))
        return dt + jnp.log(-jnp.expm1(-dt))

    def a_log(kk, shape):
        return jnp.log(jax.random.uniform(kk, shape, f32, minval=1.0, maxval=16.0))

    return {
        'x_prompt': nrm(k[0], (BATCH, SEQ, D_MODEL), 1.0),
        'x_sample': nrm(k[1], (DEC_BATCH, DEC_SEQ, D_MODEL), 1.0),
        'state_ssd_fwd': nrm(k[2], (DEC_BATCH, DEPTH, SSD_HEADS, SSD_HEAD_DIM, SSD_STATE), 0.1),
        'state_ssd_bwd': nrm(k[3], (DEC_BATCH, DEPTH, SSD_HEADS, SSD_HEAD_DIM, SSD_STATE), 0.1),
        'state_dn_fwd': nrm(k[4], (DEC_BATCH, DEPTH, DN_HEADS, DN_HEAD_DIM, DN_HEAD_DIM), 0.1),
        'state_dn_bwd': nrm(k[5], (DEC_BATCH, DEPTH, DN_HEADS, DN_HEAD_DIM, DN_HEAD_DIM), 0.1),
        'c': nrm(k[6], (DEC_BATCH, D_MODEL), 1.0),
        'c_ctx': nrm(k[7], (D_MODEL,), 1.0),
        'w_ada': nrm(k[8], (DEPTH, D_MODEL, N_MOD * D_MODEL), 0.5 * D_MODEL ** -0.5),
        'b_ada': nrm(k[9], (DEPTH, N_MOD * D_MODEL), 0.02),
        'norm1_w': 1.0 + nrm(k[10], (DEPTH, D_MODEL), 0.02),
        'w_in': nrm(k[11], (DEPTH, D_MODEL, IN_COLS), D_MODEL ** -0.5),
        'conv_ssd_w': nrm(k[12], (DEPTH, CONV_W, SSD_XBC), CONV_W ** -0.5),
        'conv_ssd_b': nrm(k[13], (DEPTH, SSD_XBC), 0.02),
        'conv_dn_w': nrm(k[14], (DEPTH, CONV_W, 3 * DN_WIDTH), CONV_W ** -0.5),
        'conv_dn_b': nrm(k[15], (DEPTH, 3 * DN_WIDTH), 0.02),
        'ssd_dt_bias': dt_bias(k[16], (DEPTH, 2, SSD_HEADS)),
        'ssd_a_log': a_log(k[17], (DEPTH, 2, SSD_HEADS)),
        'ssd_d': 1.0 + nrm(k[18], (DEPTH, SSD_HEADS), 0.02),
        'ssd_norm_w': 1.0 + nrm(k[19], (DEPTH, SSD_WIDTH), 0.02),
        'dn_dt_bias': dt_bias(k[20], (DEPTH, 2, DN_HEADS)),
        'dn_a_log': a_log(k[21], (DEPTH, 2, DN_HEADS)),
        'dn_norm_w': 1.0 + nrm(k[22], (DEPTH, DN_HEAD_DIM), 0.02),
        'w_out': nrm(k[23], (DEPTH, MIX_WIDTH, D_MODEL), MIX_WIDTH ** -0.5),
        'norm2_w': 1.0 + nrm(k[24], (DEPTH, D_MODEL), 0.02),
        'w_router_group': nrm(k[25], (DEPTH, D_MODEL, N_EXPERT_GROUPS), D_MODEL ** -0.5),
        'b_router_group': nrm(k[26], (DEPTH, N_EXPERT_GROUPS), 0.01),
        'w_router_expert': nrm(k[27], (DEPTH, D_MODEL, N_EXPERTS), D_MODEL ** -0.5),
        'b_router_expert': nrm(k[28], (DEPTH, N_EXPERTS), 0.01),
        'w_gate': nrm(k[29], (DEPTH, N_EXPERTS, D_MODEL, D_FF_EXPERT), D_MODEL ** -0.5),
        'w_up': nrm(k[30], (DEPTH, N_EXPERTS, D_MODEL, D_FF_EXPERT), D_MODEL ** -0.5),
        'w_down': nrm(k[31], (DEPTH, N_EXPERTS, D_FF_EXPERT, D_MODEL), D_FF_EXPERT ** -0.5),
        'final_norm_w': 1.0 + nrm(k[32], (D_MODEL,), 0.02),
    }


def reference(x_prompt, x_sample, state_ssd_fwd, state_ssd_bwd, state_dn_fwd, state_dn_bwd, c, c_ctx,
              w_ada, b_ada, norm1_w, w_in, conv_ssd_w, conv_ssd_b, conv_dn_w, conv_dn_b,
              ssd_dt_bias, ssd_a_log, ssd_d, ssd_norm_w, dn_dt_bias, dn_a_log, dn_norm_w, w_out,
              norm2_w, w_router_group, b_router_group, w_router_expert, b_router_expert,
              w_gate, w_up, w_down, final_norm_w):
    nb = x_prompt.shape[0]
    zero_init = (jnp.zeros((nb, SSD_HEADS, SSD_HEAD_DIM, SSD_STATE), jnp.float32),
                 jnp.zeros((nb, SSD_HEADS, SSD_HEAD_DIM, SSD_STATE), jnp.float32),
                 jnp.zeros((nb, DN_HEADS, DN_HEAD_DIM, DN_HEAD_DIM), jnp.float32),
                 jnp.zeros((nb, DN_HEADS, DN_HEAD_DIM, DN_HEAD_DIM), jnp.float32))
    xc = x_prompt
    xl = x_sample + grid_sincos_2d(x_sample.shape[1]).astype(x_sample.dtype)
    ctx_states = ([], [], [], [])
    for layer in range(DEPTH):
        lp = {
            'w_ada': w_ada[layer], 'b_ada': b_ada[layer], 'norm1_w': norm1_w[layer],
            'w_in': w_in[layer], 'conv_ssd_w': conv_ssd_w[layer], 'conv_ssd_b': conv_ssd_b[layer],
            'conv_dn_w': conv_dn_w[layer], 'conv_dn_b': conv_dn_b[layer],
            'ssd_dt_bias': ssd_dt_bias[layer], 'ssd_a_log': ssd_a_log[layer], 'ssd_d': ssd_d[layer],
            'ssd_norm_w': ssd_norm_w[layer], 'dn_dt_bias': dn_dt_bias[layer], 'dn_a_log': dn_a_log[layer],
            'dn_norm_w': dn_norm_w[layer], 'w_out': w_out[layer], 'norm2_w': norm2_w[layer],
            'w_router_group': w_router_group[layer], 'b_router_group': b_router_group[layer],
            'w_router_expert': w_router_expert[layer], 'b_router_expert': b_router_expert[layer],
            'w_gate': w_gate[layer], 'w_up': w_up[layer], 'w_down': w_down[layer],
        }
        xc, st = trunk_layer(xc, adaln_mod(c_ctx[None, :], lp), lp, zero_init)
        for lst, s in zip(ctx_states, st):
            lst.append(s)
        lat_init = (state_ssd_fwd[:, layer], state_ssd_bwd[:, layer],
                    state_dn_fwd[:, layer], state_dn_bwd[:, layer])
        xl, _ = trunk_layer(xl, adaln_mod(c, lp), lp, lat_init)
    y_prompt = rmsnorm(xc, final_norm_w)
    y_sample = rmsnorm(xl, final_norm_w)
    new_ssd_fwd = jnp.stack(ctx_states[0], axis=1).astype(x_prompt.dtype)
    new_ssd_bwd = jnp.stack(ctx_states[1], axis=1).astype(x_prompt.dtype)
    new_dn_fwd = jnp.stack(ctx_states[2], axis=1).astype(x_prompt.dtype)
    new_dn_bwd = jnp.stack(ctx_states[3], axis=1).astype(x_prompt.dtype)
    return (y_prompt, y_sample, new_ssd_fwd, new_ssd_bwd, new_dn_fwd, new_dn_bwd)
```

```python
import functools

import numpy as np
import jax
import jax.numpy as jnp
from jax import lax
from jax.experimental import pallas as pl
from jax.experimental.pallas import tpu as pltpu

F32 = jnp.float32
BF16 = jnp.bfloat16
HI = lax.Precision.HIGHEST

D_MODEL = 1024
EPS = 1e-6
GRID_W = 64
POS_BASE = 10000.0
CONV_W = 5
N_MOD = 6
SSD_HEADS = 16
SSD_HEAD_DIM = 64
SSD_WIDTH = SSD_HEADS * SSD_HEAD_DIM
SSD_GROUPS = 2
SSD_STATE = 64
SSD_CHUNK = 128
DN_HEADS = 8
DN_HEAD_DIM = 128
DN_WIDTH = DN_HEADS * DN_HEAD_DIM
DN_CHUNK = 64
N_EXPERT_GROUPS = 4
EXPERTS_PER_GROUP = 8
N_EXPERTS = N_EXPERT_GROUPS * EXPERTS_PER_GROUP
D_FF = 512

LANES = 128
SUBLANES = 8
VMEM_LIMIT = 56 * 1024 * 1024

COL_Z_S = 0
COL_X = 1024
COL_BC = 2048
COL_QKV = 2304
COL_Z_D = 5376
COL_SMALL = 6400
PROJ_COLS = 6528

ROW_TILE = 256
MOE_TILE = 256
NEG_BIG = -1e30


def _dot(a, b, prec=None):
    return jnp.dot(a, b, preferred_element_type=F32, precision=prec)


def _dot_nt(a, b, prec=None):
    return lax.dot_general(a, b, (((1,), (1,)), ((), ())), preferred_element_type=F32, precision=prec)


def _dot_tn(a, b, prec=None):
    return lax.dot_general(a, b, (((0,), (0,)), ((), ())), preferred_element_type=F32, precision=prec)


def _silu(x):
    return x * jax.nn.sigmoid(x)


def _softplus(x):
    return jnp.maximum(x, 0.0) + jnp.log1p(jnp.exp(-jnp.abs(x)))


def _rms_scale(x):
    return lax.rsqrt(jnp.mean(x * x, axis=-1, keepdims=True) + EPS)


def _cparams(sem, vmem=VMEM_LIMIT):
    return pltpu.CompilerParams(dimension_semantics=sem, vmem_limit_bytes=vmem)


def _ada_kernel(c_ref, w_ref, b_ref, o_ref):
    c = c_ref[...]
    o_ref[...] = _dot(_silu(c), w_ref[...], HI) + b_ref[...]


def _adaln(cond, w_ada, b_ada):
    n_out = N_MOD * D_MODEL
    tn = 1536
    return pl.pallas_call(
        _ada_kernel,
        grid=(n_out // tn,),
        in_specs=[pl.BlockSpec((SUBLANES, D_MODEL), lambda j: (0, 0)),
                  pl.BlockSpec((D_MODEL, tn), lambda j: (0, j)),
                  pl.BlockSpec((1, tn), lambda j: (0, j))],
        out_specs=pl.BlockSpec((SUBLANES, tn), lambda j: (0, j)),
        out_shape=jax.ShapeDtypeStruct((SUBLANES, n_out), F32),
        compiler_params=_cparams(("arbitrary",)),
        name="adaln",
    )(cond, w_ada, b_ada.reshape(1, n_out))


def _inproj_kernel(xc_ref, xl_ref, pos_ref, mod_ref, nw_ref, w_ref, proj_ref, x0_ref, *, ctx_tiles):
    is_ctx = pl.program_id(0) < ctx_tiles
    x = jnp.where(is_ctx, xc_ref[...], xl_ref[...] + pos_ref[...])
    x0_ref[...] = x
    mod = mod_ref[0]
    h = (x * _rms_scale(x) * nw_ref[...]) * (1.0 + mod[1:2]) + mod[0:1]
    proj_ref[...] = _dot(h.astype(BF16), w_ref[...])


def _stream_maps(ctx_tiles, lat_tiles_per_mod):
    def ctx_row(i):
        return (jnp.minimum(i, ctx_tiles - 1), 0)

    def lat_row(i):
        return (jnp.maximum(i - ctx_tiles, 0), 0)

    def mod_row(i):
        return (jnp.where(i < ctx_tiles, 0, 1 + (i - ctx_tiles) // lat_tiles_per_mod), 0, 0)

    return ctx_row, lat_row, mod_row


def _inproj(x_ctx, x_lat, pos, mod, norm_w, w_r, lat_seq):
    tm = ROW_TILE
    ctx_tiles = x_ctx.shape[0] // tm
    n_total = x_ctx.shape[0] + x_lat.shape[0]
    n_pos = pos.shape[0] // tm
    ctx_row, lat_row, mod_row = _stream_maps(ctx_tiles, lat_seq // tm)
    return pl.pallas_call(
        functools.partial(_inproj_kernel, ctx_tiles=ctx_tiles),
        grid=(n_total // tm,),
        in_specs=[pl.BlockSpec((tm, D_MODEL), ctx_row),
                  pl.BlockSpec((tm, D_MODEL), lat_row),
                  pl.BlockSpec((tm, D_MODEL), lambda i: (jnp.maximum(i - ctx_tiles, 0) % n_pos, 0)),
                  pl.BlockSpec((1, N_MOD, D_MODEL), mod_row),
                  pl.BlockSpec((1, D_MODEL), lambda i: (0, 0)),
                  pl.BlockSpec((D_MODEL, PROJ_COLS), lambda i: (0, 0), pipeline_mode=pl.Buffered(1))],
        out_specs=[pl.BlockSpec((tm, PROJ_COLS), lambda i: (i, 0)),
                   pl.BlockSpec((tm, D_MODEL), lambda i: (i, 0))],
        out_shape=[jax.ShapeDtypeStruct((n_total, PROJ_COLS), F32),
                   jax.ShapeDtypeStruct((n_total, D_MODEL), F32)],
        compiler_params=_cparams(("arbitrary",)),
        name="inproj",
    )(x_ctx, x_lat, pos, mod, norm_w, w_r)


CONV_ROWS = 128
CONV_HALO = SUBLANES


def _conv_silu(src_ref, w_ref, b_ref, pad_ref, store, seq_len):
    n_ch = src_ref.shape[1]
    zeros = jnp.zeros((CONV_HALO, n_ch), F32)
    pad_ref[0:CONV_HALO, :] = zeros
    pad_ref[CONV_HALO + seq_len:2 * CONV_HALO + seq_len, :] = zeros

    def copy_body(i, carry):
        r0 = pl.multiple_of(i * CONV_ROWS, CONV_ROWS)
        pad_ref[pl.ds(CONV_HALO + r0, CONV_ROWS), :] = src_ref[pl.ds(r0, CONV_ROWS), :].astype(F32)
        return carry

    lax.fori_loop(0, seq_len // CONV_ROWS, copy_body, 0)
    win = CONV_ROWS + 2 * CONV_HALO
    w = w_ref[...]
    b = b_ref[...]

    def body(i, carry):
        r0 = pl.multiple_of(i * CONV_ROWS, CONV_ROWS)
        v = pad_ref[pl.ds(r0, win), :]
        acc = jnp.zeros((CONV_ROWS, n_ch), F32) + b
        for k in range(CONV_W):
            shift = (CONV_W // 2 - k) % win
            sh = v if shift == 0 else pltpu.roll(v, shift, 0)
            acc = acc + w[k:k + 1, :] * sh[CONV_HALO:CONV_HALO + CONV_ROWS, :]
        store(pl.ds(r0, CONV_ROWS), _silu(acc))
        return carry

    lax.fori_loop(0, seq_len // CONV_ROWS, body, 0)


def _tri_masks(n, reverse):
    ri = lax.broadcasted_iota(jnp.int32, (n, n), 0)
    ci = lax.broadcasted_iota(jnp.int32, (n, n), 1)
    if reverse:
        return ri <= ci, ri < ci
    return ri >= ci, ri > ci


SSD_GROUP_HEADS = SSD_HEADS // SSD_GROUPS
SSD_GROUP_COLS = SSD_GROUP_HEADS * SSD_HEAD_DIM


def _ssd_kernel(*refs, seq_len, has_init, emit_state):
    refs = list(refs)
    (x_ref, bc_ref, z_ref, sm_ref, cwx_ref, cbx_ref, cwbc_ref, cbbc_ref, gp_ref, dexp_ref,
     sel_ref, exp_dt_ref, exp_a_ref) = refs[:13]
    refs = refs[13:]
    if has_init:
        h0f_ref, h0b_ref = refs[:2]
        refs = refs[2:]
    y_ref = refs.pop(0)
    if emit_state:
        hf_ref, hb_ref = refs[:2]
        refs = refs[2:]
    xpad, bcpad, xc, bcc, dtg, yacc, hst = refs

    q = SSD_CHUNK
    n_chunks = seq_len // q

    def store_x(rows, val):
        xc[rows, :] = val
        yacc[rows, :] = val * dexp_ref[...]

    _conv_silu(x_ref, cwx_ref, cbx_ref, xpad, store_x, seq_len)

    def store_bc(rows, val):
        bcc[rows, :] = val

    _conv_silu(bc_ref, cwbc_ref, cbbc_ref, bcpad, store_bc, seq_len)

    gp = gp_ref[...]
    bias_row = gp[0:1, :]
    aneg_row = -jnp.exp(gp[1:2, :])
    sel_dt = sel_ref[0, 0]
    sel_a = sel_ref[0, 1]

    def gate_body(i, carry):
        r0 = pl.multiple_of(i * q, q)
        dt_full = _softplus(sm_ref[pl.ds(r0, q), :].astype(F32) + bias_row)
        dtg[pl.ds(r0, q), :] = (_dot(dt_full, sel_dt, HI) + _dot(dt_full * aneg_row, sel_a, HI))
        return carry

    lax.fori_loop(0, n_chunks, gate_body, 0)

    if has_init:
        hst[0] = h0f_ref[0]
        hst[1] = h0b_ref[0]
    else:
        hst[...] = jnp.zeros(hst.shape, F32)

    ri = lax.broadcasted_iota(jnp.int32, (q, q), 0)
    ci = lax.broadcasted_iota(jnp.int32, (q, q), 1)
    lane = lax.broadcasted_iota(jnp.int32, (q, LANES), 1)
    low_half = lane < SSD_HEAD_DIM

    def chunk(d, c):
        rows = pl.ds(pl.multiple_of(c * q, q), q)
        mask = (ri >= ci) if d == 0 else (ri <= ci)
        tri = mask.astype(F32)
        xck = xc[rows, :]
        bcck = bcc[rows, :]
        b_c = bcck[:, :SSD_STATE].astype(BF16)
        c_c = bcck[:, SSD_STATE:].astype(BF16)
        gt = dtg[rows, :]
        cum = _dot(tri, gt, HI)
        cum_t = cum.T
        end8 = cum[q - SUBLANES:q, :] if d == 0 else cum[0:SUBLANES, :]
        end_row = SUBLANES - 1 if d == 0 else 0
        dt_x = _dot(gt, exp_dt_ref[d], HI)
        ac_x = _dot(cum, exp_a_ref[d], HI)
        ae_x = _dot(end8, exp_a_ref[d], HI)[end_row:end_row + 1, :]
        xdt = xck * dt_x
        cb = _dot_nt(c_c, b_c)
        lane0 = 2 * SSD_GROUP_HEADS + d * SSD_GROUP_HEADS
        parts = []
        for hp in range(SSD_GROUP_HEADS // 2):
            ms = []
            for j in (2 * hp, 2 * hp + 1):
                col = cum[:, lane0 + j:lane0 + j + 1]
                row = cum_t[lane0 + j:lane0 + j + 1, :]
                dec = jnp.exp(jnp.where(mask, col - row, NEG_BIG))
                ms.append((cb * dec).astype(BF16))
            x2 = xdt[:, hp * LANES:(hp + 1) * LANES]
            w_lo = jnp.where(low_half, x2, 0.0)
            w_hi = jnp.where(low_half, 0.0, x2)
            parts.append(_dot(jnp.concatenate(ms, axis=1),
                              jnp.concatenate([w_lo, w_hi], axis=0).astype(BF16)))
        y_diag = jnp.concatenate(parts, axis=1)
        hs = hst[d]
        y_off = _dot(c_c, hs.astype(BF16)) * jnp.exp(ac_x)
        yacc[rows, :] += y_diag + y_off
        hst[d] = hs * jnp.exp(ae_x) + _dot_tn(b_c, (xdt * jnp.exp(ae_x - ac_x)).astype(BF16))

    def step(s, carry):
        chunk(0, s)
        chunk(1, n_chunks - 1 - s)
        return carry

    lax.fori_loop(0, n_chunks, step, 0)

    def out_body(i, carry):
        rows = pl.ds(pl.multiple_of(i * q, q), q)
        y_ref[rows, :] = yacc[rows, :] * _silu(z_ref[rows, :].astype(F32))
        return carry

    lax.fori_loop(0, n_chunks, out_body, 0)
    if emit_state:
        hf_ref[0] = hst[0]
        hb_ref[0] = hst[1]


def _ssd(proj, consts, n_seq, seq_len, row_off, init, emit_state):
    rb = row_off // seq_len
    cw = SSD_GROUP_COLS
    has_init = init is not None

    def col(block_cols, base):
        return base // block_cols

    in_specs = [
        pl.BlockSpec((seq_len, cw), lambda b, g: (rb + b, col(cw, COL_X) + g)),
        pl.BlockSpec((seq_len, LANES), lambda b, g: (rb + b, col(LANES, COL_BC) + g)),
        pl.BlockSpec((seq_len, cw), lambda b, g: (rb + b, col(cw, COL_Z_S) + g)),
        pl.BlockSpec((seq_len, LANES), lambda b, g: (rb + b, col(LANES, COL_SMALL))),
        pl.BlockSpec((SUBLANES, cw), lambda b, g: (0, g)),
        pl.BlockSpec((1, cw), lambda b, g: (0, g)),
        pl.BlockSpec((SUBLANES, LANES), lambda b, g: (0, g)),
        pl.BlockSpec((1, LANES), lambda b, g: (0, g)),
        pl.BlockSpec((SUBLANES, LANES), lambda b, g: (0, 0)),
        pl.BlockSpec((1, cw), lambda b, g: (0, g)),
        pl.BlockSpec((1, 2, LANES, LANES), lambda b, g: (g, 0, 0, 0)),
        pl.BlockSpec((2, LANES, cw), lambda b, g: (0, 0, 0)),
        pl.BlockSpec((2, LANES, cw), lambda b, g: (0, 0, 0)),
    ]
    args = [proj, proj, proj, proj, consts["cw_x"], consts["cb_x"], consts["cw_bc"], consts["cb_bc"],
            consts["gate_params"], consts["d_exp"], consts["ssd_sel"], consts["ssd_exp_dt"],
            consts["ssd_exp_a"]]
    if has_init:
        in_specs += [pl.BlockSpec((1, SSD_STATE, cw), lambda b, g: (b, 0, g))] * 2
        args += list(init)
    out_specs = [pl.BlockSpec((seq_len, cw), lambda b, g: (b, g))]
    out_shape = [jax.ShapeDtypeStruct((n_seq * seq_len, SSD_WIDTH), F32)]
    if emit_state:
        out_specs += [pl.BlockSpec((1, SSD_STATE, cw), lambda b, g: (b, 0, g))] * 2
        out_shape += [jax.ShapeDtypeStruct((n_seq, SSD_STATE, SSD_WIDTH), F32)] * 2
    scratch = [
        pltpu.VMEM((seq_len + 2 * CONV_HALO, cw), F32),
        pltpu.VMEM((seq_len + 2 * CONV_HALO, LANES), F32),
        pltpu.VMEM((seq_len, cw), F32),
        pltpu.VMEM((seq_len, LANES), F32),
        pltpu.VMEM((seq_len, LANES), F32),
        pltpu.VMEM((seq_len, cw), F32),
        pltpu.VMEM((2, SSD_STATE, cw), F32),
    ]
    return pl.pallas_call(
        functools.partial(_ssd_kernel, seq_len=seq_len, has_init=has_init, emit_state=emit_state),
        grid=(n_seq, SSD_GROUPS),
        in_specs=in_specs,
        out_specs=out_specs,
        out_shape=out_shape,
        scratch_shapes=scratch,
        compiler_params=_cparams(("arbitrary", "arbitrary")),
        name="ssd_scan",
    )(*args)


def _dn_kernel(*refs, seq_len, has_init, emit_state):
    refs = list(refs)
    (q_ref, k_ref, v_ref, z_ref, sm_ref, cwq_ref, cbq_ref, cwk_ref, cbk_ref, cwv_ref, cbv_ref,
     gp_ref, nw_ref, sel_ref) = refs[:14]
    refs = refs[14:]
    if has_init:
        s0f_ref, s0b_ref = refs[:2]
        refs = refs[2:]
    y_ref = refs.pop(0)
    if emit_state:
        sf_ref, sb_ref = refs[:2]
        refs = refs[2:]
    pad, qs, ks, vs, gb, oacc, sst = refs

    cq = DN_CHUNK
    n_chunks = seq_len // cq
    scale = DN_HEAD_DIM ** -0.5

    def l2n(v):
        return v * lax.rsqrt(jnp.sum(v * v, axis=-1, keepdims=True) + EPS)

    def store_q(rows, val):
        qs[rows, :] = l2n(val) * scale

    def store_k(rows, val):
        ks[rows, :] = l2n(val)

    def store_v(rows, val):
        vs[rows, :] = val
        oacc[rows, :] = jnp.zeros_like(val)

    _conv_silu(q_ref, cwq_ref, cbq_ref, pad, store_q, seq_len)
    _conv_silu(k_ref, cwk_ref, cbk_ref, pad, store_k, seq_len)
    _conv_silu(v_ref, cwv_ref, cbv_ref, pad, store_v, seq_len)

    gp = gp_ref[...]
    bias_row = gp[0:1, :]
    aneg_row = -jnp.exp(gp[1:2, :])
    lane = lax.broadcasted_iota(jnp.int32, (CONV_ROWS, LANES), 1)
    is_decay_lane = lane < 2 * SSD_HEADS + 2 * DN_HEADS
    sel = sel_ref[0]

    def gate_body(i, carry):
        rows = pl.ds(pl.multiple_of(i * CONV_ROWS, CONV_ROWS), CONV_ROWS)
        sm = sm_ref[rows, :].astype(F32)
        g_full = aneg_row * _softplus(sm + bias_row)
        comb = jnp.where(is_decay_lane, g_full, jax.nn.sigmoid(sm))
        gb[rows, :] = _dot(comb, sel, HI)
        return carry

    lax.fori_loop(0, seq_len // CONV_ROWS, gate_body, 0)

    if has_init:
        sst[0] = s0f_ref[0, 0]
        sst[1] = s0b_ref[0, 0]
    else:
        sst[...] = jnp.zeros(sst.shape, F32)

    ri = lax.broadcasted_iota(jnp.int32, (cq, cq), 0)
    ci = lax.broadcasted_iota(jnp.int32, (cq, cq), 1)
    eye = (ri == ci).astype(F32)

    def chunk(d, c):
        rows = pl.ds(pl.multiple_of(c * cq, cq), cq)
        incl = (ri >= ci) if d == 0 else (ri <= ci)
        strict = (ri > ci) if d == 0 else (ri < ci)
        qn = qs[rows, :]
        kn = ks[rows, :]
        vv = vs[rows, :]
        gbt = gb[rows, :]
        cum = _dot(incl.astype(F32), gbt, HI)
        cum_t = cum.T
        gcol = cum[:, d:d + 1]
        grow = cum_t[d:d + 1, :]
        bcol = gbt[:, 2 + d:3 + d]
        gend = cum[cq - 1:cq, d:d + 1] if d == 0 else cum[0:1, d:d + 1]
        dec = jnp.exp(jnp.where(incl, gcol - grow, NEG_BIG))
        knb = kn.astype(BF16)
        kk = _dot_nt(knb, knb)
        qk = _dot_nt(qn.astype(BF16), knb) * dec
        a_mat = jnp.where(strict, kk * dec * bcol, 0.0)
        xp = -a_mat
        t_inv = eye + xp
        for _ in range(5):
            xp = _dot(xp, xp, HI)
            t_inv = t_inv + _dot(t_inv, xp, HI)
        eg = jnp.exp(gcol)
        rhs = jnp.concatenate([vv * bcol, kn * (bcol * eg)], axis=1)
        sol = _dot(t_inv, rhs, HI)
        u = sol[:, :DN_HEAD_DIM]
        w = sol[:, DN_HEAD_DIM:]
        s = sst[d]
        sb = s.astype(BF16)
        v_new = u - _dot(w.astype(BF16), sb)
        v_new_b = v_new.astype(BF16)
        o = _dot((qn * eg).astype(BF16), sb) + _dot(qk.astype(BF16), v_new_b)
        k_dec = (kn * jnp.exp(gend - gcol)).astype(BF16)
        sst[d] = s * jnp.exp(gend) + _dot_tn(k_dec, v_new_b)
        oacc[rows, :] += o

    def step(i, carry):
        chunk(0, i)
        chunk(1, n_chunks - 1 - i)
        return carry

    lax.fori_loop(0, n_chunks, step, 0)

    def out_body(i, carry):
        rows = pl.ds(pl.multiple_of(i * CONV_ROWS, CONV_ROWS), CONV_ROWS)
        o = oacc[rows, :]
        y_ref[rows, :] = o * _rms_scale(o) * nw_ref[...] * _silu(z_ref[rows, :].astype(F32))
        return carry

    lax.fori_loop(0, seq_len // CONV_ROWS, out_body, 0)
    if emit_state:
        sf_ref[0, 0] = sst[0]
        sb_ref[0, 0] = sst[1]


def _dn(proj, consts, n_seq, seq_len, row_off, init, emit_state):
    rb = row_off // seq_len
    hd = DN_HEAD_DIM
    has_init = init is not None
    cq0 = COL_QKV // hd
    in_specs = [
        pl.BlockSpec((seq_len, hd), lambda b, h: (rb + b, cq0 + h)),
        pl.BlockSpec((seq_len, hd), lambda b, h: (rb + b, cq0 + DN_HEADS + h)),
        pl.BlockSpec((seq_len, hd), lambda b, h: (rb + b, cq0 + 2 * DN_HEADS + h)),
        pl.BlockSpec((seq_len, hd), lambda b, h: (rb + b, COL_Z_D // hd + h)),
        pl.BlockSpec((seq_len, LANES), lambda b, h: (rb + b, COL_SMALL // LANES)),
        pl.BlockSpec((SUBLANES, hd), lambda b, h: (0, h)),
        pl.BlockSpec((1, hd), lambda b, h: (0, h)),
        pl.BlockSpec((SUBLANES, hd), lambda b, h: (0, DN_HEADS + h)),
        pl.BlockSpec((1, hd), lambda b, h: (0, DN_HEADS + h)),
        pl.BlockSpec((SUBLANES, hd), lambda b, h: (0, 2 * DN_HEADS + h)),
        pl.BlockSpec((1, hd), lambda b, h: (0, 2 * DN_HEADS + h)),
        pl.BlockSpec((SUBLANES, LANES), lambda b, h: (0, 0)),
        pl.BlockSpec((1, hd), lambda b, h: (0, 0)),
        pl.BlockSpec((1, LANES, LANES), lambda b, h: (h, 0, 0)),
    ]
    args = [proj, proj, proj, proj, proj,
            consts["cw_dn"], consts["cb_dn"], consts["cw_dn"], consts["cb_dn"], consts["cw_dn"],
            consts["cb_dn"], consts["gate_params"], consts["dn_norm_w"], consts["dn_sel"]]
    if has_init:
        in_specs += [pl.BlockSpec((1, 1, hd, hd), lambda b, h: (b, h, 0, 0))] * 2
        args += list(init)
    out_specs = [pl.BlockSpec((seq_len, hd), lambda b, h: (b, h))]
    out_shape = [jax.ShapeDtypeStruct((n_seq * seq_len, DN_WIDTH), F32)]
    if emit_state:
        out_specs += [pl.BlockSpec((1, 1, hd, hd), lambda b, h: (b, h, 0, 0))] * 2
        out_shape += [jax.ShapeDtypeStruct((n_seq, DN_HEADS, hd, hd), F32)] * 2
    scratch = [
        pltpu.VMEM((seq_len + 2 * CONV_HALO, hd), F32),
        pltpu.VMEM((seq_len, hd), F32),
        pltpu.VMEM((seq_len, hd), F32),
        pltpu.VMEM((seq_len, hd), F32),
        pltpu.VMEM((seq_len, LANES), F32),
        pltpu.VMEM((seq_len, hd), F32),
        pltpu.VMEM((2, hd, hd), F32),
    ]
    return pl.pallas_call(
        functools.partial(_dn_kernel, seq_len=seq_len, has_init=has_init, emit_state=emit_state),
        grid=(n_seq, DN_HEADS),
        in_specs=in_specs,
        out_specs=out_specs,
        out_shape=out_shape,
        scratch_shapes=scratch,
        compiler_params=_cparams(("arbitrary", "arbitrary")),
        name="dn_scan",
    )(*args)


ROUTE_ID_LANE = 0
ROUTE_GATE_LANE = 2


def _outproj_kernel(ysc_ref, ysl_ref, ydc_ref, ydl_ref, x0_ref, mod_ref, snw_ref, n2w_ref, wo_ref,
                    wr_ref, br_ref, x1_ref, h2_ref, route_ref, *, ctx_tiles):
    is_ctx = pl.program_id(0) < ctx_tiles
    ys = jnp.where(is_ctx, ysc_ref[...], ysl_ref[...])
    ysn = (ys * _rms_scale(ys) * snw_ref[...]).astype(BF16)
    ydn = jnp.where(is_ctx, ydc_ref[...], ydl_ref[...]).astype(BF16)
    m = _dot(ysn, wo_ref[0:SSD_WIDTH, :]) + _dot(ydn, wo_ref[SSD_WIDTH:, :])
    mod = mod_ref[0]
    x1 = x0_ref[...] + mod[2:3] * m
    x1_ref[...] = x1
    h2 = (x1 * _rms_scale(x1) * n2w_ref[...]) * (1.0 + mod[4:5]) + mod[3:4]
    h2_ref[...] = h2
    logits = _dot(h2, wr_ref[...], HI) + br_ref[...]
    ln = lax.broadcasted_iota(jnp.int32, logits.shape, 1)
    is_g = ln < N_EXPERT_GROUPS
    gl = jnp.where(is_g, logits, NEG_BIG)
    gmax = jnp.max(gl, axis=-1, keepdims=True)
    gidx = jnp.min(jnp.where(gl == gmax, ln, LANES), axis=-1, keepdims=True)
    gw = 1.0 / jnp.sum(jnp.where(is_g, jnp.exp(gl - gmax), 0.0), axis=-1, keepdims=True)
    lo = N_EXPERT_GROUPS + gidx * EXPERTS_PER_GROUP
    in_grp = (ln >= lo) & (ln < lo + EXPERTS_PER_GROUP)
    el = jnp.where(in_grp, logits, NEG_BIG)
    m1 = jnp.max(el, axis=-1, keepdims=True)
    i1 = jnp.min(jnp.where(el == m1, ln, LANES), axis=-1, keepdims=True)
    el2 = jnp.where(ln == i1, NEG_BIG, el)
    m2 = jnp.max(el2, axis=-1, keepdims=True)
    i2 = jnp.min(jnp.where(el2 == m2, ln, LANES), axis=-1, keepdims=True)
    t = jnp.exp(m2 - m1)
    g1 = gw / (1.0 + t)
    g2 = gw * t / (1.0 + t)
    id1 = (i1 - N_EXPERT_GROUPS).astype(F32)
    id2 = (i2 - N_EXPERT_GROUPS).astype(F32)
    route = jnp.where(ln == 0, id1, jnp.where(ln == 1, id2, jnp.where(ln == 2, g1,
                                                                      jnp.where(ln == 3, g2, 0.0))))
    route_ref[...] = route


def _outproj(yssd_ctx, yssd_lat, ydn_ctx, ydn_lat, x0, mod, consts, lat_seq):
    n_total = x0.shape[0]
    tm = ROW_TILE
    ctx_tiles = yssd_ctx.shape[0] // tm
    ctx_row, lat_row, mod_row = _stream_maps(ctx_tiles, lat_seq // tm)
    row = lambda i: (i, 0)
    const2 = lambda i: (0, 0)
    return pl.pallas_call(
        functools.partial(_outproj_kernel, ctx_tiles=ctx_tiles),
        grid=(n_total // tm,),
        in_specs=[pl.BlockSpec((tm, SSD_WIDTH), ctx_row),
                  pl.BlockSpec((tm, SSD_WIDTH), lat_row),
                  pl.BlockSpec((tm, DN_WIDTH), ctx_row),
                  pl.BlockSpec((tm, DN_WIDTH), lat_row),
                  pl.BlockSpec((tm, D_MODEL), row),
                  pl.BlockSpec((1, N_MOD, D_MODEL), mod_row),
                  pl.BlockSpec((1, SSD_WIDTH), const2),
                  pl.BlockSpec((1, D_MODEL), const2),
                  pl.BlockSpec((SSD_WIDTH + DN_WIDTH, D_MODEL), const2),
                  pl.BlockSpec((D_MODEL, LANES), const2),
                  pl.BlockSpec((1, LANES), const2)],
        out_specs=[pl.BlockSpec((tm, D_MODEL), row),
                   pl.BlockSpec((tm, D_MODEL), row),
                   pl.BlockSpec((tm, LANES), row)],
        out_shape=[jax.ShapeDtypeStruct((n_total, D_MODEL), F32),
                   jax.ShapeDtypeStruct((n_total, D_MODEL), F32),
                   jax.ShapeDtypeStruct((n_total, LANES), F32)],
        compiler_params=_cparams(("arbitrary",)),
        name="outproj_router",
    )(yssd_ctx, yssd_lat, ydn_ctx, ydn_lat, x0, mod, consts["ssd_norm_w"], consts["norm2_w"],
      consts["w_out"], consts["w_router"], consts["b_router"])


def _rank_kernel(route_ref, rank_ref, count_ref, carry):
    i = pl.program_id(0)

    @pl.when(i == 0)
    def _():
        carry[...] = jnp.zeros(carry.shape, F32)

    route = route_ref[...]
    n = route.shape[0]
    ln = lax.broadcasted_iota(jnp.int32, route.shape, 1)
    id1 = route[:, 0:1].astype(jnp.int32)
    id2 = route[:, 1:2].astype(jnp.int32)
    hit1 = ln == id1
    hit2 = ln == id2
    onehot = jnp.where(hit1, 1.0, jnp.where(hit2, 1.0, 0.0))
    ri = lax.broadcasted_iota(jnp.int32, (n, n), 0)
    ci = lax.broadcasted_iota(jnp.int32, (n, n), 1)
    before = (ri > ci).astype(BF16)
    tot = _dot(before, onehot.astype(BF16)) + carry[0:1, :]
    r1 = jnp.sum(jnp.where(hit1, tot, 0.0), axis=-1, keepdims=True)
    r2 = jnp.sum(jnp.where(hit2, tot, 0.0), axis=-1, keepdims=True)
    rank_ref[...] = jnp.where(ln == 0, r1, jnp.where(ln == 1, r2, 0.0))
    new = carry[...] + jnp.sum(onehot, axis=0, keepdims=True)
    carry[...] = new
    count_ref[...] = new


def _ranks(route):
    n_total = route.shape[0]
    tm = ROW_TILE
    return pl.pallas_call(
        _rank_kernel,
        grid=(n_total // tm,),
        in_specs=[pl.BlockSpec((tm, LANES), lambda i: (i, 0))],
        out_specs=[pl.BlockSpec((tm, LANES), lambda i: (i, 0)),
                   pl.BlockSpec((SUBLANES, LANES), lambda i: (0, 0))],
        out_shape=[jax.ShapeDtypeStruct((n_total, LANES), F32),
                   jax.ShapeDtypeStruct((SUBLANES, LANES), F32)],
        scratch_shapes=[pltpu.VMEM((SUBLANES, LANES), F32)],
        compiler_params=_cparams(("arbitrary",)),
        name="slot_ranks",
    )(route)


def _scatter_kernel(pos_ref, h_ref, xs_in_ref, xs_ref, sem):
    del xs_in_ref
    n = h_ref.shape[0]

    def body(t, carry):
        for k in range(2):
            p = pos_ref[0, 0, 2 * t + k]
            pltpu.make_async_copy(h_ref.at[pl.ds(t, 1), :], xs_ref.at[pl.ds(p, 1), :], sem).start()
        return carry

    lax.fori_loop(0, n, body, 0)
    for _ in range(2):
        pltpu.make_async_copy(h_ref, xs_ref.at[pl.ds(0, n), :], sem).wait()


def _scatter_rows(pos3, h2, n_sorted):
    n_total = h2.shape[0]
    tm = ROW_TILE
    zeros = jnp.zeros((n_sorted, D_MODEL), F32)
    return pl.pallas_call(
        _scatter_kernel,
        grid=(n_total // tm,),
        in_specs=[pl.BlockSpec((1, 1, 2 * tm), lambda i: (i, 0, 0), memory_space=pltpu.SMEM),
                  pl.BlockSpec((tm, D_MODEL), lambda i: (i, 0)),
                  pl.BlockSpec(memory_space=pl.ANY)],
        out_specs=pl.BlockSpec(memory_space=pl.ANY),
        out_shape=jax.ShapeDtypeStruct((n_sorted, D_MODEL), F32),
        scratch_shapes=[pltpu.SemaphoreType.DMA(())],
        input_output_aliases={2: 0},
        compiler_params=_cparams(("arbitrary",)),
        name="moe_scatter",
    )(pos3, h2, zeros)


def _ffn_kernel(te_ref, tv_ref, x_ref, wg_ref, wu_ref, wd_ref, y_ref, wgb, wub, wdb):
    i = pl.program_id(0)
    prev = te_ref[jnp.maximum(i - 1, 0)]

    @pl.when(jnp.logical_or(i == 0, te_ref[i] != prev))
    def _():
        wgb[...] = wg_ref[0].astype(BF16)
        wub[...] = wu_ref[0].astype(BF16)
        wdb[...] = wd_ref[0].astype(BF16)

    @pl.when(tv_ref[i] > 0)
    def _():
        xb = x_ref[...].astype(BF16)
        a = _dot(xb, wgb[...])
        u = _dot(xb, wub[...])
        act = (_silu(a) * u).astype(BF16)
        y_ref[...] = _dot(act, wdb[...])

    @pl.when(tv_ref[i] == 0)
    def _():
        y_ref[...] = jnp.zeros(y_ref.shape, F32)


def _grouped_ffn(tile_expert, tile_valid, xs, w_gate, w_up, w_down):
    n_sorted = xs.shape[0]
    tm = MOE_TILE
    grid_spec = pltpu.PrefetchScalarGridSpec(
        num_scalar_prefetch=2,
        grid=(n_sorted // tm,),
        in_specs=[pl.BlockSpec((tm, D_MODEL), lambda i, te, tv: (i, 0)),
                  pl.BlockSpec((1, D_MODEL, D_FF), lambda i, te, tv: (te[i], 0, 0)),
                  pl.BlockSpec((1, D_MODEL, D_FF), lambda i, te, tv: (te[i], 0, 0)),
                  pl.BlockSpec((1, D_FF, D_MODEL), lambda i, te, tv: (te[i], 0, 0))],
        out_specs=pl.BlockSpec((tm, D_MODEL), lambda i, te, tv: (i, 0)),
        scratch_shapes=[pltpu.VMEM((D_MODEL, D_FF), BF16),
                        pltpu.VMEM((D_MODEL, D_FF), BF16),
                        pltpu.VMEM((D_FF, D_MODEL), BF16)],
    )
    return pl.pallas_call(
        _ffn_kernel,
        grid_spec=grid_spec,
        out_shape=jax.ShapeDtypeStruct((n_sorted, D_MODEL), F32),
        compiler_params=_cparams(("arbitrary",)),
        name="moe_ffn",
    )(tile_expert, tile_valid, xs, w_gate, w_up, w_down)


def _combine_kernel(pos_ref, route_ref, x1_ref, mod_ref, fw_ref, ys_ref, out_ref, buf, sem):
    n = x1_ref.shape[0]

    def body(t, carry):
        for k in range(2):
            p = pos_ref[0, 0, 2 * t + k]
            pltpu.make_async_copy(ys_ref.at[pl.ds(p, 1), :], buf.at[k, pl.ds(t, 1), :], sem).start()
        return carry

    lax.fori_loop(0, n, body, 0)
    for k in range(2):
        pltpu.make_async_copy(ys_ref.at[pl.ds(0, n), :], buf.at[k], sem).wait()
    route = route_ref[...]
    g1 = route[:, ROUTE_GATE_LANE:ROUTE_GATE_LANE + 1]
    g2 = route[:, ROUTE_GATE_LANE + 1:ROUTE_GATE_LANE + 2]
    moe = g1 * buf[0] + g2 * buf[1]
    x2 = x1_ref[...] + mod_ref[0][5:6] * moe
    out_ref[...] = x2 * _rms_scale(x2) * fw_ref[...]


def _combine(pos3, route, x1, mod, final_w, ys, n_rows, row_off, tokens_per_mod, mod_off):
    tm = ROW_TILE
    off = row_off // tm
    per = tokens_per_mod // tm
    return pl.pallas_call(
        _combine_kernel,
        grid=(n_rows // tm,),
        in_specs=[pl.BlockSpec((1, 1, 2 * tm), lambda i: (off + i, 0, 0), memory_space=pltpu.SMEM),
                  pl.BlockSpec((tm, LANES), lambda i: (off + i, 0)),
                  pl.BlockSpec((tm, D_MODEL), lambda i: (off + i, 0)),
                  pl.BlockSpec((1, N_MOD, D_MODEL), lambda i: (mod_off + i // per, 0, 0)),
                  pl.BlockSpec((1, D_MODEL), lambda i: (0, 0)),
                  pl.BlockSpec(memory_space=pl.ANY)],
        out_specs=pl.BlockSpec((tm, D_MODEL), lambda i: (i, 0)),
        out_shape=jax.ShapeDtypeStruct((n_rows, D_MODEL), F32),
        scratch_shapes=[pltpu.VMEM((2, tm, D_MODEL), F32), pltpu.SemaphoreType.DMA(())],
        compiler_params=_cparams(("arbitrary",)),
        name="moe_combine",
    )(pos3, route, x1, mod, final_w, ys)


def _grid_sincos_2d(n_tokens):
    rows = n_tokens // GRID_W
    quarter = D_MODEL // 4
    omega = 1.0 / (POS_BASE ** (jnp.arange(quarter, dtype=F32) / quarter))
    r = jnp.broadcast_to(jnp.arange(rows, dtype=F32)[:, None], (rows, GRID_W)).reshape(-1)
    c = jnp.broadcast_to(jnp.arange(GRID_W, dtype=F32)[None, :], (rows, GRID_W)).reshape(-1)
    ar = r[:, None] * omega
    ac = c[:, None] * omega
    return jnp.concatenate([jnp.sin(ar), jnp.cos(ar), jnp.sin(ac), jnp.cos(ac)], axis=-1)


def _pad_rows(a, n):
    return jnp.concatenate([a, jnp.zeros((n - a.shape[0],) + a.shape[1:], a.dtype)], axis=0)


def _selection_constants():
    gh = SSD_GROUP_HEADS
    ssd_sel = np.zeros((SSD_GROUPS, 2, LANES, LANES), np.float32)
    for g in range(SSD_GROUPS):
        for d in range(2):
            for j in range(gh):
                src = d * SSD_HEADS + g * gh + j
                ssd_sel[g, 0, src, d * gh + j] = 1.0
                ssd_sel[g, 1, src, 2 * gh + d * gh + j] = 1.0
    exp_dt = np.zeros((2, LANES, SSD_GROUP_COLS), np.float32)
    exp_a = np.zeros((2, LANES, SSD_GROUP_COLS), np.float32)
    for d in range(2):
        for c in range(SSD_GROUP_COLS):
            exp_dt[d, d * gh + c // SSD_HEAD_DIM, c] = 1.0
            exp_a[d, 2 * gh + d * gh + c // SSD_HEAD_DIM, c] = 1.0
    dn_sel = np.zeros((DN_HEADS, LANES, LANES), np.float32)
    base = 2 * SSD_HEADS
    for h in range(DN_HEADS):
        for d in range(2):
            dn_sel[h, base + d * DN_HEADS + h, d] = 1.0
            dn_sel[h, base + 2 * DN_HEADS + d * DN_HEADS + h, 2 + d] = 1.0
    return ssd_sel, exp_dt, exp_a, dn_sel


def kernel(x_prompt, x_sample, state_ssd_fwd, state_ssd_bwd, state_dn_fwd, state_dn_bwd, c, c_ctx, w_ada, b_ada, norm1_w, w_in, conv_ssd_w, conv_ssd_b, conv_dn_w, conv_dn_b, ssd_dt_bias, ssd_a_log, ssd_d, ssd_norm_w, dn_dt_bias, dn_a_log, dn_norm_w, w_out, norm2_w, w_router_group, b_router_group, w_router_expert, b_router_expert, w_gate, w_up, w_down, final_norm_w):
    n_b, seq, _ = x_prompt.shape
    dec_b, dec_seq, _ = x_sample.shape
    n_ctx = n_b * seq
    n_lat = dec_b * dec_seq
    n_total = n_ctx + n_lat
    layer = 0

    wi = w_in[layer]
    xbc0 = SSD_WIDTH
    bm0 = xbc0 + SSD_WIDTH
    cm0 = bm0 + SSD_GROUPS * SSD_STATE
    dt0 = cm0 + SSD_GROUPS * SSD_STATE
    qkv0 = dt0 + 2 * SSD_HEADS
    zd0 = qkv0 + 3 * DN_WIDTH
    ad0 = zd0 + DN_WIDTH
    bd0 = ad0 + 2 * DN_HEADS
    bc_cols = []
    for g in range(SSD_GROUPS):
        bc_cols += [wi[:, bm0 + g * SSD_STATE:bm0 + (g + 1) * SSD_STATE],
                    wi[:, cm0 + g * SSD_STATE:cm0 + (g + 1) * SSD_STATE]]
    w_r = jnp.concatenate(
        [wi[:, 0:SSD_WIDTH], wi[:, xbc0:bm0]] + bc_cols +
        [wi[:, qkv0:zd0], wi[:, zd0:ad0], wi[:, dt0:qkv0], wi[:, ad0:bd0], wi[:, bd0:bd0 + 2 * DN_HEADS],
         jnp.zeros((D_MODEL, PROJ_COLS - COL_SMALL - 2 * SSD_HEADS - 4 * DN_HEADS), F32)],
        axis=1).astype(BF16)

    cs_w = conv_ssd_w[layer]
    cs_b = conv_ssd_b[layer]
    bcw, bcb = [], []
    for g in range(SSD_GROUPS):
        for base in (SSD_WIDTH, SSD_WIDTH + SSD_GROUPS * SSD_STATE):
            sl = slice(base + g * SSD_STATE, base + (g + 1) * SSD_STATE)
            bcw.append(cs_w[:, sl])
            bcb.append(cs_b[sl])
    gate_params = jnp.zeros((SUBLANES, LANES), F32)
    gate_params = gate_params.at[0, 0:2 * SSD_HEADS].set(ssd_dt_bias[layer].reshape(-1))
    gate_params = gate_params.at[0, 2 * SSD_HEADS:2 * SSD_HEADS + 2 * DN_HEADS].set(dn_dt_bias[layer].reshape(-1))
    gate_params = gate_params.at[1, 0:2 * SSD_HEADS].set(ssd_a_log[layer].reshape(-1))
    gate_params = gate_params.at[1, 2 * SSD_HEADS:2 * SSD_HEADS + 2 * DN_HEADS].set(dn_a_log[layer].reshape(-1))
    ssd_sel, exp_dt, exp_a, dn_sel = _selection_constants()
    w_router = jnp.concatenate(
        [w_router_group[layer], w_router_expert[layer],
         jnp.zeros((D_MODEL, LANES - N_EXPERT_GROUPS - N_EXPERTS), F32)], axis=1)
    b_router = jnp.concatenate(
        [b_router_group[layer], b_router_expert[layer],
         jnp.zeros((LANES - N_EXPERT_GROUPS - N_EXPERTS,), F32)]).reshape(1, LANES)
    consts = {
        "cw_x": _pad_rows(cs_w[:, 0:SSD_WIDTH], SUBLANES),
        "cb_x": cs_b[0:SSD_WIDTH].reshape(1, -1),
        "cw_bc": _pad_rows(jnp.concatenate(bcw, axis=1), SUBLANES),
        "cb_bc": jnp.concatenate(bcb).reshape(1, -1),
        "cw_dn": _pad_rows(conv_dn_w[layer], SUBLANES),
        "cb_dn": conv_dn_b[layer].reshape(1, -1),
        "gate_params": gate_params,
        "d_exp": jnp.repeat(ssd_d[layer], SSD_HEAD_DIM).reshape(1, -1),
        "ssd_sel": jnp.asarray(ssd_sel),
        "ssd_exp_dt": jnp.asarray(exp_dt),
        "ssd_exp_a": jnp.asarray(exp_a),
        "dn_sel": jnp.asarray(dn_sel),
        "dn_norm_w": dn_norm_w[layer].reshape(1, -1),
        "ssd_norm_w": ssd_norm_w[layer].reshape(1, -1),
        "norm2_w": norm2_w[layer].reshape(1, -1),
        "w_out": w_out[layer].astype(BF16),
        "w_router": w_router,
        "b_router": b_router,
    }

    cond = _pad_rows(jnp.concatenate([c_ctx[None, :], c], axis=0), SUBLANES)
    mod = _adaln(cond, w_ada[layer], b_ada[layer]).reshape(SUBLANES, N_MOD, D_MODEL)

    pos = _grid_sincos_2d(dec_seq)
    n1w = norm1_w[layer].reshape(1, -1)
    proj, x0 = _inproj(x_prompt.reshape(n_ctx, D_MODEL), x_sample.reshape(n_lat, D_MODEL), pos, mod, n1w,
                       w_r, dec_seq)

    def ssd_state_in(s):
        return s[:, layer].transpose(0, 3, 1, 2).reshape(dec_b, SSD_STATE, SSD_WIDTH)

    yssd_ctx, hf, hb = _ssd(proj, consts, n_b, seq, 0, None, True)
    yssd_lat = _ssd(proj, consts, dec_b, dec_seq, n_ctx,
                    (ssd_state_in(state_ssd_fwd), ssd_state_in(state_ssd_bwd)), False)[0]
    ydn_ctx, sf, sb = _dn(proj, consts, n_b, seq, 0, None, True)
    ydn_lat = _dn(proj, consts, dec_b, dec_seq, n_ctx,
                  (state_dn_fwd[:, layer], state_dn_bwd[:, layer]), False)[0]

    x1, h2, route = _outproj(yssd_ctx, yssd_lat, ydn_ctx, ydn_lat, x0, mod, consts, dec_seq)

    rank, counts = _ranks(route)
    counts = counts[0, :N_EXPERTS].astype(jnp.int32)
    padded = ((counts + MOE_TILE - 1) // MOE_TILE) * MOE_TILE
    ends = jnp.cumsum(padded)
    offsets = ends - padded
    n_sorted = 2 * n_total + N_EXPERTS * MOE_TILE
    ids = route[:, ROUTE_ID_LANE:ROUTE_ID_LANE + 2].astype(jnp.int32)
    pos_slots = offsets[ids] + rank[:, 0:2].astype(jnp.int32)
    pos3 = pos_slots.reshape(n_total // ROW_TILE, 1, 2 * ROW_TILE)
    tile_start = jnp.arange(n_sorted // MOE_TILE, dtype=jnp.int32) * MOE_TILE
    tile_expert = jnp.minimum(jnp.searchsorted(ends, tile_start, side="right"), N_EXPERTS - 1).astype(jnp.int32)
    tile_valid = (tile_start < ends[-1]).astype(jnp.int32)

    xs = _scatter_rows(pos3, h2, n_sorted)
    ys = _grouped_ffn(tile_expert, tile_valid, xs, w_gate[layer], w_up[layer], w_down[layer])
    fw = final_norm_w.reshape(1, -1)
    y_prompt = _combine(pos3, route, x1, mod, fw, ys, n_ctx, 0, n_ctx, 0)
    y_sample = _combine(pos3, route, x1, mod, fw, ys, n_lat, n_ctx, dec_seq, 1)

    def ssd_state_out(s):
        return s.reshape(n_b, SSD_STATE, SSD_HEADS, SSD_HEAD_DIM).transpose(0, 2, 3, 1)[:, None]

    return (y_prompt.reshape(n_b, seq, D_MODEL), y_sample.reshape(dec_b, dec_seq, D_MODEL),
            ssd_state_out(hf), ssd_state_out(hb), sf[:, None], sb[:, None])
```

```python
import functools

import numpy as np
import jax
import jax.numpy as jnp
from jax import lax
from jax.experimental import pallas as pl
from jax.experimental.pallas import tpu as pltpu

F32 = jnp.float32
BF16 = jnp.bfloat16
HI = lax.Precision.HIGHEST

D_MODEL = 1024
EPS = 1e-6
GRID_W = 64
POS_BASE = 10000.0
CONV_W = 5
N_MOD = 6
SSD_HEADS = 16
SSD_HEAD_DIM = 64
SSD_WIDTH = SSD_HEADS * SSD_HEAD_DIM
SSD_GROUPS = 2
SSD_STATE = 64
SSD_CHUNK = 128
DN_HEADS = 8
DN_HEAD_DIM = 128
DN_WIDTH = DN_HEADS * DN_HEAD_DIM
DN_CHUNK = 64
N_EXPERT_GROUPS = 4
EXPERTS_PER_GROUP = 8
N_EXPERTS = N_EXPERT_GROUPS * EXPERTS_PER_GROUP
D_FF = 512

LANES = 128
SUBLANES = 8
VMEM_LIMIT = 56 * 1024 * 1024

COL_Z_S = 0
COL_X = 1024
COL_BC = 2048
COL_QKV = 2304
COL_Z_D = 5376
COL_SMALL = 6400
PROJ_COLS = 6528

ROW_TILE = 256
MOE_TILE = 256
NEG_BIG = -1e30


def _dot(a, b, prec=None):
    return jnp.dot(a, b, preferred_element_type=F32, precision=prec)


def _dot_nt(a, b, prec=None):
    return lax.dot_general(a, b, (((1,), (1,)), ((), ())), preferred_element_type=F32, precision=prec)


def _dot_tn(a, b, prec=None):
    return lax.dot_general(a, b, (((0,), (0,)), ((), ())), preferred_element_type=F32, precision=prec)


def _silu(x):
    return x * jax.nn.sigmoid(x)


def _softplus(x):
    return jnp.maximum(x, 0.0) + jnp.log1p(jnp.exp(-jnp.abs(x)))


def _rms_scale(x):
    return lax.rsqrt(jnp.mean(x * x, axis=-1, keepdims=True) + EPS)


def _cparams(sem, vmem=VMEM_LIMIT):
    return pltpu.CompilerParams(dimension_semantics=sem, vmem_limit_bytes=vmem)


def _ada_kernel(c_ref, w_ref, b_ref, o_ref):
    c = c_ref[...]
    o_ref[...] = _dot(_silu(c), w_ref[...], HI) + b_ref[...]


def _adaln(cond, w_ada, b_ada):
    n_out = N_MOD * D_MODEL
    tn = 1536
    return pl.pallas_call(
        _ada_kernel,
        grid=(n_out // tn,),
        in_specs=[pl.BlockSpec((SUBLANES, D_MODEL), lambda j: (0, 0)),
                  pl.BlockSpec((D_MODEL, tn), lambda j: (0, j)),
                  pl.BlockSpec((1, tn), lambda j: (0, j))],
        out_specs=pl.BlockSpec((SUBLANES, tn), lambda j: (0, j)),
        out_shape=jax.ShapeDtypeStruct((SUBLANES, n_out), F32),
        compiler_params=_cparams(("arbitrary",)),
        name="adaln",
    )(cond, w_ada, b_ada.reshape(1, n_out))


def _inproj_kernel(xc_ref, xl_ref, pos_ref, mod_ref, nw_ref, w_ref, proj_ref, x0_ref, *, ctx_tiles):
    is_ctx = pl.program_id(0) < ctx_tiles
    x = jnp.where(is_ctx, xc_ref[...], xl_ref[...] + pos_ref[...])
    x0_ref[...] = x
    mod = mod_ref[0]
    h = (x * _rms_scale(x) * nw_ref[...]) * (1.0 + mod[1:2]) + mod[0:1]
    proj_ref[...] = _dot(h.astype(BF16), w_ref[...])


def _stream_maps(ctx_tiles, lat_tiles_per_mod):
    def ctx_row(i):
        return (jnp.minimum(i, ctx_tiles - 1), 0)

    def lat_row(i):
        return (jnp.maximum(i - ctx_tiles, 0), 0)

    def mod_row(i):
        return (jnp.where(i < ctx_tiles, 0, 1 + (i - ctx_tiles) // lat_tiles_per_mod), 0, 0)

    return ctx_row, lat_row, mod_row


def _inproj(x_ctx, x_lat, pos, mod, norm_w, w_r, lat_seq):
    tm = ROW_TILE
    ctx_tiles = x_ctx.shape[0] // tm
    n_total = x_ctx.shape[0] + x_lat.shape[0]
    n_pos = pos.shape[0] // tm
    ctx_row, lat_row, mod_row = _stream_maps(ctx_tiles, lat_seq // tm)
    return pl.pallas_call(
        functools.partial(_inproj_kernel, ctx_tiles=ctx_tiles),
        grid=(n_total // tm,),
        in_specs=[pl.BlockSpec((tm, D_MODEL), ctx_row),
                  pl.BlockSpec((tm, D_MODEL), lat_row),
                  pl.BlockSpec((tm, D_MODEL), lambda i: (jnp.maximum(i - ctx_tiles, 0) % n_pos, 0)),
                  pl.BlockSpec((1, N_MOD, D_MODEL), mod_row),
                  pl.BlockSpec((1, D_MODEL), lambda i: (0, 0)),
                  pl.BlockSpec((D_MODEL, PROJ_COLS), lambda i: (0, 0), pipeline_mode=pl.Buffered(1))],
        out_specs=[pl.BlockSpec((tm, PROJ_COLS), lambda i: (i, 0)),
                   pl.BlockSpec((tm, D_MODEL), lambda i: (i, 0))],
        out_shape=[jax.ShapeDtypeStruct((n_total, PROJ_COLS), F32),
                   jax.ShapeDtypeStruct((n_total, D_MODEL), F32)],
        compiler_params=_cparams(("arbitrary",)),
        name="inproj",
    )(x_ctx, x_lat, pos, mod, norm_w, w_r)


CONV_ROWS = 128
CONV_HALO = SUBLANES


def _conv_silu(src_ref, w_ref, b_ref, pad_ref, store, blk_len, seg_len):
    n_ch = src_ref.shape[1]
    zeros = jnp.zeros((CONV_HALO, n_ch), F32)
    pad_ref[0:CONV_HALO, :] = zeros
    pad_ref[CONV_HALO + blk_len:2 * CONV_HALO + blk_len, :] = zeros

    def copy_body(i, carry):
        r0 = pl.multiple_of(i * CONV_ROWS, CONV_ROWS)
        pad_ref[pl.ds(CONV_HALO + r0, CONV_ROWS), :] = src_ref[pl.ds(r0, CONV_ROWS), :].astype(F32)
        return carry

    lax.fori_loop(0, blk_len // CONV_ROWS, copy_body, 0)
    win = CONV_ROWS + 2 * CONV_HALO
    w = w_ref[...]
    b = b_ref[...]
    row = lax.broadcasted_iota(jnp.int32, (CONV_ROWS, n_ch), 0)

    def body(i, carry):
        r0 = pl.multiple_of(i * CONV_ROWS, CONV_ROWS)
        seg_pos = r0 % seg_len
        at_start = seg_pos == 0
        at_end = seg_pos + CONV_ROWS == seg_len
        v = pad_ref[pl.ds(r0, win), :]
        acc = jnp.zeros((CONV_ROWS, n_ch), F32) + b
        for k in range(CONV_W):
            off = k - CONV_W // 2
            shift = (-off) % win
            sh = v if shift == 0 else pltpu.roll(v, shift, 0)
            tap = sh[CONV_HALO:CONV_HALO + CONV_ROWS, :]
            if off < 0:
                tap = jnp.where(jnp.logical_and(at_start, row < -off), 0.0, tap)
            elif off > 0:
                tap = jnp.where(jnp.logical_and(at_end, row >= CONV_ROWS - off), 0.0, tap)
            acc = acc + w[k:k + 1, :] * tap
        store(pl.ds(r0, CONV_ROWS), _silu(acc))
        return carry

    lax.fori_loop(0, blk_len // CONV_ROWS, body, 0)


SSD_GROUP_HEADS = SSD_HEADS // SSD_GROUPS
SSD_GROUP_COLS = SSD_GROUP_HEADS * SSD_HEAD_DIM


def _ssd_kernel(*refs, seq_len, has_init, emit_state):
    refs = list(refs)
    (x_ref, bc_ref, z_ref, sm_ref, cwx_ref, cbx_ref, cwbc_ref, cbbc_ref, gp_ref, dexp_ref,
     sel_ref, exp_dt_ref, exp_a_ref) = refs[:13]
    refs = refs[13:]
    if has_init:
        h0f_ref, h0b_ref = refs[:2]
        refs = refs[2:]
    y_ref = refs.pop(0)
    if emit_state:
        hf_ref, hb_ref = refs[:2]
        refs = refs[2:]
    xpad, bcpad, xc, bcc, dtg, yacc, hst = refs

    q = SSD_CHUNK
    n_chunks = seq_len // q

    def store_x(rows, val):
        xc[rows, :] = val
        yacc[rows, :] = val * dexp_ref[...]

    _conv_silu(x_ref, cwx_ref, cbx_ref, xpad, store_x, seq_len, seq_len)

    def store_bc(rows, val):
        bcc[rows, :] = val

    _conv_silu(bc_ref, cwbc_ref, cbbc_ref, bcpad, store_bc, seq_len, seq_len)

    gp = gp_ref[...]
    bias_row = gp[0:1, :]
    aneg_row = -jnp.exp(gp[1:2, :])
    sel_dt = sel_ref[0, 0]
    sel_a = sel_ref[0, 1]

    def gate_body(i, carry):
        r0 = pl.multiple_of(i * q, q)
        dt_full = _softplus(sm_ref[pl.ds(r0, q), :].astype(F32) + bias_row)
        dtg[pl.ds(r0, q), :] = (_dot(dt_full, sel_dt, HI) + _dot(dt_full * aneg_row, sel_a, HI))
        return carry

    lax.fori_loop(0, n_chunks, gate_body, 0)

    if has_init:
        hst[0] = h0f_ref[0]
        hst[1] = h0b_ref[0]
    else:
        hst[...] = jnp.zeros(hst.shape, F32)

    ri = lax.broadcasted_iota(jnp.int32, (q, q), 0)
    ci = lax.broadcasted_iota(jnp.int32, (q, q), 1)
    lane = lax.broadcasted_iota(jnp.int32, (q, LANES), 1)
    low_half = lane < SSD_HEAD_DIM

    def chunk(d, c):
        rows = pl.ds(pl.multiple_of(c * q, q), q)
        mask = (ri >= ci) if d == 0 else (ri <= ci)
        tri = mask.astype(F32)
        xck = xc[rows, :]
        bcck = bcc[rows, :]
        b_c = bcck[:, :SSD_STATE].astype(BF16)
        c_c = bcck[:, SSD_STATE:].astype(BF16)
        gt = dtg[rows, :]
        cum = _dot(tri, gt, HI)
        cum_t = cum.T
        end8 = cum[q - SUBLANES:q, :] if d == 0 else cum[0:SUBLANES, :]
        end_row = SUBLANES - 1 if d == 0 else 0
        dt_x = _dot(gt, exp_dt_ref[d], HI)
        ac_x = _dot(cum, exp_a_ref[d], HI)
        ae_x = _dot(end8, exp_a_ref[d], HI)[end_row:end_row + 1, :]
        xdt = xck * dt_x
        cb = _dot_nt(c_c, b_c)
        lane0 = 2 * SSD_GROUP_HEADS + d * SSD_GROUP_HEADS
        parts = []
        for hp in range(SSD_GROUP_HEADS // 2):
            ms = []
            for j in (2 * hp, 2 * hp + 1):
                col = cum[:, lane0 + j:lane0 + j + 1]
                row = cum_t[lane0 + j:lane0 + j + 1, :]
                dec = jnp.exp(jnp.where(mask, col - row, NEG_BIG))
                ms.append((cb * dec).astype(BF16))
            x2 = xdt[:, hp * LANES:(hp + 1) * LANES]
            w_lo = jnp.where(low_half, x2, 0.0)
            w_hi = jnp.where(low_half, 0.0, x2)
            parts.append(_dot(jnp.concatenate(ms, axis=1),
                              jnp.concatenate([w_lo, w_hi], axis=0).astype(BF16)))
        y_diag = jnp.concatenate(parts, axis=1)
        hs = hst[d]
        y_off = _dot(c_c, hs.astype(BF16)) * jnp.exp(ac_x)
        yacc[rows, :] += y_diag + y_off
        hst[d] = hs * jnp.exp(ae_x) + _dot_tn(b_c, (xdt * jnp.exp(ae_x - ac_x)).astype(BF16))

    def step(s, carry):
        chunk(0, s)
        chunk(1, n_chunks - 1 - s)
        return carry

    lax.fori_loop(0, n_chunks, step, 0)

    def out_body(i, carry):
        rows = pl.ds(pl.multiple_of(i * q, q), q)
        y_ref[rows, :] = yacc[rows, :] * _silu(z_ref[rows, :].astype(F32))
        return carry

    lax.fori_loop(0, n_chunks, out_body, 0)
    if emit_state:
        hf_ref[0] = hst[0]
        hb_ref[0] = hst[1]


def _ssd(proj, consts, n_seq, seq_len, row_off, init, emit_state):
    rb = row_off // seq_len
    cw = SSD_GROUP_COLS
    has_init = init is not None

    def col(block_cols, base):
        return base // block_cols

    in_specs = [
        pl.BlockSpec((seq_len, cw), lambda b, g: (rb + b, col(cw, COL_X) + g)),
        pl.BlockSpec((seq_len, LANES), lambda b, g: (rb + b, col(LANES, COL_BC) + g)),
        pl.BlockSpec((seq_len, cw), lambda b, g: (rb + b, col(cw, COL_Z_S) + g)),
        pl.BlockSpec((seq_len, LANES), lambda b, g: (rb + b, col(LANES, COL_SMALL))),
        pl.BlockSpec((SUBLANES, cw), lambda b, g: (0, g)),
        pl.BlockSpec((1, cw), lambda b, g: (0, g)),
        pl.BlockSpec((SUBLANES, LANES), lambda b, g: (0, g)),
        pl.BlockSpec((1, LANES), lambda b, g: (0, g)),
        pl.BlockSpec((SUBLANES, LANES), lambda b, g: (0, 0)),
        pl.BlockSpec((1, cw), lambda b, g: (0, g)),
        pl.BlockSpec((1, 2, LANES, LANES), lambda b, g: (g, 0, 0, 0)),
        pl.BlockSpec((2, LANES, cw), lambda b, g: (0, 0, 0)),
        pl.BlockSpec((2, LANES, cw), lambda b, g: (0, 0, 0)),
    ]
    args = [proj, proj, proj, proj, consts["cw_x"], consts["cb_x"], consts["cw_bc"], consts["cb_bc"],
            consts["gate_params"], consts["d_exp"], consts["ssd_sel"], consts["ssd_exp_dt"],
            consts["ssd_exp_a"]]
    if has_init:
        in_specs += [pl.BlockSpec((1, SSD_STATE, cw), lambda b, g: (b, 0, g))] * 2
        args += list(init)
    out_specs = [pl.BlockSpec((seq_len, cw), lambda b, g: (b, g))]
    out_shape = [jax.ShapeDtypeStruct((n_seq * seq_len, SSD_WIDTH), F32)]
    if emit_state:
        out_specs += [pl.BlockSpec((1, SSD_STATE, cw), lambda b, g: (b, 0, g))] * 2
        out_shape += [jax.ShapeDtypeStruct((n_seq, SSD_STATE, SSD_WIDTH), F32)] * 2
    scratch = [
        pltpu.VMEM((seq_len + 2 * CONV_HALO, cw), F32),
        pltpu.VMEM((seq_len + 2 * CONV_HALO, LANES), F32),
        pltpu.VMEM((seq_len, cw), F32),
        pltpu.VMEM((seq_len, LANES), F32),
        pltpu.VMEM((seq_len, LANES), F32),
        pltpu.VMEM((seq_len, cw), F32),
        pltpu.VMEM((2, SSD_STATE, cw), F32),
    ]
    return pl.pallas_call(
        functools.partial(_ssd_kernel, seq_len=seq_len, has_init=has_init, emit_state=emit_state),
        grid=(n_seq, SSD_GROUPS),
        in_specs=in_specs,
        out_specs=out_specs,
        out_shape=out_shape,
        scratch_shapes=scratch,
        compiler_params=_cparams(("arbitrary", "arbitrary")),
        name="ssd_scan",
    )(*args)


DN_UNIT = 2 * DN_CHUNK
DN_W2_ROWS = 2 * DN_CHUNK
DN_M2_ROWS = DN_CHUNK + DN_HEAD_DIM
DN_HB = 2
DN_UG = 4
DN_INST = DN_UG * DN_HB * 2
DN_GATE_LANES = 16


def _dn_kernel(*refs, blk_len, seg_len, has_init, emit_state):
    refs = list(refs)
    (q_ref, k_ref, v_ref, z_ref, sm_ref, cwq_ref, cbq_ref, cwk_ref, cbk_ref, cwv_ref, cbv_ref,
     gp_ref, nw_ref, sel_ref) = refs[:14]
    refs = refs[14:]
    if has_init:
        s0_refs = refs[:2]
        refs = refs[2:]
    y_ref = refs.pop(0)
    if emit_state:
        s_out_refs = refs[:2]
        refs = refs[2:]
    (pad, qs, ks, vs, gfull, cumf, cumb, oacc, sst, kk_s, qk_s, a_s, p_s, t_s, rhs_s,
     w2, m2, us, cds, vp_s, op_s) = refs

    cq = DN_CHUNK
    hd = DN_HEAD_DIM
    unit = DN_UNIT
    n_units = blk_len // unit
    seg_units = seg_len // unit
    run_units = min(seg_units, DN_UG)
    chains_per_dir = DN_UG // run_units
    scale = DN_HEAD_DIM ** -0.5

    def l2n(v):
        return v * lax.rsqrt(jnp.sum(v * v, axis=-1, keepdims=True) + EPS)

    def per_head(fn, val):
        return jnp.concatenate([fn(val[:, j * hd:(j + 1) * hd]) for j in range(DN_HB)], axis=1)

    def store_q(rows, val):
        qs[rows, :] = per_head(lambda v: l2n(v) * scale, val)

    def store_k(rows, val):
        ks[rows, :] = per_head(l2n, val)

    def store_v(rows, val):
        vs[rows, :] = val
        oacc[rows, :] = jnp.zeros_like(val)

    _conv_silu(q_ref, cwq_ref, cbq_ref, pad, store_q, blk_len, seg_len)
    _conv_silu(k_ref, cwk_ref, cbk_ref, pad, store_k, blk_len, seg_len)
    _conv_silu(v_ref, cwv_ref, cbv_ref, pad, store_v, blk_len, seg_len)

    ri = lax.broadcasted_iota(jnp.int32, (unit, unit), 0)
    ci = lax.broadcasted_iota(jnp.int32, (unit, unit), 1)
    same = (ri // cq) == (ci // cq)
    incl = [jnp.logical_and(same, ri >= ci), jnp.logical_and(same, ri <= ci)]
    strict = [jnp.logical_and(same, ri > ci), jnp.logical_and(same, ri < ci)]
    tri = [jnp.where(m, 1.0, 0.0).astype(BF16) for m in incl]
    in_chunk = [ri < cq, ri >= cq]
    eye = jnp.where(ri == ci, 1.0, 0.0)

    def level_mask(sz):
        return jnp.logical_and((ri // (2 * sz)) == (ci // (2 * sz)), (ri // sz) != (ci // sz))

    gp = gp_ref[...]
    bias_row = gp[0:1, :]
    aneg_row = -jnp.exp(gp[1:2, :])
    lane = lax.broadcasted_iota(jnp.int32, (CONV_ROWS, LANES), 1)
    is_decay_lane = lane < 2 * SSD_HEADS + 2 * DN_HEADS

    def sum_parts(m):
        return m + pltpu.roll(m, LANES - 4, 1) + pltpu.roll(m, LANES - 8, 1)

    def gate_body(i, carry):
        for sub in range(DN_UG):
            rows = pl.ds(pl.multiple_of((i * DN_UG + sub) * unit, unit), unit)
            sm = sm_ref[rows, :].astype(F32)
            comb = jnp.where(is_decay_lane, aneg_row * _softplus(sm + bias_row), jax.nn.sigmoid(sm))
            hi = comb.astype(BF16)
            rest = comb - hi.astype(F32)
            mid = rest.astype(BF16)
            lo = (rest - mid.astype(F32)).astype(BF16)
            g3 = _dot(hi, sel_ref[0, 0]) + _dot(mid, sel_ref[0, 1]) + _dot(lo, sel_ref[0, 2])
            g3b = g3.astype(BF16)
            gfull[rows, :] = sum_parts(g3)
            cumf[rows, :] = sum_parts(_dot(tri[0], g3b))
            cumb[rows, :] = sum_parts(_dot(tri[1], g3b))
        return carry

    lax.fori_loop(0, n_units // DN_UG, gate_body, 0)

    def chain_id(j, d, s):
        return (j * 2 + d) * chains_per_dir + s

    def init_state(j, d):
        return s0_refs[d][0, j] if has_init else jnp.zeros((hd, hd), F32)

    for j in range(DN_HB):
        for d in range(2):
            for s in range(chains_per_dir):
                sst[chain_id(j, d, s)] = init_state(j, d)

    def inst(ul, j, d):
        return (ul * DN_HB + j) * 2 + d

    def unit_of(k, ul, d):
        u = k * DN_UG + ul
        return u if d == 0 else n_units - 1 - u

    zero_half = jnp.zeros((cq, hd), BF16)

    def solve_batch(k):
        for ul in range(DN_UG):
            for d in range(2):
                rows = pl.ds(pl.multiple_of(unit_of(k, ul, d) * unit, unit), unit)
                for j in range(DN_HB):
                    hs = slice(j * hd, (j + 1) * hd)
                    knb = ks[rows, hs].astype(BF16)
                    i = inst(ul, j, d)
                    kk_s[i] = _dot_nt(knb, knb)
                    qk_s[i] = _dot_nt(qs[rows, hs].astype(BF16), knb)
        for ul in range(DN_UG):
            for d in range(2):
                rows = pl.ds(pl.multiple_of(unit_of(k, ul, d) * unit, unit), unit)
                gf = gfull[rows, :]
                cum = (cumf if d == 0 else cumb)[rows, :]
                cum_t = cum.T
                tot = cumf[rows, :] + cumb[rows, :] - gf
                for j in range(DN_HB):
                    hs = slice(j * hd, (j + 1) * hd)
                    lane0 = DN_GATE_LANES * j
                    i = inst(ul, j, d)
                    qn = qs[rows, hs]
                    kn = ks[rows, hs]
                    gcol = cum[:, lane0 + d:lane0 + d + 1]
                    grow = cum_t[lane0 + d:lane0 + d + 1, :]
                    bcol = gf[:, lane0 + 2 + d:lane0 + 3 + d]
                    gtot = tot[:, lane0 + d:lane0 + d + 1]
                    dec = jnp.exp(jnp.where(incl[d], gcol - grow, NEG_BIG))
                    a_mat = jnp.where(strict[d], kk_s[i] * dec * bcol, 0.0)
                    a_s[i] = a_mat.astype(BF16)
                    t_s[i] = eye - jnp.where(level_mask(1), a_mat, 0.0)
                    eg = jnp.exp(gcol)
                    rhs_s[i] = jnp.concatenate([vs[rows, hs] * bcol, kn * (bcol * eg)], axis=1)
                    qk = (qk_s[i] * dec).astype(BF16)
                    qd = (qn * eg).astype(BF16)
                    kd = kn * jnp.exp(gtot - gcol)
                    cd = jnp.exp(gtot)
                    for c in (0, 1):
                        rs = slice(c * cq, (c + 1) * cq)
                        w2[i, c * DN_W2_ROWS + cq:(c + 1) * DN_W2_ROWS, :] = qd[rs]
                        m2[i, c * DN_M2_ROWS:c * DN_M2_ROWS + cq, :] = qk[rs]
                        kd_t = jnp.where(in_chunk[c], kd, 0.0).T
                        m2[i, c * DN_M2_ROWS + cq:(c + 1) * DN_M2_ROWS, :] = kd_t.astype(BF16)
                        cds[i, c * SUBLANES:(c + 1) * SUBLANES, :] = jnp.broadcast_to(
                            cd[c * cq:c * cq + SUBLANES], (SUBLANES, hd))
        sz = 2
        while sz < cq:
            mask = level_mask(sz)
            for i in range(DN_INST):
                p_s[i] = _dot(t_s[i].astype(BF16), jnp.where(mask, a_s[i], 0.0)).astype(BF16)
            for i in range(DN_INST):
                t = t_s[i]
                t_s[i] = t - _dot(p_s[i], t.astype(BF16))
            sz *= 2
        for i in range(DN_INST):
            rhs = rhs_s[i]
            sol = rhs + _dot((t_s[i] - eye).astype(BF16), rhs.astype(BF16))
            us[i] = sol[:, :hd]
            w = sol[:, hd:].astype(BF16)
            for c in (0, 1):
                w2[i, c * DN_W2_ROWS:c * DN_W2_ROWS + cq, :] = w[c * cq:(c + 1) * cq]

    def recur_batch(k):
        for t in range(2 * run_units):
            plan = []
            for j in range(DN_HB):
                for d in range(2):
                    for s in range(chains_per_dir):
                        ul = s * run_units + t // 2
                        c = t % 2 if d == 0 else 1 - t % 2
                        plan.append((chain_id(j, d, s), inst(ul, j, d), j, d, c,
                                     unit_of(k, ul, d) * unit + c * cq))
            for ch, i, j, d, c, pos in plan:
                first = (pos % seg_len == 0) if d == 0 else ((pos + cq) % seg_len == 0)
                s_val = jnp.where(first, init_state(j, d), sst[ch])
                sst[ch] = s_val
                tt = _dot(w2[i, c * DN_W2_ROWS:(c + 1) * DN_W2_ROWS, :], s_val.astype(BF16))
                v_new = (us[i, c * cq:(c + 1) * cq, :] - tt[0:cq]).astype(BF16)
                vp_s[ch] = jnp.concatenate([v_new, zero_half] if c == 0 else [zero_half, v_new], axis=0)
                op_s[ch] = tt[cq:2 * cq]
            for ch, i, j, d, c, pos in plan:
                t2 = _dot(m2[i, c * DN_M2_ROWS:(c + 1) * DN_M2_ROWS, :], vp_s[ch])
                orows = pl.ds(pl.multiple_of(pos, cq), cq)
                oacc[orows, j * hd:(j + 1) * hd] += op_s[ch] + t2[0:cq]
                s_new = sst[ch] * cds[i, c * SUBLANES:c * SUBLANES + 1, :] + t2[cq:cq + hd]
                sst[ch] = s_new
                if emit_state and t == 2 * run_units - 1:
                    s_out_refs[d][pos // seg_len, j] = s_new

    def batch(k, carry):
        solve_batch(k)
        recur_batch(k)
        return carry

    lax.fori_loop(0, n_units // DN_UG, batch, 0)

    def out_body(i, carry):
        rows = pl.ds(pl.multiple_of(i * CONV_ROWS, CONV_ROWS), CONV_ROWS)
        for j in range(DN_HB):
            hs = slice(j * hd, (j + 1) * hd)
            o = oacc[rows, hs]
            y_ref[rows, hs] = o * _rms_scale(o) * nw_ref[...] * _silu(z_ref[rows, hs].astype(F32))
        return carry

    lax.fori_loop(0, blk_len // CONV_ROWS, out_body, 0)


def _dn(proj, consts, n_blk, blk_len, seg_len, row_off, init, emit_state):
    rb = row_off // blk_len
    hd = DN_HEAD_DIM
    has_init = init is not None
    n_seg = blk_len // seg_len
    assert CONV_ROWS == DN_UNIT and (blk_len // DN_UNIT) % DN_UG == 0
    seg_units = seg_len // DN_UNIT
    assert DN_UG % seg_units == 0 or seg_units % DN_UG == 0
    n_chains = DN_HB * 2 * (DN_UG // min(seg_units, DN_UG))
    bw = DN_HB * hd
    n_hp = DN_HEADS // DN_HB
    cq0 = COL_QKV // bw
    in_specs = [
        pl.BlockSpec((blk_len, bw), lambda b, h: (rb + b, cq0 + h)),
        pl.BlockSpec((blk_len, bw), lambda b, h: (rb + b, cq0 + n_hp + h)),
        pl.BlockSpec((blk_len, bw), lambda b, h: (rb + b, cq0 + 2 * n_hp + h)),
        pl.BlockSpec((blk_len, bw), lambda b, h: (rb + b, COL_Z_D // bw + h)),
        pl.BlockSpec((blk_len, LANES), lambda b, h: (rb + b, COL_SMALL // LANES)),
        pl.BlockSpec((SUBLANES, bw), lambda b, h: (0, h)),
        pl.BlockSpec((1, bw), lambda b, h: (0, h)),
        pl.BlockSpec((SUBLANES, bw), lambda b, h: (0, n_hp + h)),
        pl.BlockSpec((1, bw), lambda b, h: (0, n_hp + h)),
        pl.BlockSpec((SUBLANES, bw), lambda b, h: (0, 2 * n_hp + h)),
        pl.BlockSpec((1, bw), lambda b, h: (0, 2 * n_hp + h)),
        pl.BlockSpec((SUBLANES, LANES), lambda b, h: (0, 0)),
        pl.BlockSpec((1, hd), lambda b, h: (0, 0)),
        pl.BlockSpec((1, 3, LANES, LANES), lambda b, h: (h, 0, 0, 0)),
    ]
    args = [proj, proj, proj, proj, proj,
            consts["cw_dn"], consts["cb_dn"], consts["cw_dn"], consts["cb_dn"], consts["cw_dn"],
            consts["cb_dn"], consts["gate_params"], consts["dn_norm_w"], consts["dn_sel"]]
    if has_init:
        assert n_seg == 1
        in_specs += [pl.BlockSpec((1, DN_HB, hd, hd), lambda b, h: (b, h, 0, 0))] * 2
        args += list(init)
    out_specs = [pl.BlockSpec((blk_len, bw), lambda b, h: (b, h))]
    out_shape = [jax.ShapeDtypeStruct((n_blk * blk_len, DN_WIDTH), F32)]
    if emit_state:
        out_specs += [pl.BlockSpec((n_seg, DN_HB, hd, hd), lambda b, h: (b, h, 0, 0))] * 2
        out_shape += [jax.ShapeDtypeStruct((n_blk * n_seg, DN_HEADS, hd, hd), F32)] * 2
    scratch = [
        pltpu.VMEM((blk_len + 2 * CONV_HALO, bw), F32),
        pltpu.VMEM((blk_len, bw), F32),
        pltpu.VMEM((blk_len, bw), F32),
        pltpu.VMEM((blk_len, bw), F32),
        pltpu.VMEM((blk_len, LANES), F32),
        pltpu.VMEM((blk_len, LANES), F32),
        pltpu.VMEM((blk_len, LANES), F32),
        pltpu.VMEM((blk_len, bw), F32),
        pltpu.VMEM((n_chains, hd, hd), F32),
        pltpu.VMEM((DN_INST, DN_UNIT, DN_UNIT), F32),
        pltpu.VMEM((DN_INST, DN_UNIT, DN_UNIT), F32),
        pltpu.VMEM((DN_INST, DN_UNIT, DN_UNIT), BF16),
        pltpu.VMEM((DN_INST, DN_UNIT, DN_UNIT), BF16),
        pltpu.VMEM((DN_INST, DN_UNIT, DN_UNIT), F32),
        pltpu.VMEM((DN_INST, DN_UNIT, 2 * hd), F32),
        pltpu.VMEM((DN_INST, 2 * DN_W2_ROWS, hd), BF16),
        pltpu.VMEM((DN_INST, 2 * DN_M2_ROWS, hd), BF16),
        pltpu.VMEM((DN_INST, DN_UNIT, hd), F32),
        pltpu.VMEM((DN_INST, 2 * SUBLANES, hd), F32),
        pltpu.VMEM((n_chains, DN_UNIT, hd), BF16),
        pltpu.VMEM((n_chains, DN_CHUNK, hd), F32),
    ]
    return pl.pallas_call(
        functools.partial(_dn_kernel, blk_len=blk_len, seg_len=seg_len, has_init=has_init,
                          emit_state=emit_state),
        grid=(n_blk, n_hp),
        in_specs=in_specs,
        out_specs=out_specs,
        out_shape=out_shape,
        scratch_shapes=scratch,
        compiler_params=_cparams(("arbitrary", "arbitrary")),
        name="dn_scan",
    )(*args)


ROUTE_ID_LANE = 0
ROUTE_GATE_LANE = 2


def _outproj_kernel(ysc_ref, ysl_ref, ydc_ref, ydl_ref, x0_ref, mod_ref, snw_ref, n2w_ref, wo_ref,
                    wr_ref, br_ref, x1_ref, h2_ref, route_ref, *, ctx_tiles):
    is_ctx = pl.program_id(0) < ctx_tiles
    ys = jnp.where(is_ctx, ysc_ref[...], ysl_ref[...])
    ysn = (ys * _rms_scale(ys) * snw_ref[...]).astype(BF16)
    ydn = jnp.where(is_ctx, ydc_ref[...], ydl_ref[...]).astype(BF16)
    m = _dot(ysn, wo_ref[0:SSD_WIDTH, :]) + _dot(ydn, wo_ref[SSD_WIDTH:, :])
    mod = mod_ref[0]
    x1 = x0_ref[...] + mod[2:3] * m
    x1_ref[...] = x1
    h2 = (x1 * _rms_scale(x1) * n2w_ref[...]) * (1.0 + mod[4:5]) + mod[3:4]
    h2_ref[...] = h2
    logits = _dot(h2, wr_ref[...], HI) + br_ref[...]
    ln = lax.broadcasted_iota(jnp.int32, logits.shape, 1)
    is_g = ln < N_EXPERT_GROUPS
    gl = jnp.where(is_g, logits, NEG_BIG)
    gmax = jnp.max(gl, axis=-1, keepdims=True)
    gidx = jnp.min(jnp.where(gl == gmax, ln, LANES), axis=-1, keepdims=True)
    gw = 1.0 / jnp.sum(jnp.where(is_g, jnp.exp(gl - gmax), 0.0), axis=-1, keepdims=True)
    lo = N_EXPERT_GROUPS + gidx * EXPERTS_PER_GROUP
    in_grp = (ln >= lo) & (ln < lo + EXPERTS_PER_GROUP)
    el = jnp.where(in_grp, logits, NEG_BIG)
    m1 = jnp.max(el, axis=-1, keepdims=True)
    i1 = jnp.min(jnp.where(el == m1, ln, LANES), axis=-1, keepdims=True)
    el2 = jnp.where(ln == i1, NEG_BIG, el)
    m2 = jnp.max(el2, axis=-1, keepdims=True)
    i2 = jnp.min(jnp.where(el2 == m2, ln, LANES), axis=-1, keepdims=True)
    t = jnp.exp(m2 - m1)
    g1 = gw / (1.0 + t)
    g2 = gw * t / (1.0 + t)
    id1 = (i1 - N_EXPERT_GROUPS).astype(F32)
    id2 = (i2 - N_EXPERT_GROUPS).astype(F32)
    route = jnp.where(ln == 0, id1, jnp.where(ln == 1, id2, jnp.where(ln == 2, g1,
                                                                      jnp.where(ln == 3, g2, 0.0))))
    route_ref[...] = route


def _outproj(yssd_ctx, yssd_lat, ydn_ctx, ydn_lat, x0, mod, consts, lat_seq):
    n_total = x0.shape[0]
    tm = ROW_TILE
    ctx_tiles = yssd_ctx.shape[0] // tm
    ctx_row, lat_row, mod_row = _stream_maps(ctx_tiles, lat_seq // tm)
    row = lambda i: (i, 0)
    const2 = lambda i: (0, 0)
    return pl.pallas_call(
        functools.partial(_outproj_kernel, ctx_tiles=ctx_tiles),
        grid=(n_total // tm,),
        in_specs=[pl.BlockSpec((tm, SSD_WIDTH), ctx_row),
                  pl.BlockSpec((tm, SSD_WIDTH), lat_row),
                  pl.BlockSpec((tm, DN_WIDTH), ctx_row),
                  pl.BlockSpec((tm, DN_WIDTH), lat_row),
                  pl.BlockSpec((tm, D_MODEL), row),
                  pl.BlockSpec((1, N_MOD, D_MODEL), mod_row),
                  pl.BlockSpec((1, SSD_WIDTH), const2),
                  pl.BlockSpec((1, D_MODEL), const2),
                  pl.BlockSpec((SSD_WIDTH + DN_WIDTH, D_MODEL), const2),
                  pl.BlockSpec((D_MODEL, LANES), const2),
                  pl.BlockSpec((1, LANES), const2)],
        out_specs=[pl.BlockSpec((tm, D_MODEL), row),
                   pl.BlockSpec((tm, D_MODEL), row),
                   pl.BlockSpec((tm, LANES), row)],
        out_shape=[jax.ShapeDtypeStruct((n_total, D_MODEL), F32),
                   jax.ShapeDtypeStruct((n_total, D_MODEL), F32),
                   jax.ShapeDtypeStruct((n_total, LANES), F32)],
        compiler_params=_cparams(("arbitrary",)),
        name="outproj_router",
    )(yssd_ctx, yssd_lat, ydn_ctx, ydn_lat, x0, mod, consts["ssd_norm_w"], consts["norm2_w"],
      consts["w_out"], consts["w_router"], consts["b_router"])


def _rank_kernel(route_ref, rank_ref, count_ref, carry):
    i = pl.program_id(0)

    @pl.when(i == 0)
    def _():
        carry[...] = jnp.zeros(carry.shape, F32)

    route = route_ref[...]
    n = route.shape[0]
    ln = lax.broadcasted_iota(jnp.int32, route.shape, 1)
    id1 = route[:, 0:1].astype(jnp.int32)
    id2 = route[:, 1:2].astype(jnp.int32)
    hit1 = ln == id1
    hit2 = ln == id2
    onehot = jnp.where(hit1, 1.0, jnp.where(hit2, 1.0, 0.0))
    ri = lax.broadcasted_iota(jnp.int32, (n, n), 0)
    ci = lax.broadcasted_iota(jnp.int32, (n, n), 1)
    before = (ri > ci).astype(BF16)
    tot = _dot(before, onehot.astype(BF16)) + carry[0:1, :]
    r1 = jnp.sum(jnp.where(hit1, tot, 0.0), axis=-1, keepdims=True)
    r2 = jnp.sum(jnp.where(hit2, tot, 0.0), axis=-1, keepdims=True)
    rank_ref[...] = jnp.where(ln == 0, r1, jnp.where(ln == 1, r2, 0.0))
    new = carry[...] + jnp.sum(onehot, axis=0, keepdims=True)
    carry[...] = new
    count_ref[...] = new


def _ranks(route):
    n_total = route.shape[0]
    tm = ROW_TILE
    return pl.pallas_call(
        _rank_kernel,
        grid=(n_total // tm,),
        in_specs=[pl.BlockSpec((tm, LANES), lambda i: (i, 0))],
        out_specs=[pl.BlockSpec((tm, LANES), lambda i: (i, 0)),
                   pl.BlockSpec((SUBLANES, LANES), lambda i: (0, 0))],
        out_shape=[jax.ShapeDtypeStruct((n_total, LANES), F32),
                   jax.ShapeDtypeStruct((SUBLANES, LANES), F32)],
        scratch_shapes=[pltpu.VMEM((SUBLANES, LANES), F32)],
        compiler_params=_cparams(("arbitrary",)),
        name="slot_ranks",
    )(route)


def _scatter_kernel(pos_ref, h_ref, xs_in_ref, xs_ref, sem):
    del xs_in_ref
    n = h_ref.shape[0]

    def body(t, carry):
        for k in range(2):
            p = pos_ref[0, 0, 2 * t + k]
            pltpu.make_async_copy(h_ref.at[pl.ds(t, 1), :], xs_ref.at[pl.ds(p, 1), :], sem).start()
        return carry

    lax.fori_loop(0, n, body, 0)
    for _ in range(2):
        pltpu.make_async_copy(h_ref, xs_ref.at[pl.ds(0, n), :], sem).wait()


def _scatter_rows(pos3, h2, n_sorted):
    n_total = h2.shape[0]
    tm = ROW_TILE
    zeros = jnp.zeros((n_sorted, D_MODEL), F32)
    return pl.pallas_call(
        _scatter_kernel,
        grid=(n_total // tm,),
        in_specs=[pl.BlockSpec((1, 1, 2 * tm), lambda i: (i, 0, 0), memory_space=pltpu.SMEM),
                  pl.BlockSpec((tm, D_MODEL), lambda i: (i, 0)),
                  pl.BlockSpec(memory_space=pl.ANY)],
        out_specs=pl.BlockSpec(memory_space=pl.ANY),
        out_shape=jax.ShapeDtypeStruct((n_sorted, D_MODEL), F32),
        scratch_shapes=[pltpu.SemaphoreType.DMA(())],
        input_output_aliases={2: 0},
        compiler_params=_cparams(("arbitrary",)),
        name="moe_scatter",
    )(pos3, h2, zeros)


def _ffn_kernel(te_ref, tv_ref, x_ref, wg_ref, wu_ref, wd_ref, y_ref, wgb, wub, wdb):
    i = pl.program_id(0)
    prev = te_ref[jnp.maximum(i - 1, 0)]

    @pl.when(jnp.logical_or(i == 0, te_ref[i] != prev))
    def _():
        wgb[...] = wg_ref[0].astype(BF16)
        wub[...] = wu_ref[0].astype(BF16)
        wdb[...] = wd_ref[0].astype(BF16)

    @pl.when(tv_ref[i] > 0)
    def _():
        xb = x_ref[...].astype(BF16)
        a = _dot(xb, wgb[...])
        u = _dot(xb, wub[...])
        act = (_silu(a) * u).astype(BF16)
        y_ref[...] = _dot(act, wdb[...])

    @pl.when(tv_ref[i] == 0)
    def _():
        y_ref[...] = jnp.zeros(y_ref.shape, F32)


def _grouped_ffn(tile_expert, tile_valid, xs, w_gate, w_up, w_down):
    n_sorted = xs.shape[0]
    tm = MOE_TILE
    grid_spec = pltpu.PrefetchScalarGridSpec(
        num_scalar_prefetch=2,
        grid=(n_sorted // tm,),
        in_specs=[pl.BlockSpec((tm, D_MODEL), lambda i, te, tv: (i, 0)),
                  pl.BlockSpec((1, D_MODEL, D_FF), lambda i, te, tv: (te[i], 0, 0)),
                  pl.BlockSpec((1, D_MODEL, D_FF), lambda i, te, tv: (te[i], 0, 0)),
                  pl.BlockSpec((1, D_FF, D_MODEL), lambda i, te, tv: (te[i], 0, 0))],
        out_specs=pl.BlockSpec((tm, D_MODEL), lambda i, te, tv: (i, 0)),
        scratch_shapes=[pltpu.VMEM((D_MODEL, D_FF), BF16),
                        pltpu.VMEM((D_MODEL, D_FF), BF16),
                        pltpu.VMEM((D_FF, D_MODEL), BF16)],
    )
    return pl.pallas_call(
        _ffn_kernel,
        grid_spec=grid_spec,
        out_shape=jax.ShapeDtypeStruct((n_sorted, D_MODEL), F32),
        compiler_params=_cparams(("arbitrary",)),
        name="moe_ffn",
    )(tile_expert, tile_valid, xs, w_gate, w_up, w_down)


def _combine_kernel(pos_ref, route_ref, x1_ref, mod_ref, fw_ref, ys_ref, out_ref, buf, sem):
    n = x1_ref.shape[0]

    def body(t, carry):
        for k in range(2):
            p = pos_ref[0, 0, 2 * t + k]
            pltpu.make_async_copy(ys_ref.at[pl.ds(p, 1), :], buf.at[k, pl.ds(t, 1), :], sem).start()
        return carry

    lax.fori_loop(0, n, body, 0)
    for k in range(2):
        pltpu.make_async_copy(ys_ref.at[pl.ds(0, n), :], buf.at[k], sem).wait()
    route = route_ref[...]
    g1 = route[:, ROUTE_GATE_LANE:ROUTE_GATE_LANE + 1]
    g2 = route[:, ROUTE_GATE_LANE + 1:ROUTE_GATE_LANE + 2]
    moe = g1 * buf[0] + g2 * buf[1]
    x2 = x1_ref[...] + mod_ref[0][5:6] * moe
    out_ref[...] = x2 * _rms_scale(x2) * fw_ref[...]


def _combine(pos3, route, x1, mod, final_w, ys, n_rows, row_off, tokens_per_mod, mod_off):
    tm = ROW_TILE
    off = row_off // tm
    per = tokens_per_mod // tm
    return pl.pallas_call(
        _combine_kernel,
        grid=(n_rows // tm,),
        in_specs=[pl.BlockSpec((1, 1, 2 * tm), lambda i: (off + i, 0, 0), memory_space=pltpu.SMEM),
                  pl.BlockSpec((tm, LANES), lambda i: (off + i, 0)),
                  pl.BlockSpec((tm, D_MODEL), lambda i: (off + i, 0)),
                  pl.BlockSpec((1, N_MOD, D_MODEL), lambda i: (mod_off + i // per, 0, 0)),
                  pl.BlockSpec((1, D_MODEL), lambda i: (0, 0)),
                  pl.BlockSpec(memory_space=pl.ANY)],
        out_specs=pl.BlockSpec((tm, D_MODEL), lambda i: (i, 0)),
        out_shape=jax.ShapeDtypeStruct((n_rows, D_MODEL), F32),
        scratch_shapes=[pltpu.VMEM((2, tm, D_MODEL), F32), pltpu.SemaphoreType.DMA(())],
        compiler_params=_cparams(("arbitrary",)),
        name="moe_combine",
    )(pos3, route, x1, mod, final_w, ys)


def _grid_sincos_2d(n_tokens):
    rows = n_tokens // GRID_W
    quarter = D_MODEL // 4
    omega = 1.0 / (POS_BASE ** (jnp.arange(quarter, dtype=F32) / quarter))
    r = jnp.broadcast_to(jnp.arange(rows, dtype=F32)[:, None], (rows, GRID_W)).reshape(-1)
    c = jnp.broadcast_to(jnp.arange(GRID_W, dtype=F32)[None, :], (rows, GRID_W)).reshape(-1)
    ar = r[:, None] * omega
    ac = c[:, None] * omega
    return jnp.concatenate([jnp.sin(ar), jnp.cos(ar), jnp.sin(ac), jnp.cos(ac)], axis=-1)


def _pad_rows(a, n):
    return jnp.concatenate([a, jnp.zeros((n - a.shape[0],) + a.shape[1:], a.dtype)], axis=0)


def _selection_constants():
    gh = SSD_GROUP_HEADS
    ssd_sel = np.zeros((SSD_GROUPS, 2, LANES, LANES), np.float32)
    for g in range(SSD_GROUPS):
        for d in range(2):
            for j in range(gh):
                src = d * SSD_HEADS + g * gh + j
                ssd_sel[g, 0, src, d * gh + j] = 1.0
                ssd_sel[g, 1, src, 2 * gh + d * gh + j] = 1.0
    exp_dt = np.zeros((2, LANES, SSD_GROUP_COLS), np.float32)
    exp_a = np.zeros((2, LANES, SSD_GROUP_COLS), np.float32)
    for d in range(2):
        for c in range(SSD_GROUP_COLS):
            exp_dt[d, d * gh + c // SSD_HEAD_DIM, c] = 1.0
            exp_a[d, 2 * gh + d * gh + c // SSD_HEAD_DIM, c] = 1.0
    dn_sel = np.zeros((DN_HEADS // DN_HB, 3, LANES, LANES), np.float32)
    base = 2 * SSD_HEADS
    for h in range(DN_HEADS):
        dst = DN_GATE_LANES * (h % DN_HB)
        for part in range(3):
            for d in range(2):
                dn_sel[h // DN_HB, part, base + d * DN_HEADS + h, dst + 4 * part + d] = 1.0
                dn_sel[h // DN_HB, part, base + 2 * DN_HEADS + d * DN_HEADS + h, dst + 4 * part + 2 + d] = 1.0
    return ssd_sel, exp_dt, exp_a, dn_sel


def _dn_consts(conv_w, conv_b, ssd_dt_bias, ssd_a_log, dn_dt_bias, dn_a_log, dn_norm_w):
    n_ssd = 2 * SSD_HEADS
    n_dn = 2 * DN_HEADS
    gate_params = jnp.zeros((SUBLANES, LANES), F32)
    gate_params = gate_params.at[0, 0:n_ssd].set(ssd_dt_bias.reshape(-1))
    gate_params = gate_params.at[0, n_ssd:n_ssd + n_dn].set(dn_dt_bias.reshape(-1))
    gate_params = gate_params.at[1, 0:n_ssd].set(ssd_a_log.reshape(-1))
    gate_params = gate_params.at[1, n_ssd:n_ssd + n_dn].set(dn_a_log.reshape(-1))
    dn_sel = _selection_constants()[3]
    return {
        "cw_dn": _pad_rows(conv_w, SUBLANES),
        "cb_dn": conv_b.reshape(1, -1),
        "gate_params": gate_params,
        "dn_sel": jnp.asarray(dn_sel, dtype=BF16),
        "dn_norm_w": dn_norm_w.reshape(1, -1),
    }


def kernel(x_prompt, x_sample, state_ssd_fwd, state_ssd_bwd, state_dn_fwd, state_dn_bwd, c, c_ctx, w_ada, b_ada, norm1_w, w_in, conv_ssd_w, conv_ssd_b, conv_dn_w, conv_dn_b, ssd_dt_bias, ssd_a_log, ssd_d, ssd_norm_w, dn_dt_bias, dn_a_log, dn_norm_w, w_out, norm2_w, w_router_group, b_router_group, w_router_expert, b_router_expert, w_gate, w_up, w_down, final_norm_w):
    n_b, seq, _ = x_prompt.shape
    dec_b, dec_seq, _ = x_sample.shape
    n_ctx = n_b * seq
    n_lat = dec_b * dec_seq
    n_total = n_ctx + n_lat
    layer = 0

    wi = w_in[layer]
    xbc0 = SSD_WIDTH
    bm0 = xbc0 + SSD_WIDTH
    cm0 = bm0 + SSD_GROUPS * SSD_STATE
    dt0 = cm0 + SSD_GROUPS * SSD_STATE
    qkv0 = dt0 + 2 * SSD_HEADS
    zd0 = qkv0 + 3 * DN_WIDTH
    ad0 = zd0 + DN_WIDTH
    bd0 = ad0 + 2 * DN_HEADS
    bc_cols = []
    for g in range(SSD_GROUPS):
        bc_cols += [wi[:, bm0 + g * SSD_STATE:bm0 + (g + 1) * SSD_STATE],
                    wi[:, cm0 + g * SSD_STATE:cm0 + (g + 1) * SSD_STATE]]
    w_r = jnp.concatenate(
        [wi[:, 0:SSD_WIDTH], wi[:, xbc0:bm0]] + bc_cols +
        [wi[:, qkv0:zd0], wi[:, zd0:ad0], wi[:, dt0:qkv0], wi[:, ad0:bd0], wi[:, bd0:bd0 + 2 * DN_HEADS],
         jnp.zeros((D_MODEL, PROJ_COLS - COL_SMALL - 2 * SSD_HEADS - 4 * DN_HEADS), F32)],
        axis=1).astype(BF16)

    cs_w = conv_ssd_w[layer]
    cs_b = conv_ssd_b[layer]
    bcw, bcb = [], []
    for g in range(SSD_GROUPS):
        for base in (SSD_WIDTH, SSD_WIDTH + SSD_GROUPS * SSD_STATE):
            sl = slice(base + g * SSD_STATE, base + (g + 1) * SSD_STATE)
            bcw.append(cs_w[:, sl])
            bcb.append(cs_b[sl])
    ssd_sel, exp_dt, exp_a, _ = _selection_constants()
    w_router = jnp.concatenate(
        [w_router_group[layer], w_router_expert[layer],
         jnp.zeros((D_MODEL, LANES - N_EXPERT_GROUPS - N_EXPERTS), F32)], axis=1)
    b_router = jnp.concatenate(
        [b_router_group[layer], b_router_expert[layer],
         jnp.zeros((LANES - N_EXPERT_GROUPS - N_EXPERTS,), F32)]).reshape(1, LANES)
    consts = {
        "cw_x": _pad_rows(cs_w[:, 0:SSD_WIDTH], SUBLANES),
        "cb_x": cs_b[0:SSD_WIDTH].reshape(1, -1),
        "cw_bc": _pad_rows(jnp.concatenate(bcw, axis=1), SUBLANES),
        "cb_bc": jnp.concatenate(bcb).reshape(1, -1),
        "d_exp": jnp.repeat(ssd_d[layer], SSD_HEAD_DIM).reshape(1, -1),
        "ssd_sel": jnp.asarray(ssd_sel),
        "ssd_exp_dt": jnp.asarray(exp_dt),
        "ssd_exp_a": jnp.asarray(exp_a),
        "ssd_norm_w": ssd_norm_w[layer].reshape(1, -1),
        "norm2_w": norm2_w[layer].reshape(1, -1),
        "w_out": w_out[layer].astype(BF16),
        "w_router": w_router,
        "b_router": b_router,
    }
    consts.update(_dn_consts(conv_dn_w[layer], conv_dn_b[layer], ssd_dt_bias[layer], ssd_a_log[layer],
                             dn_dt_bias[layer], dn_a_log[layer], dn_norm_w[layer]))

    cond = _pad_rows(jnp.concatenate([c_ctx[None, :], c], axis=0), SUBLANES)
    mod = _adaln(cond, w_ada[layer], b_ada[layer]).reshape(SUBLANES, N_MOD, D_MODEL)

    pos = _grid_sincos_2d(dec_seq)
    n1w = norm1_w[layer].reshape(1, -1)
    proj, x0 = _inproj(x_prompt.reshape(n_ctx, D_MODEL), x_sample.reshape(n_lat, D_MODEL), pos, mod, n1w,
                       w_r, dec_seq)

    def ssd_state_in(s):
        return s[:, layer].transpose(0, 3, 1, 2).reshape(dec_b, SSD_STATE, SSD_WIDTH)

    yssd_ctx, hf, hb = _ssd(proj, consts, n_b, seq, 0, None, True)
    yssd_lat = _ssd(proj, consts, dec_b, dec_seq, n_ctx,
                    (ssd_state_in(state_ssd_fwd), ssd_state_in(state_ssd_bwd)), False)[0]
    ydn_ctx, sf, sb = _dn(proj, consts, n_ctx // dec_seq, dec_seq, seq, 0, None, True)
    ydn_lat = _dn(proj, consts, dec_b, dec_seq, dec_seq, n_ctx,
                  (state_dn_fwd[:, layer], state_dn_bwd[:, layer]), False)[0]

    x1, h2, route = _outproj(yssd_ctx, yssd_lat, ydn_ctx, ydn_lat, x0, mod, consts, dec_seq)

    rank, counts = _ranks(route)
    counts = counts[0, :N_EXPERTS].astype(jnp.int32)
    padded = ((counts + MOE_TILE - 1) // MOE_TILE) * MOE_TILE
    ends = jnp.cumsum(padded)
    offsets = ends - padded
    n_sorted = 2 * n_total + N_EXPERTS * MOE_TILE
    ids = route[:, ROUTE_ID_LANE:ROUTE_ID_LANE + 2].astype(jnp.int32)
    pos_slots = offsets[ids] + rank[:, 0:2].astype(jnp.int32)
    pos3 = pos_slots.reshape(n_total // ROW_TILE, 1, 2 * ROW_TILE)
    tile_start = jnp.arange(n_sorted // MOE_TILE, dtype=jnp.int32) * MOE_TILE
    tile_expert = jnp.minimum(jnp.searchsorted(ends, tile_start, side="right"), N_EXPERTS - 1).astype(jnp.int32)
    tile_valid = (tile_start < ends[-1]).astype(jnp.int32)

    xs = _scatter_rows(pos3, h2, n_sorted)
    ys = _grouped_ffn(tile_expert, tile_valid, xs, w_gate[layer], w_up[layer], w_down[layer])
    fw = final_norm_w.reshape(1, -1)
    y_prompt = _combine(pos3, route, x1, mod, fw, ys, n_ctx, 0, n_ctx, 0)
    y_sample = _combine(pos3, route, x1, mod, fw, ys, n_lat, n_ctx, dec_seq, 1)

    def ssd_state_out(s):
        return s.reshape(n_b, SSD_STATE, SSD_HEADS, SSD_HEAD_DIM).transpose(0, 2, 3, 1)[:, None]

    return (y_prompt.reshape(n_b, seq, D_MODEL), y_sample.reshape(dec_b, dec_seq, D_MODEL),
            ssd_state_out(hf), ssd_state_out(hb), sf[:, None], sb[:, None])
```

```python
import functools

import numpy as np
import jax
import jax.numpy as jnp
from jax import lax
from jax.experimental import pallas as pl
from jax.experimental.pallas import tpu as pltpu

F32 = jnp.float32
BF16 = jnp.bfloat16
HI = lax.Precision.HIGHEST

D_MODEL = 1024
EPS = 1e-6
GRID_W = 64
POS_BASE = 10000.0
CONV_W = 5
N_MOD = 6
SSD_HEADS = 16
SSD_HEAD_DIM = 64
SSD_WIDTH = SSD_HEADS * SSD_HEAD_DIM
SSD_GROUPS = 2
SSD_STATE = 64
SSD_CHUNK = 128
DN_HEADS = 8
DN_HEAD_DIM = 128
DN_WIDTH = DN_HEADS * DN_HEAD_DIM
DN_CHUNK = 64
N_EXPERT_GROUPS = 4
EXPERTS_PER_GROUP = 8
N_EXPERTS = N_EXPERT_GROUPS * EXPERTS_PER_GROUP
D_FF = 512

LANES = 128
SUBLANES = 8
VMEM_LIMIT = 56 * 1024 * 1024

COL_Z_S = 0
COL_X = 1024
COL_BC = 2048
COL_QKV = 2304
COL_Z_D = 5376
COL_SMALL = 6400
PROJ_COLS = 6528

ROW_TILE = 256
MOE_TILE = 256
DMA_UNROLL = 8
NEG_BIG = -1e30


def _dot(a, b, prec=None):
    return jnp.dot(a, b, preferred_element_type=F32, precision=prec)


def _dot_nt(a, b, prec=None):
    return lax.dot_general(a, b, (((1,), (1,)), ((), ())), preferred_element_type=F32, precision=prec)


def _dot_tn(a, b, prec=None):
    return lax.dot_general(a, b, (((0,), (0,)), ((), ())), preferred_element_type=F32, precision=prec)


def _silu(x):
    return x * jax.nn.sigmoid(x)


def _softplus(x):
    return jnp.maximum(x, 0.0) + jnp.log1p(jnp.exp(-jnp.abs(x)))


def _rms_scale(x):
    return lax.rsqrt(jnp.mean(x * x, axis=-1, keepdims=True) + EPS)


PACK_COLS = D_MODEL // 2
HIGH_HALF = 0xFFFF0000


def _pack_rows(x):
    lo = pltpu.bitcast(x[:, :PACK_COLS].astype(BF16).astype(F32), jnp.uint32)
    hi = pltpu.bitcast(x[:, PACK_COLS:].astype(BF16).astype(F32), jnp.uint32)
    return (lo >> 16) | (hi & jnp.uint32(HIGH_HALF))


def _unpack_rows(u):
    return pltpu.bitcast(u << 16, F32), pltpu.bitcast(u & jnp.uint32(HIGH_HALF), F32)


def _cparams(sem, vmem=VMEM_LIMIT):
    return pltpu.CompilerParams(dimension_semantics=sem, vmem_limit_bytes=vmem)


def _ada_kernel(c_ref, w_ref, b_ref, o_ref):
    c = c_ref[...]
    o_ref[...] = _dot(_silu(c), w_ref[...], HI) + b_ref[...]


def _adaln(cond, w_ada, b_ada):
    n_out = N_MOD * D_MODEL
    tn = 1536
    return pl.pallas_call(
        _ada_kernel,
        grid=(n_out // tn,),
        in_specs=[pl.BlockSpec((SUBLANES, D_MODEL), lambda j: (0, 0)),
                  pl.BlockSpec((D_MODEL, tn), lambda j: (0, j)),
                  pl.BlockSpec((1, tn), lambda j: (0, j))],
        out_specs=pl.BlockSpec((SUBLANES, tn), lambda j: (0, j)),
        out_shape=jax.ShapeDtypeStruct((SUBLANES, n_out), F32),
        compiler_params=_cparams(("arbitrary",)),
        name="adaln",
    )(cond, w_ada, b_ada.reshape(1, n_out))


def _inproj_kernel(xc_ref, xl_ref, pos_ref, mod_ref, nw_ref, w_ref, proj_ref, x0_ref, *, ctx_tiles):
    is_ctx = pl.program_id(0) < ctx_tiles
    x = jnp.where(is_ctx, xc_ref[...], xl_ref[...] + pos_ref[...])
    x0_ref[...] = x
    mod = mod_ref[0]
    h = (x * _rms_scale(x) * nw_ref[...]) * (1.0 + mod[1:2]) + mod[0:1]
    proj_ref[...] = _dot(h.astype(BF16), w_ref[...])


def _stream_maps(ctx_tiles, lat_tiles_per_mod):
    def ctx_row(i):
        return (jnp.minimum(i, ctx_tiles - 1), 0)

    def lat_row(i):
        return (jnp.maximum(i - ctx_tiles, 0), 0)

    def mod_row(i):
        return (jnp.where(i < ctx_tiles, 0, 1 + (i - ctx_tiles) // lat_tiles_per_mod), 0, 0)

    return ctx_row, lat_row, mod_row


def _inproj(x_ctx, x_lat, pos, mod, norm_w, w_r, lat_seq):
    tm = ROW_TILE
    ctx_tiles = x_ctx.shape[0] // tm
    n_total = x_ctx.shape[0] + x_lat.shape[0]
    n_pos = pos.shape[0] // tm
    ctx_row, lat_row, mod_row = _stream_maps(ctx_tiles, lat_seq // tm)
    return pl.pallas_call(
        functools.partial(_inproj_kernel, ctx_tiles=ctx_tiles),
        grid=(n_total // tm,),
        in_specs=[pl.BlockSpec((tm, D_MODEL), ctx_row),
                  pl.BlockSpec((tm, D_MODEL), lat_row),
                  pl.BlockSpec((tm, D_MODEL), lambda i: (jnp.maximum(i - ctx_tiles, 0) % n_pos, 0)),
                  pl.BlockSpec((1, N_MOD, D_MODEL), mod_row),
                  pl.BlockSpec((1, D_MODEL), lambda i: (0, 0)),
                  pl.BlockSpec((D_MODEL, PROJ_COLS), lambda i: (0, 0), pipeline_mode=pl.Buffered(1))],
        out_specs=[pl.BlockSpec((tm, PROJ_COLS), lambda i: (i, 0)),
                   pl.BlockSpec((tm, D_MODEL), lambda i: (i, 0))],
        out_shape=[jax.ShapeDtypeStruct((n_total, PROJ_COLS), F32),
                   jax.ShapeDtypeStruct((n_total, D_MODEL), F32)],
        compiler_params=_cparams(("arbitrary",)),
        name="inproj",
    )(x_ctx, x_lat, pos, mod, norm_w, w_r)


CONV_ROWS = 128
CONV_HALO = SUBLANES


def _conv_silu(src_ref, w_ref, b_ref, pad_ref, store, blk_len, seg_len):
    n_ch = src_ref.shape[1]
    zeros = jnp.zeros((CONV_HALO, n_ch), F32)
    pad_ref[0:CONV_HALO, :] = zeros
    pad_ref[CONV_HALO + blk_len:2 * CONV_HALO + blk_len, :] = zeros

    def copy_body(i, carry):
        r0 = pl.multiple_of(i * CONV_ROWS, CONV_ROWS)
        pad_ref[pl.ds(CONV_HALO + r0, CONV_ROWS), :] = src_ref[pl.ds(r0, CONV_ROWS), :].astype(F32)
        return carry

    lax.fori_loop(0, blk_len // CONV_ROWS, copy_body, 0)
    win = CONV_ROWS + 2 * CONV_HALO
    w = w_ref[...]
    b = b_ref[...]
    row = lax.broadcasted_iota(jnp.int32, (CONV_ROWS, n_ch), 0)

    def body(i, carry):
        r0 = pl.multiple_of(i * CONV_ROWS, CONV_ROWS)
        seg_pos = r0 % seg_len
        at_start = seg_pos == 0
        at_end = seg_pos + CONV_ROWS == seg_len
        v = pad_ref[pl.ds(r0, win), :]
        acc = jnp.zeros((CONV_ROWS, n_ch), F32) + b
        for k in range(CONV_W):
            off = k - CONV_W // 2
            shift = (-off) % win
            sh = v if shift == 0 else pltpu.roll(v, shift, 0)
            tap = sh[CONV_HALO:CONV_HALO + CONV_ROWS, :]
            if off < 0:
                tap = jnp.where(jnp.logical_and(at_start, row < -off), 0.0, tap)
            elif off > 0:
                tap = jnp.where(jnp.logical_and(at_end, row >= CONV_ROWS - off), 0.0, tap)
            acc = acc + w[k:k + 1, :] * tap
        store(pl.ds(r0, CONV_ROWS), _silu(acc))
        return carry

    lax.fori_loop(0, blk_len // CONV_ROWS, body, 0)


SSD_GROUP_HEADS = SSD_HEADS // SSD_GROUPS
SSD_GROUP_COLS = SSD_GROUP_HEADS * SSD_HEAD_DIM
SSD_GATE_LANES = 2 * SSD_GROUP_HEADS


def _ssd_kernel(*refs, seq_len, has_init, emit_state):
    refs = list(refs)
    (x_ref, bc_ref, z_ref, sm_ref, cwx_ref, cbx_ref, cwbc_ref, cbbc_ref, gp_ref, dexp_ref,
     sel_ref, exp_ref) = refs[:12]
    refs = refs[12:]
    if has_init:
        h0f_ref, h0b_ref = refs[:2]
        refs = refs[2:]
    y_ref = refs.pop(0)
    if emit_state:
        hf_ref, hb_ref = refs[:2]
        refs = refs[2:]
    xpad, bcpad, xc, bcc, dt3, da3, yacc, hst = refs

    q = SSD_CHUNK
    n_chunks = seq_len // q
    n_gl = SSD_GATE_LANES
    lane = lax.broadcasted_iota(jnp.int32, (q, LANES), 1)
    gate_lane = lane < n_gl

    def sum_pieces(m):
        return m + pltpu.roll(m, LANES - n_gl, 1) + pltpu.roll(m, LANES - 2 * n_gl, 1)

    def pack_pieces(v):
        v = jnp.where(gate_lane, v, 0.0)
        hi = v.astype(BF16).astype(F32)
        rest = v - hi
        mid = rest.astype(BF16).astype(F32)
        lo = rest - mid
        return (hi + pltpu.roll(mid, n_gl, 1) + pltpu.roll(lo, 2 * n_gl, 1)).astype(BF16)

    def store_x(rows, val):
        xc[rows, :] = val
        yacc[rows, :] = val * dexp_ref[...]

    _conv_silu(x_ref, cwx_ref, cbx_ref, xpad, store_x, seq_len, seq_len)

    def store_bc(rows, val):
        bcc[rows, :] = val

    _conv_silu(bc_ref, cwbc_ref, cbbc_ref, bcpad, store_bc, seq_len, seq_len)

    gp = gp_ref[...]
    bias_row = gp[0:1, :]
    aneg_sel = _dot(-jnp.exp(gp), sel_ref[0, 0].astype(F32), HI)[1:2, :]

    def gate_body(i, carry):
        rows = pl.ds(pl.multiple_of(i * q, q), q)
        dt_full = _softplus(sm_ref[rows, :].astype(F32) + bias_row)
        hi = dt_full.astype(BF16)
        rest = dt_full - hi.astype(F32)
        mid = rest.astype(BF16)
        lo = (rest - mid.astype(F32)).astype(BF16)
        dt_sel = sum_pieces(_dot(hi, sel_ref[0, 0]) + _dot(mid, sel_ref[0, 1]) + _dot(lo, sel_ref[0, 2]))
        dt3[rows, :] = pack_pieces(dt_sel)
        da3[rows, :] = pack_pieces(dt_sel * aneg_sel)
        return carry

    lax.fori_loop(0, n_chunks, gate_body, 0)

    if has_init:
        hst[0] = h0f_ref[0]
        hst[1] = h0b_ref[0]
    else:
        hst[...] = jnp.zeros(hst.shape, F32)

    ri = lax.broadcasted_iota(jnp.int32, (q, q), 0)
    ci = lax.broadcasted_iota(jnp.int32, (q, q), 1)
    low_half = lane < SSD_HEAD_DIM
    masks = [ri >= ci, ri <= ci]
    tris = [jnp.where(m, 1.0, 0.0).astype(BF16) for m in masks]

    def chunk(d, c):
        rows = pl.ds(pl.multiple_of(c * q, q), q)
        mask = masks[d]
        xck = xc[rows, :]
        bcck = bcc[rows, :]
        b_c = bcck[:, :SSD_STATE].astype(BF16)
        c_c = bcck[:, SSD_STATE:].astype(BF16)
        cum = sum_pieces(_dot(tris[d], da3[rows, :]))
        cum_t = cum.T
        dt_x = _dot(dt3[rows, :], exp_ref[d])
        ac_x = _dot(pack_pieces(cum), exp_ref[d])
        ae_x = ac_x[q - 1:q, :] if d == 0 else ac_x[0:1, :]
        xdt = xck * dt_x
        cb = _dot_nt(c_c, b_c)
        lane0 = d * SSD_GROUP_HEADS
        parts = []
        for hp in range(SSD_GROUP_HEADS // 2):
            ms = []
            for j in (2 * hp, 2 * hp + 1):
                col = cum[:, lane0 + j:lane0 + j + 1]
                row = cum_t[lane0 + j:lane0 + j + 1, :]
                dec = jnp.exp(jnp.where(mask, col - row, NEG_BIG))
                ms.append((cb * dec).astype(BF16))
            x2 = xdt[:, hp * LANES:(hp + 1) * LANES]
            w_lo = jnp.where(low_half, x2, 0.0)
            w_hi = jnp.where(low_half, 0.0, x2)
            parts.append(_dot(jnp.concatenate(ms, axis=1),
                              jnp.concatenate([w_lo, w_hi], axis=0).astype(BF16)))
        y_diag = jnp.concatenate(parts, axis=1)
        hs = hst[d]
        y_off = _dot(c_c, hs.astype(BF16)) * jnp.exp(ac_x)
        yacc[rows, :] += y_diag + y_off
        hst[d] = hs * jnp.exp(ae_x) + _dot_tn(b_c, (xdt * jnp.exp(ae_x - ac_x)).astype(BF16))

    def step(s, carry):
        chunk(0, s)
        chunk(1, n_chunks - 1 - s)
        return carry

    lax.fori_loop(0, n_chunks, step, 0)

    def out_body(i, carry):
        rows = pl.ds(pl.multiple_of(i * q, q), q)
        y_ref[rows, :] = yacc[rows, :] * _silu(z_ref[rows, :].astype(F32))
        return carry

    lax.fori_loop(0, n_chunks, out_body, 0)
    if emit_state:
        hf_ref[0] = hst[0]
        hb_ref[0] = hst[1]


def _ssd(proj, consts, n_seq, seq_len, row_off, init, emit_state):
    rb = row_off // seq_len
    cw = SSD_GROUP_COLS
    has_init = init is not None

    def col(block_cols, base):
        return base // block_cols

    in_specs = [
        pl.BlockSpec((seq_len, cw), lambda b, g: (rb + b, col(cw, COL_X) + g)),
        pl.BlockSpec((seq_len, LANES), lambda b, g: (rb + b, col(LANES, COL_BC) + g)),
        pl.BlockSpec((seq_len, cw), lambda b, g: (rb + b, col(cw, COL_Z_S) + g)),
        pl.BlockSpec((seq_len, LANES), lambda b, g: (rb + b, col(LANES, COL_SMALL))),
        pl.BlockSpec((SUBLANES, cw), lambda b, g: (0, g)),
        pl.BlockSpec((1, cw), lambda b, g: (0, g)),
        pl.BlockSpec((SUBLANES, LANES), lambda b, g: (0, g)),
        pl.BlockSpec((1, LANES), lambda b, g: (0, g)),
        pl.BlockSpec((SUBLANES, LANES), lambda b, g: (0, 0)),
        pl.BlockSpec((1, cw), lambda b, g: (0, g)),
        pl.BlockSpec((1, 3, LANES, LANES), lambda b, g: (g, 0, 0, 0)),
        pl.BlockSpec((2, LANES, cw), lambda b, g: (0, 0, 0)),
    ]
    args = [proj, proj, proj, proj, consts["cw_x"], consts["cb_x"], consts["cw_bc"], consts["cb_bc"],
            consts["gate_params"], consts["d_exp"], consts["ssd_sel"], consts["ssd_exp"]]
    if has_init:
        in_specs += [pl.BlockSpec((1, SSD_STATE, cw), lambda b, g: (b, 0, g))] * 2
        args += list(init)
    out_specs = [pl.BlockSpec((seq_len, cw), lambda b, g: (b, g))]
    out_shape = [jax.ShapeDtypeStruct((n_seq * seq_len, SSD_WIDTH), F32)]
    if emit_state:
        out_specs += [pl.BlockSpec((1, SSD_STATE, cw), lambda b, g: (b, 0, g))] * 2
        out_shape += [jax.ShapeDtypeStruct((n_seq, SSD_STATE, SSD_WIDTH), F32)] * 2
    scratch = [
        pltpu.VMEM((seq_len + 2 * CONV_HALO, cw), F32),
        pltpu.VMEM((seq_len + 2 * CONV_HALO, LANES), F32),
        pltpu.VMEM((seq_len, cw), F32),
        pltpu.VMEM((seq_len, LANES), F32),
        pltpu.VMEM((seq_len, LANES), BF16),
        pltpu.VMEM((seq_len, LANES), BF16),
        pltpu.VMEM((seq_len, cw), F32),
        pltpu.VMEM((2, SSD_STATE, cw), F32),
    ]
    return pl.pallas_call(
        functools.partial(_ssd_kernel, seq_len=seq_len, has_init=has_init, emit_state=emit_state),
        grid=(n_seq, SSD_GROUPS),
        in_specs=in_specs,
        out_specs=out_specs,
        out_shape=out_shape,
        scratch_shapes=scratch,
        compiler_params=_cparams(("arbitrary", "arbitrary")),
        name="ssd_scan",
    )(*args)


DN_UNIT = 2 * DN_CHUNK
DN_W2_ROWS = 2 * DN_CHUNK
DN_M2_ROWS = DN_CHUNK + DN_HEAD_DIM
DN_HB = 2
DN_UG = 4
DN_INST = DN_UG * DN_HB * 2
DN_GATE_LANES = 16


def _dn_kernel(*refs, blk_len, seg_len, has_init, emit_state):
    refs = list(refs)
    (q_ref, k_ref, v_ref, z_ref, sm_ref, cwq_ref, cbq_ref, cwk_ref, cbk_ref, cwv_ref, cbv_ref,
     gp_ref, nw_ref, sel_ref) = refs[:14]
    refs = refs[14:]
    if has_init:
        s0_refs = refs[:2]
        refs = refs[2:]
    y_ref = refs.pop(0)
    if emit_state:
        s_out_refs = refs[:2]
        refs = refs[2:]
    (pad, qs, ks, vs, gfull, cumf, cumb, oacc, sst, kk_s, qk_s, a_s, p_s, t_s, rhs_s,
     w2, m2, us, cds, vp_s, op_s) = refs

    cq = DN_CHUNK
    hd = DN_HEAD_DIM
    unit = DN_UNIT
    n_units = blk_len // unit
    seg_units = seg_len // unit
    run_units = min(seg_units, DN_UG)
    chains_per_dir = DN_UG // run_units
    scale = DN_HEAD_DIM ** -0.5

    def l2n(v):
        return v * lax.rsqrt(jnp.sum(v * v, axis=-1, keepdims=True) + EPS)

    def per_head(fn, val):
        return jnp.concatenate([fn(val[:, j * hd:(j + 1) * hd]) for j in range(DN_HB)], axis=1)

    def store_q(rows, val):
        qs[rows, :] = per_head(lambda v: l2n(v) * scale, val)

    def store_k(rows, val):
        ks[rows, :] = per_head(l2n, val)

    def store_v(rows, val):
        vs[rows, :] = val
        oacc[rows, :] = jnp.zeros_like(val)

    _conv_silu(q_ref, cwq_ref, cbq_ref, pad, store_q, blk_len, seg_len)
    _conv_silu(k_ref, cwk_ref, cbk_ref, pad, store_k, blk_len, seg_len)
    _conv_silu(v_ref, cwv_ref, cbv_ref, pad, store_v, blk_len, seg_len)

    ri = lax.broadcasted_iota(jnp.int32, (unit, unit), 0)
    ci = lax.broadcasted_iota(jnp.int32, (unit, unit), 1)
    same = (ri // cq) == (ci // cq)
    incl = [jnp.logical_and(same, ri >= ci), jnp.logical_and(same, ri <= ci)]
    strict = [jnp.logical_and(same, ri > ci), jnp.logical_and(same, ri < ci)]
    tri = [jnp.where(m, 1.0, 0.0).astype(BF16) for m in incl]
    in_chunk = [ri < cq, ri >= cq]
    eye = jnp.where(ri == ci, 1.0, 0.0)

    def level_mask(sz):
        return jnp.logical_and((ri // (2 * sz)) == (ci // (2 * sz)), (ri // sz) != (ci // sz))

    gp = gp_ref[...]
    bias_row = gp[0:1, :]
    aneg_row = -jnp.exp(gp[1:2, :])
    lane = lax.broadcasted_iota(jnp.int32, (CONV_ROWS, LANES), 1)
    is_decay_lane = lane < 2 * SSD_HEADS + 2 * DN_HEADS

    def sum_parts(m):
        return m + pltpu.roll(m, LANES - 4, 1) + pltpu.roll(m, LANES - 8, 1)

    def gate_body(i, carry):
        for sub in range(DN_UG):
            rows = pl.ds(pl.multiple_of((i * DN_UG + sub) * unit, unit), unit)
            sm = sm_ref[rows, :].astype(F32)
            comb = jnp.where(is_decay_lane, aneg_row * _softplus(sm + bias_row), jax.nn.sigmoid(sm))
            hi = comb.astype(BF16)
            rest = comb - hi.astype(F32)
            mid = rest.astype(BF16)
            lo = (rest - mid.astype(F32)).astype(BF16)
            g3 = _dot(hi, sel_ref[0, 0]) + _dot(mid, sel_ref[0, 1]) + _dot(lo, sel_ref[0, 2])
            g3b = g3.astype(BF16)
            gfull[rows, :] = sum_parts(g3)
            cumf[rows, :] = sum_parts(_dot(tri[0], g3b))
            cumb[rows, :] = sum_parts(_dot(tri[1], g3b))
        return carry

    lax.fori_loop(0, n_units // DN_UG, gate_body, 0)

    def chain_id(j, d, s):
        return (j * 2 + d) * chains_per_dir + s

    def init_state(j, d):
        return s0_refs[d][0, j] if has_init else jnp.zeros((hd, hd), F32)

    for j in range(DN_HB):
        for d in range(2):
            for s in range(chains_per_dir):
                sst[chain_id(j, d, s)] = init_state(j, d)

    def inst(ul, j, d):
        return (ul * DN_HB + j) * 2 + d

    def unit_of(k, ul, d):
        u = k * DN_UG + ul
        return u if d == 0 else n_units - 1 - u

    zero_half = jnp.zeros((cq, hd), BF16)

    def solve_batch(k):
        for ul in range(DN_UG):
            for d in range(2):
                rows = pl.ds(pl.multiple_of(unit_of(k, ul, d) * unit, unit), unit)
                for j in range(DN_HB):
                    hs = slice(j * hd, (j + 1) * hd)
                    knb = ks[rows, hs].astype(BF16)
                    i = inst(ul, j, d)
                    kk_s[i] = _dot_nt(knb, knb)
                    qk_s[i] = _dot_nt(qs[rows, hs].astype(BF16), knb)
        for ul in range(DN_UG):
            for d in range(2):
                rows = pl.ds(pl.multiple_of(unit_of(k, ul, d) * unit, unit), unit)
                gf = gfull[rows, :]
                cum = (cumf if d == 0 else cumb)[rows, :]
                cum_t = cum.T
                tot = cumf[rows, :] + cumb[rows, :] - gf
                for j in range(DN_HB):
                    hs = slice(j * hd, (j + 1) * hd)
                    lane0 = DN_GATE_LANES * j
                    i = inst(ul, j, d)
                    qn = qs[rows, hs]
                    kn = ks[rows, hs]
                    gcol = cum[:, lane0 + d:lane0 + d + 1]
                    grow = cum_t[lane0 + d:lane0 + d + 1, :]
                    bcol = gf[:, lane0 + 2 + d:lane0 + 3 + d]
                    gtot = tot[:, lane0 + d:lane0 + d + 1]
                    dec = jnp.exp(jnp.where(incl[d], gcol - grow, NEG_BIG))
                    a_mat = jnp.where(strict[d], kk_s[i] * dec * bcol, 0.0)
                    a_s[i] = a_mat.astype(BF16)
                    t_s[i] = eye - jnp.where(level_mask(1), a_mat, 0.0)
                    eg = jnp.exp(gcol)
                    rhs_s[i] = jnp.concatenate([vs[rows, hs] * bcol, kn * (bcol * eg)], axis=1)
                    qk = (qk_s[i] * dec).astype(BF16)
                    qd = (qn * eg).astype(BF16)
                    kd = kn * jnp.exp(gtot - gcol)
                    cd = jnp.exp(gtot)
                    for c in (0, 1):
                        rs = slice(c * cq, (c + 1) * cq)
                        w2[i, c * DN_W2_ROWS + cq:(c + 1) * DN_W2_ROWS, :] = qd[rs]
                        m2[i, c * DN_M2_ROWS:c * DN_M2_ROWS + cq, :] = qk[rs]
                        kd_t = jnp.where(in_chunk[c], kd, 0.0).T
                        m2[i, c * DN_M2_ROWS + cq:(c + 1) * DN_M2_ROWS, :] = kd_t.astype(BF16)
                        cds[i, c * SUBLANES:(c + 1) * SUBLANES, :] = jnp.broadcast_to(
                            cd[c * cq:c * cq + SUBLANES], (SUBLANES, hd))
        sz = 2
        while sz < cq:
            mask = level_mask(sz)
            for i in range(DN_INST):
                p_s[i] = _dot(t_s[i].astype(BF16), jnp.where(mask, a_s[i], 0.0)).astype(BF16)
            for i in range(DN_INST):
                t = t_s[i]
                t_s[i] = t - _dot(p_s[i], t.astype(BF16))
            sz *= 2
        for i in range(DN_INST):
            rhs = rhs_s[i]
            sol = rhs + _dot((t_s[i] - eye).astype(BF16), rhs.astype(BF16))
            us[i] = sol[:, :hd]
            w = sol[:, hd:].astype(BF16)
            for c in (0, 1):
                w2[i, c * DN_W2_ROWS:c * DN_W2_ROWS + cq, :] = w[c * cq:(c + 1) * cq]

    def recur_batch(k):
        for t in range(2 * run_units):
            plan = []
            for j in range(DN_HB):
                for d in range(2):
                    for s in range(chains_per_dir):
                        ul = s * run_units + t // 2
                        c = t % 2 if d == 0 else 1 - t % 2
                        plan.append((chain_id(j, d, s), inst(ul, j, d), j, d, c,
                                     unit_of(k, ul, d) * unit + c * cq))
            for ch, i, j, d, c, pos in plan:
                first = (pos % seg_len == 0) if d == 0 else ((pos + cq) % seg_len == 0)
                s_val = jnp.where(first, init_state(j, d), sst[ch])
                sst[ch] = s_val
                tt = _dot(w2[i, c * DN_W2_ROWS:(c + 1) * DN_W2_ROWS, :], s_val.astype(BF16))
                v_new = (us[i, c * cq:(c + 1) * cq, :] - tt[0:cq]).astype(BF16)
                vp_s[ch] = jnp.concatenate([v_new, zero_half] if c == 0 else [zero_half, v_new], axis=0)
                op_s[ch] = tt[cq:2 * cq]
            for ch, i, j, d, c, pos in plan:
                t2 = _dot(m2[i, c * DN_M2_ROWS:(c + 1) * DN_M2_ROWS, :], vp_s[ch])
                orows = pl.ds(pl.multiple_of(pos, cq), cq)
                oacc[orows, j * hd:(j + 1) * hd] += op_s[ch] + t2[0:cq]
                s_new = sst[ch] * cds[i, c * SUBLANES:c * SUBLANES + 1, :] + t2[cq:cq + hd]
                sst[ch] = s_new
                if emit_state and t == 2 * run_units - 1:
                    s_out_refs[d][pos // seg_len, j] = s_new

    def batch(k, carry):
        solve_batch(k)
        recur_batch(k)
        return carry

    lax.fori_loop(0, n_units // DN_UG, batch, 0)

    def out_body(i, carry):
        rows = pl.ds(pl.multiple_of(i * CONV_ROWS, CONV_ROWS), CONV_ROWS)
        for j in range(DN_HB):
            hs = slice(j * hd, (j + 1) * hd)
            o = oacc[rows, hs]
            y_ref[rows, hs] = o * _rms_scale(o) * nw_ref[...] * _silu(z_ref[rows, hs].astype(F32))
        return carry

    lax.fori_loop(0, blk_len // CONV_ROWS, out_body, 0)


def _dn(proj, consts, n_blk, blk_len, seg_len, row_off, init, emit_state):
    rb = row_off // blk_len
    hd = DN_HEAD_DIM
    has_init = init is not None
    n_seg = blk_len // seg_len
    assert CONV_ROWS == DN_UNIT and (blk_len // DN_UNIT) % DN_UG == 0
    seg_units = seg_len // DN_UNIT
    assert DN_UG % seg_units == 0 or seg_units % DN_UG == 0
    n_chains = DN_HB * 2 * (DN_UG // min(seg_units, DN_UG))
    bw = DN_HB * hd
    n_hp = DN_HEADS // DN_HB
    cq0 = COL_QKV // bw
    in_specs = [
        pl.BlockSpec((blk_len, bw), lambda b, h: (rb + b, cq0 + h)),
        pl.BlockSpec((blk_len, bw), lambda b, h: (rb + b, cq0 + n_hp + h)),
        pl.BlockSpec((blk_len, bw), lambda b, h: (rb + b, cq0 + 2 * n_hp + h)),
        pl.BlockSpec((blk_len, bw), lambda b, h: (rb + b, COL_Z_D // bw + h)),
        pl.BlockSpec((blk_len, LANES), lambda b, h: (rb + b, COL_SMALL // LANES)),
        pl.BlockSpec((SUBLANES, bw), lambda b, h: (0, h)),
        pl.BlockSpec((1, bw), lambda b, h: (0, h)),
        pl.BlockSpec((SUBLANES, bw), lambda b, h: (0, n_hp + h)),
        pl.BlockSpec((1, bw), lambda b, h: (0, n_hp + h)),
        pl.BlockSpec((SUBLANES, bw), lambda b, h: (0, 2 * n_hp + h)),
        pl.BlockSpec((1, bw), lambda b, h: (0, 2 * n_hp + h)),
        pl.BlockSpec((SUBLANES, LANES), lambda b, h: (0, 0)),
        pl.BlockSpec((1, hd), lambda b, h: (0, 0)),
        pl.BlockSpec((1, 3, LANES, LANES), lambda b, h: (h, 0, 0, 0)),
    ]
    args = [proj, proj, proj, proj, proj,
            consts["cw_dn"], consts["cb_dn"], consts["cw_dn"], consts["cb_dn"], consts["cw_dn"],
            consts["cb_dn"], consts["gate_params"], consts["dn_norm_w"], consts["dn_sel"]]
    if has_init:
        assert n_seg == 1
        in_specs += [pl.BlockSpec((1, DN_HB, hd, hd), lambda b, h: (b, h, 0, 0))] * 2
        args += list(init)
    out_specs = [pl.BlockSpec((blk_len, bw), lambda b, h: (b, h))]
    out_shape = [jax.ShapeDtypeStruct((n_blk * blk_len, DN_WIDTH), F32)]
    if emit_state:
        out_specs += [pl.BlockSpec((n_seg, DN_HB, hd, hd), lambda b, h: (b, h, 0, 0))] * 2
        out_shape += [jax.ShapeDtypeStruct((n_blk * n_seg, DN_HEADS, hd, hd), F32)] * 2
    scratch = [
        pltpu.VMEM((blk_len + 2 * CONV_HALO, bw), F32),
        pltpu.VMEM((blk_len, bw), F32),
        pltpu.VMEM((blk_len, bw), F32),
        pltpu.VMEM((blk_len, bw), F32),
        pltpu.VMEM((blk_len, LANES), F32),
        pltpu.VMEM((blk_len, LANES), F32),
        pltpu.VMEM((blk_len, LANES), F32),
        pltpu.VMEM((blk_len, bw), F32),
        pltpu.VMEM((n_chains, hd, hd), F32),
        pltpu.VMEM((DN_INST, DN_UNIT, DN_UNIT), F32),
        pltpu.VMEM((DN_INST, DN_UNIT, DN_UNIT), F32),
        pltpu.VMEM((DN_INST, DN_UNIT, DN_UNIT), BF16),
        pltpu.VMEM((DN_INST, DN_UNIT, DN_UNIT), BF16),
        pltpu.VMEM((DN_INST, DN_UNIT, DN_UNIT), F32),
        pltpu.VMEM((DN_INST, DN_UNIT, 2 * hd), F32),
        pltpu.VMEM((DN_INST, 2 * DN_W2_ROWS, hd), BF16),
        pltpu.VMEM((DN_INST, 2 * DN_M2_ROWS, hd), BF16),
        pltpu.VMEM((DN_INST, DN_UNIT, hd), F32),
        pltpu.VMEM((DN_INST, 2 * SUBLANES, hd), F32),
        pltpu.VMEM((n_chains, DN_UNIT, hd), BF16),
        pltpu.VMEM((n_chains, DN_CHUNK, hd), F32),
    ]
    return pl.pallas_call(
        functools.partial(_dn_kernel, blk_len=blk_len, seg_len=seg_len, has_init=has_init,
                          emit_state=emit_state),
        grid=(n_blk, n_hp),
        in_specs=in_specs,
        out_specs=out_specs,
        out_shape=out_shape,
        scratch_shapes=scratch,
        compiler_params=_cparams(("arbitrary", "arbitrary")),
        name="dn_scan",
    )(*args)


ROUTE_ID_LANE = 0
ROUTE_GATE_LANE = 2


def _outproj_kernel(ysc_ref, ysl_ref, ydc_ref, ydl_ref, x0_ref, mod_ref, snw_ref, n2w_ref, wo_ref,
                    wr_ref, br_ref, x1_ref, h2_ref, route_ref, *, ctx_tiles):
    is_ctx = pl.program_id(0) < ctx_tiles
    ys = jnp.where(is_ctx, ysc_ref[...], ysl_ref[...])
    ysn = (ys * _rms_scale(ys) * snw_ref[...]).astype(BF16)
    ydn = jnp.where(is_ctx, ydc_ref[...], ydl_ref[...]).astype(BF16)
    m = _dot(ysn, wo_ref[0:SSD_WIDTH, :]) + _dot(ydn, wo_ref[SSD_WIDTH:, :])
    mod = mod_ref[0]
    x1 = x0_ref[...] + mod[2:3] * m
    x1_ref[...] = x1
    h2 = (x1 * _rms_scale(x1) * n2w_ref[...]) * (1.0 + mod[4:5]) + mod[3:4]
    h2_ref[...] = _pack_rows(h2)
    logits = _dot(h2.astype(BF16), wr_ref[...]) + br_ref[...]
    ln = lax.broadcasted_iota(jnp.int32, logits.shape, 1)
    is_g = ln < N_EXPERT_GROUPS
    gl = jnp.where(is_g, logits, NEG_BIG)
    gmax = jnp.max(gl, axis=-1, keepdims=True)
    gidx = jnp.min(jnp.where(gl == gmax, ln, LANES), axis=-1, keepdims=True)
    gw = 1.0 / jnp.sum(jnp.where(is_g, jnp.exp(gl - gmax), 0.0), axis=-1, keepdims=True)
    lo = N_EXPERT_GROUPS + gidx * EXPERTS_PER_GROUP
    in_grp = (ln >= lo) & (ln < lo + EXPERTS_PER_GROUP)
    el = jnp.where(in_grp, logits, NEG_BIG)
    m1 = jnp.max(el, axis=-1, keepdims=True)
    i1 = jnp.min(jnp.where(el == m1, ln, LANES), axis=-1, keepdims=True)
    el2 = jnp.where(ln == i1, NEG_BIG, el)
    m2 = jnp.max(el2, axis=-1, keepdims=True)
    i2 = jnp.min(jnp.where(el2 == m2, ln, LANES), axis=-1, keepdims=True)
    t = jnp.exp(m2 - m1)
    g1 = gw / (1.0 + t)
    g2 = gw * t / (1.0 + t)
    id1 = (i1 - N_EXPERT_GROUPS).astype(F32)
    id2 = (i2 - N_EXPERT_GROUPS).astype(F32)
    route = jnp.where(ln == 0, id1, jnp.where(ln == 1, id2, jnp.where(ln == 2, g1,
                                                                      jnp.where(ln == 3, g2, 0.0))))
    route_ref[...] = route


def _outproj(yssd_ctx, yssd_lat, ydn_ctx, ydn_lat, x0, mod, consts, lat_seq):
    n_total = x0.shape[0]
    tm = ROW_TILE
    ctx_tiles = yssd_ctx.shape[0] // tm
    ctx_row, lat_row, mod_row = _stream_maps(ctx_tiles, lat_seq // tm)
    row = lambda i: (i, 0)
    const2 = lambda i: (0, 0)
    return pl.pallas_call(
        functools.partial(_outproj_kernel, ctx_tiles=ctx_tiles),
        grid=(n_total // tm,),
        in_specs=[pl.BlockSpec((tm, SSD_WIDTH), ctx_row),
                  pl.BlockSpec((tm, SSD_WIDTH), lat_row),
                  pl.BlockSpec((tm, DN_WIDTH), ctx_row),
                  pl.BlockSpec((tm, DN_WIDTH), lat_row),
                  pl.BlockSpec((tm, D_MODEL), row),
                  pl.BlockSpec((1, N_MOD, D_MODEL), mod_row),
                  pl.BlockSpec((1, SSD_WIDTH), const2),
                  pl.BlockSpec((1, D_MODEL), const2),
                  pl.BlockSpec((SSD_WIDTH + DN_WIDTH, D_MODEL), const2),
                  pl.BlockSpec((D_MODEL, LANES), const2),
                  pl.BlockSpec((1, LANES), const2)],
        out_specs=[pl.BlockSpec((tm, D_MODEL), row),
                   pl.BlockSpec((tm, PACK_COLS), row),
                   pl.BlockSpec((tm, LANES), row)],
        out_shape=[jax.ShapeDtypeStruct((n_total, D_MODEL), F32),
                   jax.ShapeDtypeStruct((n_total, PACK_COLS), jnp.uint32),
                   jax.ShapeDtypeStruct((n_total, LANES), F32)],
        compiler_params=_cparams(("arbitrary",)),
        name="outproj_router",
    )(yssd_ctx, yssd_lat, ydn_ctx, ydn_lat, x0, mod, consts["ssd_norm_w"], consts["norm2_w"],
      consts["w_out"], consts["w_router"], consts["b_router"])


def _rank_kernel(route_ref, rank_ref, count_ref, carry):
    i = pl.program_id(0)

    @pl.when(i == 0)
    def _():
        carry[...] = jnp.zeros(carry.shape, F32)

    route = route_ref[...]
    n = route.shape[0]
    ln = lax.broadcasted_iota(jnp.int32, route.shape, 1)
    id1 = route[:, 0:1].astype(jnp.int32)
    id2 = route[:, 1:2].astype(jnp.int32)
    hit1 = ln == id1
    hit2 = ln == id2
    onehot = jnp.where(hit1, 1.0, jnp.where(hit2, 1.0, 0.0))
    ri = lax.broadcasted_iota(jnp.int32, (n, n), 0)
    ci = lax.broadcasted_iota(jnp.int32, (n, n), 1)
    before = (ri > ci).astype(BF16)
    tot = _dot(before, onehot.astype(BF16)) + carry[0:1, :]
    r1 = jnp.sum(jnp.where(hit1, tot, 0.0), axis=-1, keepdims=True)
    r2 = jnp.sum(jnp.where(hit2, tot, 0.0), axis=-1, keepdims=True)
    info = jnp.where(ln < 2, route, jnp.where(ln == 2, r1, jnp.where(ln == 3, r2, 0.0)))
    rank_ref[0] = info.T[0:SUBLANES, :]
    new = carry[...] + jnp.sum(onehot, axis=0, keepdims=True)
    carry[...] = new
    count_ref[...] = new


def _ranks(route):
    n_total = route.shape[0]
    tm = ROW_TILE
    return pl.pallas_call(
        _rank_kernel,
        grid=(n_total // tm,),
        in_specs=[pl.BlockSpec((tm, LANES), lambda i: (i, 0))],
        out_specs=[pl.BlockSpec((1, SUBLANES, tm), lambda i: (i, 0, 0)),
                   pl.BlockSpec((SUBLANES, LANES), lambda i: (0, 0))],
        out_shape=[jax.ShapeDtypeStruct((n_total // tm, SUBLANES, tm), F32),
                   jax.ShapeDtypeStruct((SUBLANES, LANES), F32)],
        scratch_shapes=[pltpu.VMEM((SUBLANES, LANES), F32)],
        compiler_params=_cparams(("arbitrary",)),
        name="slot_ranks",
    )(route)


def _scatter_kernel(pos_ref, h_ref, xs_in_ref, xs_ref, sem):
    del xs_in_ref
    n = h_ref.shape[0]

    def body(tb, carry):
        for u in range(DMA_UNROLL):
            t = tb * DMA_UNROLL + u
            for k in range(2):
                p = pos_ref[0, 0, k * n + t]
                pltpu.make_async_copy(h_ref.at[pl.ds(t, 1), :], xs_ref.at[pl.ds(p, 1), :],
                                      sem).start(priority=k)
        return carry

    lax.fori_loop(0, n // DMA_UNROLL, body, 0)
    for _ in range(2):
        pltpu.make_async_copy(h_ref, xs_ref.at[pl.ds(0, n), :], sem).wait()


def _scatter_rows(pos3, h2, n_sorted):
    n_total = h2.shape[0]
    tm = ROW_TILE
    zeros = jnp.zeros((n_sorted, PACK_COLS), jnp.uint32)
    return pl.pallas_call(
        _scatter_kernel,
        grid=(n_total // tm,),
        in_specs=[pl.BlockSpec((1, 1, 2 * tm), lambda i: (i, 0, 0), memory_space=pltpu.SMEM),
                  pl.BlockSpec((tm, PACK_COLS), lambda i: (i, 0)),
                  pl.BlockSpec(memory_space=pl.ANY)],
        out_specs=pl.BlockSpec(memory_space=pl.ANY),
        out_shape=jax.ShapeDtypeStruct((n_sorted, PACK_COLS), jnp.uint32),
        scratch_shapes=[pltpu.SemaphoreType.DMA(())],
        input_output_aliases={2: 0},
        compiler_params=_cparams(("arbitrary",)),
        name="moe_scatter",
    )(pos3, h2, zeros)


def _ffn_kernel(te_ref, tv_ref, x_ref, wg_ref, wu_ref, wd_ref, y_ref, wgb, wub, wdb):
    i = pl.program_id(0)
    prev = te_ref[jnp.maximum(i - 1, 0)]

    @pl.when(jnp.logical_or(i == 0, te_ref[i] != prev))
    def _():
        wgb[...] = wg_ref[0].astype(BF16)
        wub[...] = wu_ref[0].astype(BF16)
        wdb[...] = wd_ref[0].astype(BF16)

    @pl.when(tv_ref[i] > 0)
    def _():
        x_lo, x_hi = _unpack_rows(x_ref[...])
        xb = jnp.concatenate([x_lo.astype(BF16), x_hi.astype(BF16)], axis=1)
        a = _dot(xb, wgb[...])
        u = _dot(xb, wub[...])
        act = (_silu(a) * u).astype(BF16)
        y_ref[...] = _pack_rows(_dot(act, wdb[...]))

    @pl.when(tv_ref[i] == 0)
    def _():
        y_ref[...] = jnp.zeros(y_ref.shape, jnp.uint32)


def _grouped_ffn(tile_expert, tile_valid, xs, w_gate, w_up, w_down):
    n_sorted = xs.shape[0]
    tm = MOE_TILE
    grid_spec = pltpu.PrefetchScalarGridSpec(
        num_scalar_prefetch=2,
        grid=(n_sorted // tm,),
        in_specs=[pl.BlockSpec((tm, PACK_COLS), lambda i, te, tv: (i, 0)),
                  pl.BlockSpec((1, D_MODEL, D_FF), lambda i, te, tv: (te[i], 0, 0)),
                  pl.BlockSpec((1, D_MODEL, D_FF), lambda i, te, tv: (te[i], 0, 0)),
                  pl.BlockSpec((1, D_FF, D_MODEL), lambda i, te, tv: (te[i], 0, 0))],
        out_specs=pl.BlockSpec((tm, PACK_COLS), lambda i, te, tv: (i, 0)),
        scratch_shapes=[pltpu.VMEM((D_MODEL, D_FF), BF16),
                        pltpu.VMEM((D_MODEL, D_FF), BF16),
                        pltpu.VMEM((D_FF, D_MODEL), BF16)],
    )
    return pl.pallas_call(
        _ffn_kernel,
        grid_spec=grid_spec,
        out_shape=jax.ShapeDtypeStruct((n_sorted, PACK_COLS), jnp.uint32),
        compiler_params=_cparams(("arbitrary",)),
        name="moe_ffn",
    )(tile_expert, tile_valid, xs, w_gate, w_up, w_down)


def _combine_kernel(pos_ref, route_ref, x1_ref, mod_ref, fw_ref, ys_ref, out_ref, buf, sem):
    n = x1_ref.shape[0]

    def body(tb, carry):
        for u in range(DMA_UNROLL):
            t = tb * DMA_UNROLL + u
            for k in range(2):
                p = pos_ref[0, 0, k * n + t]
                pltpu.make_async_copy(ys_ref.at[pl.ds(p, 1), :], buf.at[k, pl.ds(t, 1), :],
                                      sem).start(priority=k)
        return carry

    lax.fori_loop(0, n // DMA_UNROLL, body, 0)
    for k in range(2):
        pltpu.make_async_copy(ys_ref.at[pl.ds(0, n), :], buf.at[k], sem).wait()
    route = route_ref[...]
    g1 = route[:, ROUTE_GATE_LANE:ROUTE_GATE_LANE + 1]
    g2 = route[:, ROUTE_GATE_LANE + 1:ROUTE_GATE_LANE + 2]
    y1_lo, y1_hi = _unpack_rows(buf[0])
    y2_lo, y2_hi = _unpack_rows(buf[1])
    moe = jnp.concatenate([g1 * y1_lo + g2 * y2_lo, g1 * y1_hi + g2 * y2_hi], axis=1)
    x2 = x1_ref[...] + mod_ref[0][5:6] * moe
    out_ref[...] = x2 * _rms_scale(x2) * fw_ref[...]


def _combine(pos3, route, x1, mod, final_w, ys, n_rows, row_off, tokens_per_mod, mod_off):
    tm = ROW_TILE
    off = row_off // tm
    per = tokens_per_mod // tm
    return pl.pallas_call(
        _combine_kernel,
        grid=(n_rows // tm,),
        in_specs=[pl.BlockSpec((1, 1, 2 * tm), lambda i: (off + i, 0, 0), memory_space=pltpu.SMEM),
                  pl.BlockSpec((tm, LANES), lambda i: (off + i, 0)),
                  pl.BlockSpec((tm, D_MODEL), lambda i: (off + i, 0)),
                  pl.BlockSpec((1, N_MOD, D_MODEL), lambda i: (mod_off + i // per, 0, 0)),
                  pl.BlockSpec((1, D_MODEL), lambda i: (0, 0)),
                  pl.BlockSpec(memory_space=pl.ANY)],
        out_specs=pl.BlockSpec((tm, D_MODEL), lambda i: (i, 0)),
        out_shape=jax.ShapeDtypeStruct((n_rows, D_MODEL), F32),
        scratch_shapes=[pltpu.VMEM((2, tm, PACK_COLS), jnp.uint32), pltpu.SemaphoreType.DMA(())],
        compiler_params=_cparams(("arbitrary",)),
        name="moe_combine",
    )(pos3, route, x1, mod, final_w, ys)


def _grid_sincos_2d(n_tokens):
    rows = n_tokens // GRID_W
    quarter = D_MODEL // 4
    omega = 1.0 / (POS_BASE ** (jnp.arange(quarter, dtype=F32) / quarter))
    r = jnp.broadcast_to(jnp.arange(rows, dtype=F32)[:, None], (rows, GRID_W)).reshape(-1)
    c = jnp.broadcast_to(jnp.arange(GRID_W, dtype=F32)[None, :], (rows, GRID_W)).reshape(-1)
    ar = r[:, None] * omega
    ac = c[:, None] * omega
    return jnp.concatenate([jnp.sin(ar), jnp.cos(ar), jnp.sin(ac), jnp.cos(ac)], axis=-1)


def _pad_rows(a, n):
    return jnp.concatenate([a, jnp.zeros((n - a.shape[0],) + a.shape[1:], a.dtype)], axis=0)


def _selection_constants():
    gh = SSD_GROUP_HEADS
    ssd_sel = np.zeros((SSD_GROUPS, 3, LANES, LANES), np.float32)
    for g in range(SSD_GROUPS):
        for part in range(3):
            for d in range(2):
                for j in range(gh):
                    ssd_sel[g, part, d * SSD_HEADS + g * gh + j, SSD_GATE_LANES * part + d * gh + j] = 1.0
    ssd_exp = np.zeros((2, LANES, SSD_GROUP_COLS), np.float32)
    for d in range(2):
        for part in range(3):
            for c in range(SSD_GROUP_COLS):
                ssd_exp[d, SSD_GATE_LANES * part + d * gh + c // SSD_HEAD_DIM, c] = 1.0
    dn_sel = np.zeros((DN_HEADS // DN_HB, 3, LANES, LANES), np.float32)
    base = 2 * SSD_HEADS
    for h in range(DN_HEADS):
        dst = DN_GATE_LANES * (h % DN_HB)
        for part in range(3):
            for d in range(2):
                dn_sel[h // DN_HB, part, base + d * DN_HEADS + h, dst + 4 * part + d] = 1.0
                dn_sel[h // DN_HB, part, base + 2 * DN_HEADS + d * DN_HEADS + h, dst + 4 * part + 2 + d] = 1.0
    return ssd_sel, ssd_exp, dn_sel


def _ssd_consts(conv_w, conv_b, ssd_d):
    bcw, bcb = [], []
    for g in range(SSD_GROUPS):
        for base in (SSD_WIDTH, SSD_WIDTH + SSD_GROUPS * SSD_STATE):
            sl = slice(base + g * SSD_STATE, base + (g + 1) * SSD_STATE)
            bcw.append(conv_w[:, sl])
            bcb.append(conv_b[sl])
    ssd_sel, ssd_exp, _ = _selection_constants()
    return {
        "cw_x": _pad_rows(conv_w[:, 0:SSD_WIDTH], SUBLANES),
        "cb_x": conv_b[0:SSD_WIDTH].reshape(1, -1),
        "cw_bc": _pad_rows(jnp.concatenate(bcw, axis=1), SUBLANES),
        "cb_bc": jnp.concatenate(bcb).reshape(1, -1),
        "d_exp": jnp.repeat(ssd_d, SSD_HEAD_DIM).reshape(1, -1),
        "ssd_sel": jnp.asarray(ssd_sel, dtype=BF16),
        "ssd_exp": jnp.asarray(ssd_exp, dtype=BF16),
    }


def _dn_consts(conv_w, conv_b, ssd_dt_bias, ssd_a_log, dn_dt_bias, dn_a_log, dn_norm_w):
    n_ssd = 2 * SSD_HEADS
    n_dn = 2 * DN_HEADS
    gate_params = jnp.zeros((SUBLANES, LANES), F32)
    gate_params = gate_params.at[0, 0:n_ssd].set(ssd_dt_bias.reshape(-1))
    gate_params = gate_params.at[0, n_ssd:n_ssd + n_dn].set(dn_dt_bias.reshape(-1))
    gate_params = gate_params.at[1, 0:n_ssd].set(ssd_a_log.reshape(-1))
    gate_params = gate_params.at[1, n_ssd:n_ssd + n_dn].set(dn_a_log.reshape(-1))
    dn_sel = _selection_constants()[2]
    return {
        "cw_dn": _pad_rows(conv_w, SUBLANES),
        "cb_dn": conv_b.reshape(1, -1),
        "gate_params": gate_params,
        "dn_sel": jnp.asarray(dn_sel, dtype=BF16),
        "dn_norm_w": dn_norm_w.reshape(1, -1),
    }


def kernel(x_prompt, x_sample, state_ssd_fwd, state_ssd_bwd, state_dn_fwd, state_dn_bwd, c, c_ctx, w_ada, b_ada, norm1_w, w_in, conv_ssd_w, conv_ssd_b, conv_dn_w, conv_dn_b, ssd_dt_bias, ssd_a_log, ssd_d, ssd_norm_w, dn_dt_bias, dn_a_log, dn_norm_w, w_out, norm2_w, w_router_group, b_router_group, w_router_expert, b_router_expert, w_gate, w_up, w_down, final_norm_w):
    n_b, seq, _ = x_prompt.shape
    dec_b, dec_seq, _ = x_sample.shape
    n_ctx = n_b * seq
    n_lat = dec_b * dec_seq
    n_total = n_ctx + n_lat
    layer = 0

    wi = w_in[layer]
    xbc0 = SSD_WIDTH
    bm0 = xbc0 + SSD_WIDTH
    cm0 = bm0 + SSD_GROUPS * SSD_STATE
    dt0 = cm0 + SSD_GROUPS * SSD_STATE
    qkv0 = dt0 + 2 * SSD_HEADS
    zd0 = qkv0 + 3 * DN_WIDTH
    ad0 = zd0 + DN_WIDTH
    bd0 = ad0 + 2 * DN_HEADS
    bc_cols = []
    for g in range(SSD_GROUPS):
        bc_cols += [wi[:, bm0 + g * SSD_STATE:bm0 + (g + 1) * SSD_STATE],
                    wi[:, cm0 + g * SSD_STATE:cm0 + (g + 1) * SSD_STATE]]
    w_r = jnp.concatenate(
        [wi[:, 0:SSD_WIDTH], wi[:, xbc0:bm0]] + bc_cols +
        [wi[:, qkv0:zd0], wi[:, zd0:ad0], wi[:, dt0:qkv0], wi[:, ad0:bd0], wi[:, bd0:bd0 + 2 * DN_HEADS],
         jnp.zeros((D_MODEL, PROJ_COLS - COL_SMALL - 2 * SSD_HEADS - 4 * DN_HEADS), F32)],
        axis=1).astype(BF16)

    w_router = jnp.concatenate(
        [w_router_group[layer], w_router_expert[layer],
         jnp.zeros((D_MODEL, LANES - N_EXPERT_GROUPS - N_EXPERTS), F32)], axis=1)
    b_router = jnp.concatenate(
        [b_router_group[layer], b_router_expert[layer],
         jnp.zeros((LANES - N_EXPERT_GROUPS - N_EXPERTS,), F32)]).reshape(1, LANES)
    consts = {
        "ssd_norm_w": ssd_norm_w[layer].reshape(1, -1),
        "norm2_w": norm2_w[layer].reshape(1, -1),
        "w_out": w_out[layer].astype(BF16),
        "w_router": w_router.astype(BF16),
        "b_router": b_router,
    }
    consts.update(_ssd_consts(conv_ssd_w[layer], conv_ssd_b[layer], ssd_d[layer]))
    consts.update(_dn_consts(conv_dn_w[layer], conv_dn_b[layer], ssd_dt_bias[layer], ssd_a_log[layer],
                             dn_dt_bias[layer], dn_a_log[layer], dn_norm_w[layer]))

    cond = _pad_rows(jnp.concatenate([c_ctx[None, :], c], axis=0), SUBLANES)
    mod = _adaln(cond, w_ada[layer], b_ada[layer]).reshape(SUBLANES, N_MOD, D_MODEL)

    pos = _grid_sincos_2d(dec_seq)
    n1w = norm1_w[layer].reshape(1, -1)
    proj, x0 = _inproj(x_prompt.reshape(n_ctx, D_MODEL), x_sample.reshape(n_lat, D_MODEL), pos, mod, n1w,
                       w_r, dec_seq)

    def ssd_state_in(s):
        return s[:, layer].transpose(0, 3, 1, 2).reshape(dec_b, SSD_STATE, SSD_WIDTH)

    yssd_ctx, hf, hb = _ssd(proj, consts, n_b, seq, 0, None, True)
    yssd_lat = _ssd(proj, consts, dec_b, dec_seq, n_ctx,
                    (ssd_state_in(state_ssd_fwd), ssd_state_in(state_ssd_bwd)), False)[0]
    ydn_ctx, sf, sb = _dn(proj, consts, n_ctx // dec_seq, dec_seq, seq, 0, None, True)
    ydn_lat = _dn(proj, consts, dec_b, dec_seq, dec_seq, n_ctx,
                  (state_dn_fwd[:, layer], state_dn_bwd[:, layer]), False)[0]

    x1, h2, route = _outproj(yssd_ctx, yssd_lat, ydn_ctx, ydn_lat, x0, mod, consts, dec_seq)

    slots, counts = _ranks(route)
    counts = counts[0, :N_EXPERTS].astype(jnp.int32)
    padded = ((counts + MOE_TILE - 1) // MOE_TILE) * MOE_TILE
    ends = jnp.cumsum(padded)
    offsets = ends - padded
    n_sorted = 2 * n_total + N_EXPERTS * MOE_TILE
    ids = slots[:, 0:2, :].astype(jnp.int32)
    pos_slots = offsets[ids] + slots[:, 2:4, :].astype(jnp.int32)
    pos3 = pos_slots.reshape(n_total // ROW_TILE, 1, 2 * ROW_TILE)
    tile_start = jnp.arange(n_sorted // MOE_TILE, dtype=jnp.int32) * MOE_TILE
    tile_expert = jnp.minimum(jnp.sum(tile_start[:, None] >= ends[None, :], axis=1), N_EXPERTS - 1).astype(jnp.int32)
    tile_valid = (tile_start < ends[-1]).astype(jnp.int32)

    xs = _scatter_rows(pos3, h2, n_sorted)
    ys = _grouped_ffn(tile_expert, tile_valid, xs, w_gate[layer], w_up[layer], w_down[layer])
    fw = final_norm_w.reshape(1, -1)
    y_prompt = _combine(pos3, route, x1, mod, fw, ys, n_ctx, 0, n_ctx, 0)
    y_sample = _combine(pos3, route, x1, mod, fw, ys, n_lat, n_ctx, dec_seq, 1)

    def ssd_state_out(s):
        return s.reshape(n_b, SSD_STATE, SSD_HEADS, SSD_HEAD_DIM).transpose(0, 2, 3, 1)[:, None]

    return (y_prompt.reshape(n_b, seq, D_MODEL), y_sample.reshape(dec_b, dec_seq, D_MODEL),
            ssd_state_out(hf), ssd_state_out(hb), sf[:, None], sb[:, None])
```

```python
import functools

import numpy as np
import jax
import jax.numpy as jnp
from jax import lax
from jax.experimental import pallas as pl
from jax.experimental.pallas import tpu as pltpu

F32 = jnp.float32
BF16 = jnp.bfloat16
HI = lax.Precision.HIGHEST

D_MODEL = 1024
EPS = 1e-6
GRID_W = 64
POS_BASE = 10000.0
CONV_W = 5
N_MOD = 6
SSD_HEADS = 16
SSD_HEAD_DIM = 64
SSD_WIDTH = SSD_HEADS * SSD_HEAD_DIM
SSD_GROUPS = 2
SSD_STATE = 64
SSD_CHUNK = 128
DN_HEADS = 8
DN_HEAD_DIM = 128
DN_WIDTH = DN_HEADS * DN_HEAD_DIM
DN_CHUNK = 64
N_EXPERT_GROUPS = 4
EXPERTS_PER_GROUP = 8
N_EXPERTS = N_EXPERT_GROUPS * EXPERTS_PER_GROUP
D_FF = 512

LANES = 128
SUBLANES = 8
VMEM_LIMIT = 56 * 1024 * 1024

COL_Z_S = 0
COL_X = 1024
COL_BC = 2048
COL_QKV = 2304
COL_Z_D = 5376
COL_SMALL = 6400
PROJ_COLS = 6528

ROW_TILE = 256
MOE_TILE = 256
DMA_UNROLL = 8
NEG_BIG = -1e30


def _dot(a, b, prec=None):
    return jnp.dot(a, b, preferred_element_type=F32, precision=prec)


def _dot_nt(a, b, prec=None):
    return lax.dot_general(a, b, (((1,), (1,)), ((), ())), preferred_element_type=F32, precision=prec)


def _dot_tn(a, b, prec=None):
    return lax.dot_general(a, b, (((0,), (0,)), ((), ())), preferred_element_type=F32, precision=prec)


def _silu(x):
    return x * jax.nn.sigmoid(x)


def _softplus(x):
    return jnp.maximum(x, 0.0) + jnp.log1p(jnp.exp(-jnp.abs(x)))


def _rms_scale(x):
    return lax.rsqrt(jnp.mean(x * x, axis=-1, keepdims=True) + EPS)


PACK_COLS = D_MODEL // 2
HIGH_HALF = 0xFFFF0000


def _pack_rows(x):
    lo = pltpu.bitcast(x[:, :PACK_COLS].astype(BF16).astype(F32), jnp.uint32)
    hi = pltpu.bitcast(x[:, PACK_COLS:].astype(BF16).astype(F32), jnp.uint32)
    return (lo >> 16) | (hi & jnp.uint32(HIGH_HALF))


def _unpack_rows(u):
    return pltpu.bitcast(u << 16, F32), pltpu.bitcast(u & jnp.uint32(HIGH_HALF), F32)


def _cparams(sem, vmem=VMEM_LIMIT):
    return pltpu.CompilerParams(dimension_semantics=sem, vmem_limit_bytes=vmem)


def _ada_kernel(c_ref, w_ref, b_ref, o_ref):
    c = c_ref[...]
    o_ref[...] = _dot(_silu(c), w_ref[...], HI) + b_ref[...]


def _adaln(cond, w_ada, b_ada):
    n_out = N_MOD * D_MODEL
    tn = 1536
    return pl.pallas_call(
        _ada_kernel,
        grid=(n_out // tn,),
        in_specs=[pl.BlockSpec((SUBLANES, D_MODEL), lambda j: (0, 0)),
                  pl.BlockSpec((D_MODEL, tn), lambda j: (0, j)),
                  pl.BlockSpec((1, tn), lambda j: (0, j))],
        out_specs=pl.BlockSpec((SUBLANES, tn), lambda j: (0, j)),
        out_shape=jax.ShapeDtypeStruct((SUBLANES, n_out), F32),
        compiler_params=_cparams(("arbitrary",)),
        name="adaln",
    )(cond, w_ada, b_ada.reshape(1, n_out))


def _inproj_kernel(xc_ref, xl_ref, pos_ref, mod_ref, nw_ref, w_ref, proj_ref, x0_ref, *, ctx_tiles):
    is_ctx = pl.program_id(0) < ctx_tiles
    x = jnp.where(is_ctx, xc_ref[...], xl_ref[...] + pos_ref[...])
    x0_ref[...] = x
    mod = mod_ref[0]
    h = (x * _rms_scale(x) * nw_ref[...]) * (1.0 + mod[1:2]) + mod[0:1]
    proj_ref[...] = _dot(h.astype(BF16), w_ref[...])


def _stream_maps(ctx_tiles, lat_tiles_per_mod):
    def ctx_row(i):
        return (jnp.minimum(i, ctx_tiles - 1), 0)

    def lat_row(i):
        return (jnp.maximum(i - ctx_tiles, 0), 0)

    def mod_row(i):
        return (jnp.where(i < ctx_tiles, 0, 1 + (i - ctx_tiles) // lat_tiles_per_mod), 0, 0)

    return ctx_row, lat_row, mod_row


def _inproj(x_ctx, x_lat, pos, mod, norm_w, w_r, lat_seq):
    tm = ROW_TILE
    ctx_tiles = x_ctx.shape[0] // tm
    n_total = x_ctx.shape[0] + x_lat.shape[0]
    n_pos = pos.shape[0] // tm
    ctx_row, lat_row, mod_row = _stream_maps(ctx_tiles, lat_seq // tm)
    return pl.pallas_call(
        functools.partial(_inproj_kernel, ctx_tiles=ctx_tiles),
        grid=(n_total // tm,),
        in_specs=[pl.BlockSpec((tm, D_MODEL), ctx_row),
                  pl.BlockSpec((tm, D_MODEL), lat_row),
                  pl.BlockSpec((tm, D_MODEL), lambda i: (jnp.maximum(i - ctx_tiles, 0) % n_pos, 0)),
                  pl.BlockSpec((1, N_MOD, D_MODEL), mod_row),
                  pl.BlockSpec((1, D_MODEL), lambda i: (0, 0)),
                  pl.BlockSpec((D_MODEL, PROJ_COLS), lambda i: (0, 0), pipeline_mode=pl.Buffered(1))],
        out_specs=[pl.BlockSpec((tm, PROJ_COLS), lambda i: (i, 0)),
                   pl.BlockSpec((tm, D_MODEL), lambda i: (i, 0))],
        out_shape=[jax.ShapeDtypeStruct((n_total, PROJ_COLS), F32),
                   jax.ShapeDtypeStruct((n_total, D_MODEL), F32)],
        compiler_params=_cparams(("arbitrary",)),
        name="inproj",
    )(x_ctx, x_lat, pos, mod, norm_w, w_r)


CONV_ROWS = 128
CONV_HALO = SUBLANES


def _conv_silu(src_ref, w_ref, b_ref, pad_ref, store, blk_len, seg_len):
    n_ch = src_ref.shape[1]
    zeros = jnp.zeros((CONV_HALO, n_ch), F32)
    pad_ref[0:CONV_HALO, :] = zeros
    pad_ref[CONV_HALO + blk_len:2 * CONV_HALO + blk_len, :] = zeros

    def copy_body(i, carry):
        r0 = pl.multiple_of(i * CONV_ROWS, CONV_ROWS)
        pad_ref[pl.ds(CONV_HALO + r0, CONV_ROWS), :] = src_ref[pl.ds(r0, CONV_ROWS), :].astype(F32)
        return carry

    lax.fori_loop(0, blk_len // CONV_ROWS, copy_body, 0)
    win = CONV_ROWS + 2 * CONV_HALO
    w = w_ref[...]
    b = b_ref[...]
    row = lax.broadcasted_iota(jnp.int32, (CONV_ROWS, n_ch), 0)

    def body(i, carry):
        r0 = pl.multiple_of(i * CONV_ROWS, CONV_ROWS)
        seg_pos = r0 % seg_len
        at_start = seg_pos == 0
        at_end = seg_pos + CONV_ROWS == seg_len
        v = pad_ref[pl.ds(r0, win), :]
        acc = jnp.zeros((CONV_ROWS, n_ch), F32) + b
        for k in range(CONV_W):
            off = k - CONV_W // 2
            shift = (-off) % win
            sh = v if shift == 0 else pltpu.roll(v, shift, 0)
            tap = sh[CONV_HALO:CONV_HALO + CONV_ROWS, :]
            if off < 0:
                tap = jnp.where(jnp.logical_and(at_start, row < -off), 0.0, tap)
            elif off > 0:
                tap = jnp.where(jnp.logical_and(at_end, row >= CONV_ROWS - off), 0.0, tap)
            acc = acc + w[k:k + 1, :] * tap
        store(pl.ds(r0, CONV_ROWS), _silu(acc))
        return carry

    lax.fori_loop(0, blk_len // CONV_ROWS, body, 0)


SSD_GROUP_HEADS = SSD_HEADS // SSD_GROUPS
SSD_GROUP_COLS = SSD_GROUP_HEADS * SSD_HEAD_DIM
SSD_GATE_LANES = 2 * SSD_GROUP_HEADS
SSD_UNROLL = 2


def _ssd_kernel(*refs, seq_len, has_init, emit_state):
    refs = list(refs)
    (x_ref, bc_ref, z_ref, sm_ref, cwx_ref, cbx_ref, cwbc_ref, cbbc_ref, gp_ref, dexp_ref,
     sel_ref, exp_ref) = refs[:12]
    refs = refs[12:]
    if has_init:
        h0f_ref, h0b_ref = refs[:2]
        refs = refs[2:]
    y_ref = refs.pop(0)
    if emit_state:
        hf_ref, hb_ref = refs[:2]
        refs = refs[2:]
    xpad, bcpad, xc, bcc, dt3, da3, yacc, hst = refs

    q = SSD_CHUNK
    n_chunks = seq_len // q
    n_gl = SSD_GATE_LANES
    lane = lax.broadcasted_iota(jnp.int32, (q, LANES), 1)
    gate_lane = lane < n_gl

    def sum_pieces(m):
        return m + pltpu.roll(m, LANES - n_gl, 1) + pltpu.roll(m, LANES - 2 * n_gl, 1)

    def pack_pieces(v):
        v = jnp.where(gate_lane, v, 0.0)
        hi = v.astype(BF16).astype(F32)
        rest = v - hi
        mid = rest.astype(BF16).astype(F32)
        lo = rest - mid
        return (hi + pltpu.roll(mid, n_gl, 1) + pltpu.roll(lo, 2 * n_gl, 1)).astype(BF16)

    def store_x(rows, val):
        xc[rows, :] = val
        yacc[rows, :] = val * dexp_ref[...]

    _conv_silu(x_ref, cwx_ref, cbx_ref, xpad, store_x, seq_len, seq_len)

    def store_bc(rows, val):
        bcc[rows, :] = val

    _conv_silu(bc_ref, cwbc_ref, cbbc_ref, bcpad, store_bc, seq_len, seq_len)

    gp = gp_ref[...]
    bias_row = gp[0:1, :]
    aneg_sel = _dot(-jnp.exp(gp), sel_ref[0, 0].astype(F32), HI)[1:2, :]

    def gate_body(i, carry):
        for sub in range(SSD_UNROLL):
            rows = pl.ds(pl.multiple_of((i * SSD_UNROLL + sub) * q, q), q)
            dt_full = _softplus(sm_ref[rows, :].astype(F32) + bias_row)
            hi = dt_full.astype(BF16)
            rest = dt_full - hi.astype(F32)
            mid = rest.astype(BF16)
            lo = (rest - mid.astype(F32)).astype(BF16)
            dt_sel = sum_pieces(_dot(hi, sel_ref[0, 0]) + _dot(mid, sel_ref[0, 1])
                                + _dot(lo, sel_ref[0, 2]))
            dt3[rows, :] = pack_pieces(dt_sel)
            da3[rows, :] = pack_pieces(dt_sel * aneg_sel)
        return carry

    lax.fori_loop(0, n_chunks // SSD_UNROLL, gate_body, 0)

    if has_init:
        hst[0] = h0f_ref[0]
        hst[1] = h0b_ref[0]
    else:
        hst[...] = jnp.zeros(hst.shape, F32)

    ri = lax.broadcasted_iota(jnp.int32, (q, q), 0)
    ci = lax.broadcasted_iota(jnp.int32, (q, q), 1)
    low_half = lane < SSD_HEAD_DIM
    masks = [ri >= ci, ri <= ci]
    tris = [jnp.where(m, 1.0, 0.0).astype(BF16) for m in masks]

    def chunk(d, c):
        rows = pl.ds(pl.multiple_of(c * q, q), q)
        mask = masks[d]
        xck = xc[rows, :]
        bcck = bcc[rows, :]
        b_c = bcck[:, :SSD_STATE].astype(BF16)
        c_c = bcck[:, SSD_STATE:].astype(BF16)
        cum = sum_pieces(_dot(tris[d], da3[rows, :]))
        cum_t = cum.T
        dt_x = _dot(dt3[rows, :], exp_ref[d])
        ac_x = _dot(pack_pieces(cum), exp_ref[d])
        ae_x = ac_x[q - 1:q, :] if d == 0 else ac_x[0:1, :]
        xdt = xck * dt_x
        cb = _dot_nt(c_c, b_c)
        lane0 = d * SSD_GROUP_HEADS
        parts = []
        for hp in range(SSD_GROUP_HEADS // 2):
            ms = []
            for j in (2 * hp, 2 * hp + 1):
                col = cum[:, lane0 + j:lane0 + j + 1]
                row = cum_t[lane0 + j:lane0 + j + 1, :]
                dec = jnp.exp(jnp.where(mask, col - row, NEG_BIG))
                ms.append((cb * dec).astype(BF16))
            x2 = xdt[:, hp * LANES:(hp + 1) * LANES]
            w_lo = jnp.where(low_half, x2, 0.0)
            w_hi = jnp.where(low_half, 0.0, x2)
            parts.append(_dot(jnp.concatenate(ms, axis=1),
                              jnp.concatenate([w_lo, w_hi], axis=0).astype(BF16)))
        y_diag = jnp.concatenate(parts, axis=1)
        hs = hst[d]
        y_off = _dot(c_c, hs.astype(BF16)) * jnp.exp(ac_x)
        yacc[rows, :] += y_diag + y_off
        hst[d] = hs * jnp.exp(ae_x) + _dot_tn(b_c, (xdt * jnp.exp(ae_x - ac_x)).astype(BF16))

    def step(s, carry):
        for sub in range(SSD_UNROLL):
            chunk(0, SSD_UNROLL * s + sub)
            chunk(1, n_chunks - 1 - SSD_UNROLL * s - sub)
        return carry

    lax.fori_loop(0, n_chunks // SSD_UNROLL, step, 0)

    def out_body(i, carry):
        rows = pl.ds(pl.multiple_of(i * q, q), q)
        y_ref[rows, :] = yacc[rows, :] * _silu(z_ref[rows, :].astype(F32))
        return carry

    lax.fori_loop(0, n_chunks, out_body, 0)
    if emit_state:
        hf_ref[0] = hst[0]
        hb_ref[0] = hst[1]


def _ssd(proj, consts, n_seq, seq_len, row_off, init, emit_state):
    rb = row_off // seq_len
    cw = SSD_GROUP_COLS
    has_init = init is not None

    def col(block_cols, base):
        return base // block_cols

    in_specs = [
        pl.BlockSpec((seq_len, cw), lambda b, g: (rb + b, col(cw, COL_X) + g)),
        pl.BlockSpec((seq_len, LANES), lambda b, g: (rb + b, col(LANES, COL_BC) + g)),
        pl.BlockSpec((seq_len, cw), lambda b, g: (rb + b, col(cw, COL_Z_S) + g)),
        pl.BlockSpec((seq_len, LANES), lambda b, g: (rb + b, col(LANES, COL_SMALL))),
        pl.BlockSpec((SUBLANES, cw), lambda b, g: (0, g)),
        pl.BlockSpec((1, cw), lambda b, g: (0, g)),
        pl.BlockSpec((SUBLANES, LANES), lambda b, g: (0, g)),
        pl.BlockSpec((1, LANES), lambda b, g: (0, g)),
        pl.BlockSpec((SUBLANES, LANES), lambda b, g: (0, 0)),
        pl.BlockSpec((1, cw), lambda b, g: (0, g)),
        pl.BlockSpec((1, 3, LANES, LANES), lambda b, g: (g, 0, 0, 0)),
        pl.BlockSpec((2, LANES, cw), lambda b, g: (0, 0, 0)),
    ]
    args = [proj, proj, proj, proj, consts["cw_x"], consts["cb_x"], consts["cw_bc"], consts["cb_bc"],
            consts["gate_params"], consts["d_exp"], consts["ssd_sel"], consts["ssd_exp"]]
    if has_init:
        in_specs += [pl.BlockSpec((1, SSD_STATE, cw), lambda b, g: (b, 0, g))] * 2
        args += list(init)
    out_specs = [pl.BlockSpec((seq_len, cw), lambda b, g: (b, g))]
    out_shape = [jax.ShapeDtypeStruct((n_seq * seq_len, SSD_WIDTH), F32)]
    if emit_state:
        out_specs += [pl.BlockSpec((1, SSD_STATE, cw), lambda b, g: (b, 0, g))] * 2
        out_shape += [jax.ShapeDtypeStruct((n_seq, SSD_STATE, SSD_WIDTH), F32)] * 2
    scratch = [
        pltpu.VMEM((seq_len + 2 * CONV_HALO, cw), F32),
        pltpu.VMEM((seq_len + 2 * CONV_HALO, LANES), F32),
        pltpu.VMEM((seq_len, cw), F32),
        pltpu.VMEM((seq_len, LANES), F32),
        pltpu.VMEM((seq_len, LANES), BF16),
        pltpu.VMEM((seq_len, LANES), BF16),
        pltpu.VMEM((seq_len, cw), F32),
        pltpu.VMEM((2, SSD_STATE, cw), F32),
    ]
    return pl.pallas_call(
        functools.partial(_ssd_kernel, seq_len=seq_len, has_init=has_init, emit_state=emit_state),
        grid=(n_seq, SSD_GROUPS),
        in_specs=in_specs,
        out_specs=out_specs,
        out_shape=out_shape,
        scratch_shapes=scratch,
        compiler_params=_cparams(("arbitrary", "arbitrary")),
        name="ssd_scan",
    )(*args)


DN_UNIT = 2 * DN_CHUNK
DN_W2_ROWS = 2 * DN_CHUNK
DN_M2_ROWS = DN_CHUNK + DN_HEAD_DIM
DN_HB = 2
DN_UG = 4
DN_INST = DN_UG * DN_HB * 2
DN_GATE_LANES = 16


def _dn_kernel(*refs, blk_len, seg_len, has_init, emit_state):
    refs = list(refs)
    (q_ref, k_ref, v_ref, z_ref, sm_ref, cwq_ref, cbq_ref, cwk_ref, cbk_ref, cwv_ref, cbv_ref,
     gp_ref, nw_ref, sel_ref) = refs[:14]
    refs = refs[14:]
    if has_init:
        s0_refs = refs[:2]
        refs = refs[2:]
    y_ref = refs.pop(0)
    if emit_state:
        s_out_refs = refs[:2]
        refs = refs[2:]
    (pad, qs, ks, vs, gfull, cumf, cumb, oacc, sst, kk_s, qk_s, a_s, p_s, t_s, rhs_s,
     w2, m2, us, cds, vp_s, op_s) = refs

    cq = DN_CHUNK
    hd = DN_HEAD_DIM
    unit = DN_UNIT
    n_units = blk_len // unit
    seg_units = seg_len // unit
    run_units = min(seg_units, DN_UG)
    chains_per_dir = DN_UG // run_units
    scale = DN_HEAD_DIM ** -0.5

    def l2n(v):
        return v * lax.rsqrt(jnp.sum(v * v, axis=-1, keepdims=True) + EPS)

    def per_head(fn, val):
        return jnp.concatenate([fn(val[:, j * hd:(j + 1) * hd]) for j in range(DN_HB)], axis=1)

    def store_q(rows, val):
        qs[rows, :] = per_head(lambda v: l2n(v) * scale, val)

    def store_k(rows, val):
        ks[rows, :] = per_head(l2n, val)

    def store_v(rows, val):
        vs[rows, :] = val
        oacc[rows, :] = jnp.zeros_like(val)

    _conv_silu(q_ref, cwq_ref, cbq_ref, pad, store_q, blk_len, seg_len)
    _conv_silu(k_ref, cwk_ref, cbk_ref, pad, store_k, blk_len, seg_len)
    _conv_silu(v_ref, cwv_ref, cbv_ref, pad, store_v, blk_len, seg_len)

    ri = lax.broadcasted_iota(jnp.int32, (unit, unit), 0)
    ci = lax.broadcasted_iota(jnp.int32, (unit, unit), 1)
    same = (ri // cq) == (ci // cq)
    incl = [jnp.logical_and(same, ri >= ci), jnp.logical_and(same, ri <= ci)]
    strict = [jnp.logical_and(same, ri > ci), jnp.logical_and(same, ri < ci)]
    tri = [jnp.where(m, 1.0, 0.0).astype(BF16) for m in incl]
    col_in_chunk = [ci < cq, ci >= cq]
    eye = jnp.where(ri == ci, 1.0, 0.0)

    def level_mask(sz):
        return jnp.logical_and((ri // (2 * sz)) == (ci // (2 * sz)), (ri // sz) != (ci // sz))

    gp = gp_ref[...]
    bias_row = gp[0:1, :]
    aneg_row = -jnp.exp(gp[1:2, :])
    lane = lax.broadcasted_iota(jnp.int32, (CONV_ROWS, LANES), 1)
    is_decay_lane = lane < 2 * SSD_HEADS + 2 * DN_HEADS

    def sum_parts(m):
        return m + pltpu.roll(m, LANES - 4, 1) + pltpu.roll(m, LANES - 8, 1)

    def gate_body(i, carry):
        for sub in range(DN_UG):
            rows = pl.ds(pl.multiple_of((i * DN_UG + sub) * unit, unit), unit)
            sm = sm_ref[rows, :].astype(F32)
            comb = jnp.where(is_decay_lane, aneg_row * _softplus(sm + bias_row), jax.nn.sigmoid(sm))
            hi = comb.astype(BF16)
            rest = comb - hi.astype(F32)
            mid = rest.astype(BF16)
            lo = (rest - mid.astype(F32)).astype(BF16)
            g3 = _dot(hi, sel_ref[0, 0]) + _dot(mid, sel_ref[0, 1]) + _dot(lo, sel_ref[0, 2])
            g3b = g3.astype(BF16)
            gfull[rows, :] = sum_parts(g3)
            cumf[rows, :] = sum_parts(_dot(tri[0], g3b))
            cumb[rows, :] = sum_parts(_dot(tri[1], g3b))
        return carry

    lax.fori_loop(0, n_units // DN_UG, gate_body, 0)

    def chain_id(j, d, s):
        return (j * 2 + d) * chains_per_dir + s

    def init_state(j, d):
        return s0_refs[d][0, j] if has_init else jnp.zeros((hd, hd), F32)

    for j in range(DN_HB):
        for d in range(2):
            for s in range(chains_per_dir):
                sst[chain_id(j, d, s)] = init_state(j, d)

    def inst(ul, j, d):
        return (ul * DN_HB + j) * 2 + d

    def unit_of(k, ul, d):
        u = k * DN_UG + ul
        return u if d == 0 else n_units - 1 - u

    zero_half = jnp.zeros((cq, hd), BF16)

    def solve_batch(k):
        for ul in range(DN_UG):
            for d in range(2):
                rows = pl.ds(pl.multiple_of(unit_of(k, ul, d) * unit, unit), unit)
                for j in range(DN_HB):
                    hs = slice(j * hd, (j + 1) * hd)
                    knb = ks[rows, hs].astype(BF16)
                    i = inst(ul, j, d)
                    kk_s[i] = _dot_nt(knb, knb)
                    qk_s[i] = _dot_nt(qs[rows, hs].astype(BF16), knb)
        for ul in range(DN_UG):
            for d in range(2):
                rows = pl.ds(pl.multiple_of(unit_of(k, ul, d) * unit, unit), unit)
                gf = gfull[rows, :]
                cum = (cumf if d == 0 else cumb)[rows, :]
                cum_t = cum.T
                tot = cumf[rows, :] + cumb[rows, :] - gf
                for j in range(DN_HB):
                    hs = slice(j * hd, (j + 1) * hd)
                    lane0 = DN_GATE_LANES * j
                    i = inst(ul, j, d)
                    qn = qs[rows, hs]
                    kn = ks[rows, hs]
                    gcol = cum[:, lane0 + d:lane0 + d + 1]
                    grow = cum_t[lane0 + d:lane0 + d + 1, :]
                    bcol = gf[:, lane0 + 2 + d:lane0 + 3 + d]
                    gtot = tot[:, lane0 + d:lane0 + d + 1]
                    dec = jnp.exp(jnp.where(incl[d], gcol - grow, NEG_BIG))
                    a_mat = jnp.where(strict[d], kk_s[i] * dec * bcol, 0.0)
                    a_s[i] = a_mat.astype(BF16)
                    t_s[i] = eye - jnp.where(level_mask(1), a_mat, 0.0)
                    eg = jnp.exp(gcol)
                    rhs_s[i] = jnp.concatenate([vs[rows, hs] * bcol, kn * (bcol * eg)], axis=1)
                    qk = (qk_s[i] * dec).astype(BF16)
                    qd = (qn * eg).astype(BF16)
                    kd_t = (kn * jnp.exp(gtot - gcol)).T
                    cd = jnp.exp(gtot)
                    for c in (0, 1):
                        rs = slice(c * cq, (c + 1) * cq)
                        w2[i, c * DN_W2_ROWS + cq:(c + 1) * DN_W2_ROWS, :] = qd[rs]
                        m2[i, c * DN_M2_ROWS:c * DN_M2_ROWS + cq, :] = qk[rs]
                        m2[i, c * DN_M2_ROWS + cq:(c + 1) * DN_M2_ROWS, :] = jnp.where(
                            col_in_chunk[c], kd_t, 0.0).astype(BF16)
                        cds[i, c * SUBLANES:(c + 1) * SUBLANES, :] = jnp.broadcast_to(
                            cd[c * cq:c * cq + SUBLANES], (SUBLANES, hd))
        sz = 2
        while sz < cq:
            mask = level_mask(sz)
            for i in range(DN_INST):
                p_s[i] = _dot(t_s[i].astype(BF16), jnp.where(mask, a_s[i], 0.0)).astype(BF16)
            for i in range(DN_INST):
                t = t_s[i]
                t_s[i] = t - _dot(p_s[i], t.astype(BF16))
            sz *= 2
        for i in range(DN_INST):
            rhs = rhs_s[i]
            sol = rhs + _dot((t_s[i] - eye).astype(BF16), rhs.astype(BF16))
            us[i] = sol[:, :hd]
            w = sol[:, hd:].astype(BF16)
            for c in (0, 1):
                w2[i, c * DN_W2_ROWS:c * DN_W2_ROWS + cq, :] = w[c * cq:(c + 1) * cq]

    def recur_batch(k):
        for t in range(2 * run_units):
            plan = []
            for j in range(DN_HB):
                for d in range(2):
                    for s in range(chains_per_dir):
                        ul = s * run_units + t // 2
                        c = t % 2 if d == 0 else 1 - t % 2
                        plan.append((chain_id(j, d, s), inst(ul, j, d), j, d, c,
                                     unit_of(k, ul, d) * unit + c * cq))
            for ch, i, j, d, c, pos in plan:
                first = (pos % seg_len == 0) if d == 0 else ((pos + cq) % seg_len == 0)
                s_val = jnp.where(first, init_state(j, d), sst[ch])
                sst[ch] = s_val
                tt = _dot(w2[i, c * DN_W2_ROWS:(c + 1) * DN_W2_ROWS, :], s_val.astype(BF16))
                v_new = (us[i, c * cq:(c + 1) * cq, :] - tt[0:cq]).astype(BF16)
                vp_s[ch] = jnp.concatenate([v_new, zero_half] if c == 0 else [zero_half, v_new], axis=0)
                op_s[ch] = tt[cq:2 * cq]
            for ch, i, j, d, c, pos in plan:
                t2 = _dot(m2[i, c * DN_M2_ROWS:(c + 1) * DN_M2_ROWS, :], vp_s[ch])
                orows = pl.ds(pl.multiple_of(pos, cq), cq)
                oacc[orows, j * hd:(j + 1) * hd] += op_s[ch] + t2[0:cq]
                s_new = sst[ch] * cds[i, c * SUBLANES:c * SUBLANES + 1, :] + t2[cq:cq + hd]
                sst[ch] = s_new
                if emit_state and t == 2 * run_units - 1:
                    s_out_refs[d][pos // seg_len, j] = s_new

    def batch(k, carry):
        solve_batch(k)
        recur_batch(k)
        return carry

    lax.fori_loop(0, n_units // DN_UG, batch, 0)

    def out_body(i, carry):
        rows = pl.ds(pl.multiple_of(i * CONV_ROWS, CONV_ROWS), CONV_ROWS)
        for j in range(DN_HB):
            hs = slice(j * hd, (j + 1) * hd)
            o = oacc[rows, hs]
            y_ref[rows, hs] = o * _rms_scale(o) * nw_ref[...] * _silu(z_ref[rows, hs].astype(F32))
        return carry

    lax.fori_loop(0, blk_len // CONV_ROWS, out_body, 0)


def _dn(proj, consts, n_blk, blk_len, seg_len, row_off, init, emit_state):
    rb = row_off // blk_len
    hd = DN_HEAD_DIM
    has_init = init is not None
    n_seg = blk_len // seg_len
    assert CONV_ROWS == DN_UNIT and (blk_len // DN_UNIT) % DN_UG == 0
    seg_units = seg_len // DN_UNIT
    assert DN_UG % seg_units == 0 or seg_units % DN_UG == 0
    n_chains = DN_HB * 2 * (DN_UG // min(seg_units, DN_UG))
    bw = DN_HB * hd
    n_hp = DN_HEADS // DN_HB
    cq0 = COL_QKV // bw
    in_specs = [
        pl.BlockSpec((blk_len, bw), lambda b, h: (rb + b, cq0 + h)),
        pl.BlockSpec((blk_len, bw), lambda b, h: (rb + b, cq0 + n_hp + h)),
        pl.BlockSpec((blk_len, bw), lambda b, h: (rb + b, cq0 + 2 * n_hp + h)),
        pl.BlockSpec((blk_len, bw), lambda b, h: (rb + b, COL_Z_D // bw + h)),
        pl.BlockSpec((blk_len, LANES), lambda b, h: (rb + b, COL_SMALL // LANES)),
        pl.BlockSpec((SUBLANES, bw), lambda b, h: (0, h)),
        pl.BlockSpec((1, bw), lambda b, h: (0, h)),
        pl.BlockSpec((SUBLANES, bw), lambda b, h: (0, n_hp + h)),
        pl.BlockSpec((1, bw), lambda b, h: (0, n_hp + h)),
        pl.BlockSpec((SUBLANES, bw), lambda b, h: (0, 2 * n_hp + h)),
        pl.BlockSpec((1, bw), lambda b, h: (0, 2 * n_hp + h)),
        pl.BlockSpec((SUBLANES, LANES), lambda b, h: (0, 0)),
        pl.BlockSpec((1, hd), lambda b, h: (0, 0)),
        pl.BlockSpec((1, 3, LANES, LANES), lambda b, h: (h, 0, 0, 0)),
    ]
    args = [proj, proj, proj, proj, proj,
            consts["cw_dn"], consts["cb_dn"], consts["cw_dn"], consts["cb_dn"], consts["cw_dn"],
            consts["cb_dn"], consts["gate_params"], consts["dn_norm_w"], consts["dn_sel"]]
    if has_init:
        assert n_seg == 1
        in_specs += [pl.BlockSpec((1, DN_HB, hd, hd), lambda b, h: (b, h, 0, 0))] * 2
        args += list(init)
    out_specs = [pl.BlockSpec((blk_len, bw), lambda b, h: (b, h))]
    out_shape = [jax.ShapeDtypeStruct((n_blk * blk_len, DN_WIDTH), F32)]
    if emit_state:
        out_specs += [pl.BlockSpec((n_seg, DN_HB, hd, hd), lambda b, h: (b, h, 0, 0))] * 2
        out_shape += [jax.ShapeDtypeStruct((n_blk * n_seg, DN_HEADS, hd, hd), F32)] * 2
    scratch = [
        pltpu.VMEM((blk_len + 2 * CONV_HALO, bw), F32),
        pltpu.VMEM((blk_len, bw), F32),
        pltpu.VMEM((blk_len, bw), F32),
        pltpu.VMEM((blk_len, bw), F32),
        pltpu.VMEM((blk_len, LANES), F32),
        pltpu.VMEM((blk_len, LANES), F32),
        pltpu.VMEM((blk_len, LANES), F32),
        pltpu.VMEM((blk_len, bw), F32),
        pltpu.VMEM((n_chains, hd, hd), F32),
        pltpu.VMEM((DN_INST, DN_UNIT, DN_UNIT), F32),
        pltpu.VMEM((DN_INST, DN_UNIT, DN_UNIT), F32),
        pltpu.VMEM((DN_INST, DN_UNIT, DN_UNIT), BF16),
        pltpu.VMEM((DN_INST, DN_UNIT, DN_UNIT), BF16),
        pltpu.VMEM((DN_INST, DN_UNIT, DN_UNIT), F32),
        pltpu.VMEM((DN_INST, DN_UNIT, 2 * hd), F32),
        pltpu.VMEM((DN_INST, 2 * DN_W2_ROWS, hd), BF16),
        pltpu.VMEM((DN_INST, 2 * DN_M2_ROWS, hd), BF16),
        pltpu.VMEM((DN_INST, DN_UNIT, hd), F32),
        pltpu.VMEM((DN_INST, 2 * SUBLANES, hd), F32),
        pltpu.VMEM((n_chains, DN_UNIT, hd), BF16),
        pltpu.VMEM((n_chains, DN_CHUNK, hd), F32),
    ]
    return pl.pallas_call(
        functools.partial(_dn_kernel, blk_len=blk_len, seg_len=seg_len, has_init=has_init,
                          emit_state=emit_state),
        grid=(n_blk, n_hp),
        in_specs=in_specs,
        out_specs=out_specs,
        out_shape=out_shape,
        scratch_shapes=scratch,
        compiler_params=_cparams(("arbitrary", "arbitrary")),
        name="dn_scan",
    )(*args)


ROUTE_ID_LANE = 0
ROUTE_GATE_LANE = 2


def _outproj_kernel(ysc_ref, ysl_ref, ydc_ref, ydl_ref, x0_ref, mod_ref, snw_ref, n2w_ref, wo_ref,
                    wr_ref, br_ref, x1_ref, h2_ref, route_ref, *, ctx_tiles):
    is_ctx = pl.program_id(0) < ctx_tiles
    ys = jnp.where(is_ctx, ysc_ref[...], ysl_ref[...])
    ysn = (ys * _rms_scale(ys) * snw_ref[...]).astype(BF16)
    ydn = jnp.where(is_ctx, ydc_ref[...], ydl_ref[...]).astype(BF16)
    m = _dot(ysn, wo_ref[0:SSD_WIDTH, :]) + _dot(ydn, wo_ref[SSD_WIDTH:, :])
    mod = mod_ref[0]
    x1 = x0_ref[...] + mod[2:3] * m
    x1_ref[...] = x1
    h2 = (x1 * _rms_scale(x1) * n2w_ref[...]) * (1.0 + mod[4:5]) + mod[3:4]
    h2_ref[...] = _pack_rows(h2)
    logits = _dot(h2.astype(BF16), wr_ref[...]) + br_ref[...]
    ln = lax.broadcasted_iota(jnp.int32, logits.shape, 1)
    is_g = ln < N_EXPERT_GROUPS
    gl = jnp.where(is_g, logits, NEG_BIG)
    gmax = jnp.max(gl, axis=-1, keepdims=True)
    gidx = jnp.min(jnp.where(gl == gmax, ln, LANES), axis=-1, keepdims=True)
    gw = 1.0 / jnp.sum(jnp.where(is_g, jnp.exp(gl - gmax), 0.0), axis=-1, keepdims=True)
    lo = N_EXPERT_GROUPS + gidx * EXPERTS_PER_GROUP
    in_grp = (ln >= lo) & (ln < lo + EXPERTS_PER_GROUP)
    el = jnp.where(in_grp, logits, NEG_BIG)
    m1 = jnp.max(el, axis=-1, keepdims=True)
    i1 = jnp.min(jnp.where(el == m1, ln, LANES), axis=-1, keepdims=True)
    el2 = jnp.where(ln == i1, NEG_BIG, el)
    m2 = jnp.max(el2, axis=-1, keepdims=True)
    i2 = jnp.min(jnp.where(el2 == m2, ln, LANES), axis=-1, keepdims=True)
    t = jnp.exp(m2 - m1)
    g1 = gw / (1.0 + t)
    g2 = gw * t / (1.0 + t)
    id1 = (i1 - N_EXPERT_GROUPS).astype(F32)
    id2 = (i2 - N_EXPERT_GROUPS).astype(F32)
    route = jnp.where(ln == 0, id1, jnp.where(ln == 1, id2, jnp.where(ln == 2, g1,
                                                                      jnp.where(ln == 3, g2, 0.0))))
    route_ref[...] = route


def _outproj(yssd_ctx, yssd_lat, ydn_ctx, ydn_lat, x0, mod, consts, lat_seq):
    n_total = x0.shape[0]
    tm = ROW_TILE
    ctx_tiles = yssd_ctx.shape[0] // tm
    ctx_row, lat_row, mod_row = _stream_maps(ctx_tiles, lat_seq // tm)
    row = lambda i: (i, 0)
    const2 = lambda i: (0, 0)
    return pl.pallas_call(
        functools.partial(_outproj_kernel, ctx_tiles=ctx_tiles),
        grid=(n_total // tm,),
        in_specs=[pl.BlockSpec((tm, SSD_WIDTH), ctx_row),
                  pl.BlockSpec((tm, SSD_WIDTH), lat_row),
                  pl.BlockSpec((tm, DN_WIDTH), ctx_row),
                  pl.BlockSpec((tm, DN_WIDTH), lat_row),
                  pl.BlockSpec((tm, D_MODEL), row),
                  pl.BlockSpec((1, N_MOD, D_MODEL), mod_row),
                  pl.BlockSpec((1, SSD_WIDTH), const2),
                  pl.BlockSpec((1, D_MODEL), const2),
                  pl.BlockSpec((SSD_WIDTH + DN_WIDTH, D_MODEL), const2),
                  pl.BlockSpec((D_MODEL, LANES), const2),
                  pl.BlockSpec((1, LANES), const2)],
        out_specs=[pl.BlockSpec((tm, D_MODEL), row),
                   pl.BlockSpec((tm, PACK_COLS), row),
                   pl.BlockSpec((tm, LANES), row)],
        out_shape=[jax.ShapeDtypeStruct((n_total, D_MODEL), F32),
                   jax.ShapeDtypeStruct((n_total, PACK_COLS), jnp.uint32),
                   jax.ShapeDtypeStruct((n_total, LANES), F32)],
        compiler_params=_cparams(("arbitrary",)),
        name="outproj_router",
    )(yssd_ctx, yssd_lat, ydn_ctx, ydn_lat, x0, mod, consts["ssd_norm_w"], consts["norm2_w"],
      consts["w_out"], consts["w_router"], consts["b_router"])


def _rank_kernel(route_ref, rank_ref, count_ref, carry):
    i = pl.program_id(0)

    @pl.when(i == 0)
    def _():
        carry[...] = jnp.zeros(carry.shape, F32)

    route = route_ref[...]
    n = route.shape[0]
    ln = lax.broadcasted_iota(jnp.int32, route.shape, 1)
    id1 = route[:, 0:1].astype(jnp.int32)
    id2 = route[:, 1:2].astype(jnp.int32)
    hit1 = ln == id1
    hit2 = ln == id2
    onehot = jnp.where(hit1, 1.0, jnp.where(hit2, 1.0, 0.0))
    ri = lax.broadcasted_iota(jnp.int32, (n, n), 0)
    ci = lax.broadcasted_iota(jnp.int32, (n, n), 1)
    before = (ri > ci).astype(BF16)
    tot = _dot(before, onehot.astype(BF16)) + carry[0:1, :]
    r1 = jnp.sum(jnp.where(hit1, tot, 0.0), axis=-1, keepdims=True)
    r2 = jnp.sum(jnp.where(hit2, tot, 0.0), axis=-1, keepdims=True)
    info = jnp.where(ln < 2, route, jnp.where(ln == 2, r1, jnp.where(ln == 3, r2, 0.0)))
    rank_ref[0] = info.T[0:SUBLANES, :]
    new = carry[...] + jnp.sum(onehot, axis=0, keepdims=True)
    carry[...] = new
    count_ref[...] = new


def _ranks(route):
    n_total = route.shape[0]
    tm = ROW_TILE
    return pl.pallas_call(
        _rank_kernel,
        grid=(n_total // tm,),
        in_specs=[pl.BlockSpec((tm, LANES), lambda i: (i, 0))],
        out_specs=[pl.BlockSpec((1, SUBLANES, tm), lambda i: (i, 0, 0)),
                   pl.BlockSpec((SUBLANES, LANES), lambda i: (0, 0))],
        out_shape=[jax.ShapeDtypeStruct((n_total // tm, SUBLANES, tm), F32),
                   jax.ShapeDtypeStruct((SUBLANES, LANES), F32)],
        scratch_shapes=[pltpu.VMEM((SUBLANES, LANES), F32)],
        compiler_params=_cparams(("arbitrary",)),
        name="slot_ranks",
    )(route)


def _scatter_kernel(pos_ref, h_ref, xs_in_ref, xs_ref, sem):
    del xs_in_ref
    n = h_ref.shape[0]

    def body(tb, carry):
        for u in range(DMA_UNROLL):
            t = tb * DMA_UNROLL + u
            for k in range(2):
                p = pos_ref[0, 0, k * n + t]
                pltpu.make_async_copy(h_ref.at[pl.ds(t, 1), :], xs_ref.at[pl.ds(p, 1), :],
                                      sem).start(priority=k)
        return carry

    lax.fori_loop(0, n // DMA_UNROLL, body, 0)
    for _ in range(2):
        pltpu.make_async_copy(h_ref, xs_ref.at[pl.ds(0, n), :], sem).wait()


def _scatter_rows(pos3, h2, n_sorted):
    n_total = h2.shape[0]
    tm = ROW_TILE
    zeros = jnp.zeros((n_sorted, PACK_COLS), jnp.uint32)
    return pl.pallas_call(
        _scatter_kernel,
        grid=(n_total // tm,),
        in_specs=[pl.BlockSpec((1, 1, 2 * tm), lambda i: (i, 0, 0), memory_space=pltpu.SMEM),
                  pl.BlockSpec((tm, PACK_COLS), lambda i: (i, 0)),
                  pl.BlockSpec(memory_space=pl.ANY)],
        out_specs=pl.BlockSpec(memory_space=pl.ANY),
        out_shape=jax.ShapeDtypeStruct((n_sorted, PACK_COLS), jnp.uint32),
        scratch_shapes=[pltpu.SemaphoreType.DMA(())],
        input_output_aliases={2: 0},
        compiler_params=_cparams(("arbitrary",)),
        name="moe_scatter",
    )(pos3, h2, zeros)


def _ffn_kernel(te_ref, tv_ref, x_ref, wg_ref, wu_ref, wd_ref, y_ref, wgb, wub, wdb):
    i = pl.program_id(0)
    prev = te_ref[jnp.maximum(i - 1, 0)]

    @pl.when(jnp.logical_or(i == 0, te_ref[i] != prev))
    def _():
        wgb[...] = wg_ref[0].astype(BF16)
        wub[...] = wu_ref[0].astype(BF16)
        wdb[...] = wd_ref[0].astype(BF16)

    @pl.when(tv_ref[i] > 0)
    def _():
        x_lo, x_hi = _unpack_rows(x_ref[...])
        xb = jnp.concatenate([x_lo.astype(BF16), x_hi.astype(BF16)], axis=1)
        a = _dot(xb, wgb[...])
        u = _dot(xb, wub[...])
        act = (_silu(a) * u).astype(BF16)
        y_ref[...] = _pack_rows(_dot(act, wdb[...]))

    @pl.when(tv_ref[i] == 0)
    def _():
        y_ref[...] = jnp.zeros(y_ref.shape, jnp.uint32)


def _grouped_ffn(tile_expert, tile_valid, xs, w_gate, w_up, w_down):
    n_sorted = xs.shape[0]
    tm = MOE_TILE
    grid_spec = pltpu.PrefetchScalarGridSpec(
        num_scalar_prefetch=2,
        grid=(n_sorted // tm,),
        in_specs=[pl.BlockSpec((tm, PACK_COLS), lambda i, te, tv: (i, 0)),
                  pl.BlockSpec((1, D_MODEL, D_FF), lambda i, te, tv: (te[i], 0, 0)),
                  pl.BlockSpec((1, D_MODEL, D_FF), lambda i, te, tv: (te[i], 0, 0)),
                  pl.BlockSpec((1, D_FF, D_MODEL), lambda i, te, tv: (te[i], 0, 0))],
        out_specs=pl.BlockSpec((tm, PACK_COLS), lambda i, te, tv: (i, 0)),
        scratch_shapes=[pltpu.VMEM((D_MODEL, D_FF), BF16),
                        pltpu.VMEM((D_MODEL, D_FF), BF16),
                        pltpu.VMEM((D_FF, D_MODEL), BF16)],
    )
    return pl.pallas_call(
        _ffn_kernel,
        grid_spec=grid_spec,
        out_shape=jax.ShapeDtypeStruct((n_sorted, PACK_COLS), jnp.uint32),
        compiler_params=_cparams(("arbitrary",)),
        name="moe_ffn",
    )(tile_expert, tile_valid, xs, w_gate, w_up, w_down)


def _combine_kernel(pos_ref, route_ref, x1_ref, mod_ref, fw_ref, ys_ref, out_ref, buf, sem):
    n = x1_ref.shape[0]

    def body(tb, carry):
        for u in range(DMA_UNROLL):
            t = tb * DMA_UNROLL + u
            for k in range(2):
                p = pos_ref[0, 0, k * n + t]
                pltpu.make_async_copy(ys_ref.at[pl.ds(p, 1), :], buf.at[k, pl.ds(t, 1), :],
                                      sem).start(priority=k)
        return carry

    lax.fori_loop(0, n // DMA_UNROLL, body, 0)
    for k in range(2):
        pltpu.make_async_copy(ys_ref.at[pl.ds(0, n), :], buf.at[k], sem).wait()
    route = route_ref[...]
    g1 = route[:, ROUTE_GATE_LANE:ROUTE_GATE_LANE + 1]
    g2 = route[:, ROUTE_GATE_LANE + 1:ROUTE_GATE_LANE + 2]
    y1_lo, y1_hi = _unpack_rows(buf[0])
    y2_lo, y2_hi = _unpack_rows(buf[1])
    moe = jnp.concatenate([g1 * y1_lo + g2 * y2_lo, g1 * y1_hi + g2 * y2_hi], axis=1)
    x2 = x1_ref[...] + mod_ref[0][5:6] * moe
    out_ref[...] = x2 * _rms_scale(x2) * fw_ref[...]


def _combine(pos3, route, x1, mod, final_w, ys, n_rows, row_off, tokens_per_mod, mod_off):
    tm = ROW_TILE
    off = row_off // tm
    per = tokens_per_mod // tm
    return pl.pallas_call(
        _combine_kernel,
        grid=(n_rows // tm,),
        in_specs=[pl.BlockSpec((1, 1, 2 * tm), lambda i: (off + i, 0, 0), memory_space=pltpu.SMEM),
                  pl.BlockSpec((tm, LANES), lambda i: (off + i, 0)),
                  pl.BlockSpec((tm, D_MODEL), lambda i: (off + i, 0)),
                  pl.BlockSpec((1, N_MOD, D_MODEL), lambda i: (mod_off + i // per, 0, 0)),
                  pl.BlockSpec((1, D_MODEL), lambda i: (0, 0)),
                  pl.BlockSpec(memory_space=pl.ANY)],
        out_specs=pl.BlockSpec((tm, D_MODEL), lambda i: (i, 0)),
        out_shape=jax.ShapeDtypeStruct((n_rows, D_MODEL), F32),
        scratch_shapes=[pltpu.VMEM((2, tm, PACK_COLS), jnp.uint32), pltpu.SemaphoreType.DMA(())],
        compiler_params=_cparams(("arbitrary",)),
        name="moe_combine",
    )(pos3, route, x1, mod, final_w, ys)


def _grid_sincos_2d(n_tokens):
    rows = n_tokens // GRID_W
    quarter = D_MODEL // 4
    omega = (1.0 / (np.float32(POS_BASE) ** (np.arange(quarter, dtype=np.float32) / np.float32(quarter)))).astype(np.float32)
    r = np.repeat(np.arange(rows, dtype=np.float32), GRID_W)
    c = np.tile(np.arange(GRID_W, dtype=np.float32), rows)
    ar = (r[:, None] * omega).astype(np.float32)
    ac = (c[:, None] * omega).astype(np.float32)
    return jnp.asarray(np.concatenate([np.sin(ar), np.cos(ar), np.sin(ac), np.cos(ac)], axis=-1).astype(np.float32))


def _pad_rows(a, n):
    return jnp.concatenate([a, jnp.zeros((n - a.shape[0],) + a.shape[1:], a.dtype)], axis=0)


def _selection_constants():
    gh = SSD_GROUP_HEADS
    ssd_sel = np.zeros((SSD_GROUPS, 3, LANES, LANES), np.float32)
    for g in range(SSD_GROUPS):
        for part in range(3):
            for d in range(2):
                for j in range(gh):
                    ssd_sel[g, part, d * SSD_HEADS + g * gh + j, SSD_GATE_LANES * part + d * gh + j] = 1.0
    ssd_exp = np.zeros((2, LANES, SSD_GROUP_COLS), np.float32)
    for d in range(2):
        for part in range(3):
            for c in range(SSD_GROUP_COLS):
                ssd_exp[d, SSD_GATE_LANES * part + d * gh + c // SSD_HEAD_DIM, c] = 1.0
    dn_sel = np.zeros((DN_HEADS // DN_HB, 3, LANES, LANES), np.float32)
    base = 2 * SSD_HEADS
    for h in range(DN_HEADS):
        dst = DN_GATE_LANES * (h % DN_HB)
        for part in range(3):
            for d in range(2):
                dn_sel[h // DN_HB, part, base + d * DN_HEADS + h, dst + 4 * part + d] = 1.0
                dn_sel[h // DN_HB, part, base + 2 * DN_HEADS + d * DN_HEADS + h, dst + 4 * part + 2 + d] = 1.0
    return ssd_sel, ssd_exp, dn_sel


def _ssd_consts(conv_w, conv_b, ssd_d):
    bcw, bcb = [], []
    for g in range(SSD_GROUPS):
        for base in (SSD_WIDTH, SSD_WIDTH + SSD_GROUPS * SSD_STATE):
            sl = slice(base + g * SSD_STATE, base + (g + 1) * SSD_STATE)
            bcw.append(conv_w[:, sl])
            bcb.append(conv_b[sl])
    ssd_sel, ssd_exp, _ = _selection_constants()
    return {
        "cw_x": _pad_rows(conv_w[:, 0:SSD_WIDTH], SUBLANES),
        "cb_x": conv_b[0:SSD_WIDTH].reshape(1, -1),
        "cw_bc": _pad_rows(jnp.concatenate(bcw, axis=1), SUBLANES),
        "cb_bc": jnp.concatenate(bcb).reshape(1, -1),
        "d_exp": jnp.repeat(ssd_d, SSD_HEAD_DIM).reshape(1, -1),
        "ssd_sel": jnp.asarray(ssd_sel, dtype=BF16),
        "ssd_exp": jnp.asarray(ssd_exp, dtype=BF16),
    }


def _dn_consts(conv_w, conv_b, ssd_dt_bias, ssd_a_log, dn_dt_bias, dn_a_log, dn_norm_w):
    n_ssd = 2 * SSD_HEADS
    n_dn = 2 * DN_HEADS
    gate_params = jnp.zeros((SUBLANES, LANES), F32)
    gate_params = gate_params.at[0, 0:n_ssd].set(ssd_dt_bias.reshape(-1))
    gate_params = gate_params.at[0, n_ssd:n_ssd + n_dn].set(dn_dt_bias.reshape(-1))
    gate_params = gate_params.at[1, 0:n_ssd].set(ssd_a_log.reshape(-1))
    gate_params = gate_params.at[1, n_ssd:n_ssd + n_dn].set(dn_a_log.reshape(-1))
    dn_sel = _selection_constants()[2]
    return {
        "cw_dn": _pad_rows(conv_w, SUBLANES),
        "cb_dn": conv_b.reshape(1, -1),
        "gate_params": gate_params,
        "dn_sel": jnp.asarray(dn_sel, dtype=BF16),
        "dn_norm_w": dn_norm_w.reshape(1, -1),
    }


def kernel(x_prompt, x_sample, state_ssd_fwd, state_ssd_bwd, state_dn_fwd, state_dn_bwd, c, c_ctx, w_ada, b_ada, norm1_w, w_in, conv_ssd_w, conv_ssd_b, conv_dn_w, conv_dn_b, ssd_dt_bias, ssd_a_log, ssd_d, ssd_norm_w, dn_dt_bias, dn_a_log, dn_norm_w, w_out, norm2_w, w_router_group, b_router_group, w_router_expert, b_router_expert, w_gate, w_up, w_down, final_norm_w):
    n_b, seq, _ = x_prompt.shape
    dec_b, dec_seq, _ = x_sample.shape
    n_ctx = n_b * seq
    n_lat = dec_b * dec_seq
    n_total = n_ctx + n_lat
    layer = 0

    wi = w_in[layer]
    xbc0 = SSD_WIDTH
    bm0 = xbc0 + SSD_WIDTH
    cm0 = bm0 + SSD_GROUPS * SSD_STATE
    dt0 = cm0 + SSD_GROUPS * SSD_STATE
    qkv0 = dt0 + 2 * SSD_HEADS
    zd0 = qkv0 + 3 * DN_WIDTH
    ad0 = zd0 + DN_WIDTH
    bd0 = ad0 + 2 * DN_HEADS
    bc_cols = []
    for g in range(SSD_GROUPS):
        bc_cols += [wi[:, bm0 + g * SSD_STATE:bm0 + (g + 1) * SSD_STATE],
                    wi[:, cm0 + g * SSD_STATE:cm0 + (g + 1) * SSD_STATE]]
    w_r = jnp.concatenate(
        [wi[:, 0:SSD_WIDTH], wi[:, xbc0:bm0]] + bc_cols +
        [wi[:, qkv0:zd0], wi[:, zd0:ad0], wi[:, dt0:qkv0], wi[:, ad0:bd0], wi[:, bd0:bd0 + 2 * DN_HEADS],
         jnp.zeros((D_MODEL, PROJ_COLS - COL_SMALL - 2 * SSD_HEADS - 4 * DN_HEADS), F32)],
        axis=1).astype(BF16)

    w_router = jnp.concatenate(
        [w_router_group[layer], w_router_expert[layer],
         jnp.zeros((D_MODEL, LANES - N_EXPERT_GROUPS - N_EXPERTS), F32)], axis=1)
    b_router = jnp.concatenate(
        [b_router_group[layer], b_router_expert[layer],
         jnp.zeros((LANES - N_EXPERT_GROUPS - N_EXPERTS,), F32)]).reshape(1, LANES)
    consts = {
        "ssd_norm_w": ssd_norm_w[layer].reshape(1, -1),
        "norm2_w": norm2_w[layer].reshape(1, -1),
        "w_out": w_out[layer].astype(BF16),
        "w_router": w_router.astype(BF16),
        "b_router": b_router,
    }
    consts.update(_ssd_consts(conv_ssd_w[layer], conv_ssd_b[layer], ssd_d[layer]))
    consts.update(_dn_consts(conv_dn_w[layer], conv_dn_b[layer], ssd_dt_bias[layer], ssd_a_log[layer],
                             dn_dt_bias[layer], dn_a_log[layer], dn_norm_w[layer]))

    cond = _pad_rows(jnp.concatenate([c_ctx[None, :], c], axis=0), SUBLANES)
    mod = _adaln(cond, w_ada[layer], b_ada[layer]).reshape(SUBLANES, N_MOD, D_MODEL)

    pos = _grid_sincos_2d(dec_seq)
    n1w = norm1_w[layer].reshape(1, -1)
    proj, x0 = _inproj(x_prompt.reshape(n_ctx, D_MODEL), x_sample.reshape(n_lat, D_MODEL), pos, mod, n1w,
                       w_r, dec_seq)

    def ssd_state_in(s):
        return s[:, layer].transpose(0, 3, 1, 2).reshape(dec_b, SSD_STATE, SSD_WIDTH)

    yssd_ctx, hf, hb = _ssd(proj, consts, n_b, seq, 0, None, True)
    yssd_lat = _ssd(proj, consts, dec_b, dec_seq, n_ctx,
                    (ssd_state_in(state_ssd_fwd), ssd_state_in(state_ssd_bwd)), False)[0]
    ydn_ctx, sf, sb = _dn(proj, consts, n_ctx // dec_seq, dec_seq, seq, 0, None, True)
    ydn_lat = _dn(proj, consts, dec_b, dec_seq, dec_seq, n_ctx,
                  (state_dn_fwd[:, layer], state_dn_bwd[:, layer]), False)[0]

    x1, h2, route = _outproj(yssd_ctx, yssd_lat, ydn_ctx, ydn_lat, x0, mod, consts, dec_seq)

    slots, counts = _ranks(route)
    counts = counts[0, :N_EXPERTS].astype(jnp.int32)
    padded = ((counts + MOE_TILE - 1) // MOE_TILE) * MOE_TILE
    ends = jnp.cumsum(padded)
    offsets = ends - padded
    n_sorted = 2 * n_total + N_EXPERTS * MOE_TILE
    ids = slots[:, 0:2, :].astype(jnp.int32)
    expert_ids = jnp.arange(N_EXPERTS, dtype=jnp.int32)
    slot_base = jnp.sum(jnp.where(ids[..., None] == expert_ids, offsets, 0), axis=-1)
    pos_slots = slot_base + slots[:, 2:4, :].astype(jnp.int32)
    pos3 = pos_slots.reshape(n_total // ROW_TILE, 1, 2 * ROW_TILE)
    tile_start = jnp.arange(n_sorted // MOE_TILE, dtype=jnp.int32) * MOE_TILE
    tile_expert = jnp.minimum(jnp.sum(tile_start[:, None] >= ends[None, :], axis=1), N_EXPERTS - 1).astype(jnp.int32)
    tile_valid = (tile_start < ends[-1]).astype(jnp.int32)

    xs = _scatter_rows(pos3, h2, n_sorted)
    ys = _grouped_ffn(tile_expert, tile_valid, xs, w_gate[layer], w_up[layer], w_down[layer])
    fw = final_norm_w.reshape(1, -1)
    y_prompt = _combine(pos3, route, x1, mod, fw, ys, n_ctx, 0, n_ctx, 0)
    y_sample = _combine(pos3, route, x1, mod, fw, ys, n_lat, n_ctx, dec_seq, 1)

    def ssd_state_out(s):
        return s.reshape(n_b, SSD_STATE, SSD_HEADS, SSD_HEAD_DIM).transpose(0, 2, 3, 1)[:, None]

    return (y_prompt.reshape(n_b, seq, D_MODEL), y_sample.reshape(dec_b, dec_seq, D_MODEL),
            ssd_state_out(hf), ssd_state_out(hb), sf[:, None], sb[:, None])
```

```python
import functools

import numpy as np
import jax
import jax.numpy as jnp
from jax import lax
from jax.experimental import pallas as pl
from jax.experimental.pallas import tpu as pltpu

F32 = jnp.float32
BF16 = jnp.bfloat16
HI = lax.Precision.HIGHEST

D_MODEL = 1024
EPS = 1e-6
GRID_W = 64
POS_BASE = 10000.0
CONV_W = 5
N_MOD = 6
SSD_HEADS = 16
SSD_HEAD_DIM = 64
SSD_WIDTH = SSD_HEADS * SSD_HEAD_DIM
SSD_GROUPS = 2
SSD_STATE = 64
SSD_CHUNK = 128
DN_HEADS = 8
DN_HEAD_DIM = 128
DN_WIDTH = DN_HEADS * DN_HEAD_DIM
DN_CHUNK = 64
N_EXPERT_GROUPS = 4
EXPERTS_PER_GROUP = 8
N_EXPERTS = N_EXPERT_GROUPS * EXPERTS_PER_GROUP
D_FF = 512

LANES = 128
SUBLANES = 8
VMEM_LIMIT = 56 * 1024 * 1024

COL_Z_S = 0
COL_X = 1024
COL_BC = 2048
COL_QKV = 2304
COL_Z_D = 5376
COL_SMALL = 6400
PROJ_COLS = 6528

ROW_TILE = 256
MOE_TILE = 256
DMA_UNROLL = 8
NEG_BIG = -1e30


def _dot(a, b, prec=None):
    return jnp.dot(a, b, preferred_element_type=F32, precision=prec)


def _dot_nt(a, b, prec=None):
    return lax.dot_general(a, b, (((1,), (1,)), ((), ())), preferred_element_type=F32, precision=prec)


def _dot_tn(a, b, prec=None):
    return lax.dot_general(a, b, (((0,), (0,)), ((), ())), preferred_element_type=F32, precision=prec)


def _silu(x):
    return x * jax.nn.sigmoid(x)


def _softplus(x):
    return jnp.maximum(x, 0.0) + jnp.log1p(jnp.exp(-jnp.abs(x)))


def _rms_scale(x):
    return lax.rsqrt(jnp.mean(x * x, axis=-1, keepdims=True) + EPS)


PACK_COLS = D_MODEL // 2
HIGH_HALF = 0xFFFF0000


def _pack_rows(x):
    lo = pltpu.bitcast(x[:, :PACK_COLS].astype(BF16).astype(F32), jnp.uint32)
    hi = pltpu.bitcast(x[:, PACK_COLS:].astype(BF16).astype(F32), jnp.uint32)
    return (lo >> 16) | (hi & jnp.uint32(HIGH_HALF))


def _unpack_rows(u):
    return pltpu.bitcast(u << 16, F32), pltpu.bitcast(u & jnp.uint32(HIGH_HALF), F32)


def _cparams(sem, vmem=VMEM_LIMIT):
    return pltpu.CompilerParams(dimension_semantics=sem, vmem_limit_bytes=vmem)


def _ada_kernel(c_ref, w_ref, b_ref, o_ref):
    c = c_ref[...]
    o_ref[...] = _dot(_silu(c), w_ref[...], HI) + b_ref[...]


def _adaln(cond, w_ada, b_ada):
    n_out = N_MOD * D_MODEL
    tn = 1536
    return pl.pallas_call(
        _ada_kernel,
        grid=(n_out // tn,),
        in_specs=[pl.BlockSpec((SUBLANES, D_MODEL), lambda j: (0, 0)),
                  pl.BlockSpec((D_MODEL, tn), lambda j: (0, j)),
                  pl.BlockSpec((1, tn), lambda j: (0, j))],
        out_specs=pl.BlockSpec((SUBLANES, tn), lambda j: (0, j)),
        out_shape=jax.ShapeDtypeStruct((SUBLANES, n_out), F32),
        compiler_params=_cparams(("arbitrary",)),
        name="adaln",
    )(cond, w_ada, b_ada.reshape(1, n_out))


INPROJ_TILE = 512
INPROJ_CHUNK = 1280


def _inproj_kernel(xc_ref, xl_ref, pos_ref, mod_ref, nw_ref, w_ref, proj_ref, small_ref, x0_ref, *,
                   ctx_tiles):
    is_ctx = pl.program_id(0) < ctx_tiles
    x = jnp.where(is_ctx, xc_ref[...], xl_ref[...] + pos_ref[...])
    x0_ref[...] = x
    mod = mod_ref[0]
    hb = ((x * _rms_scale(x) * nw_ref[...]) * (1.0 + mod[1:2]) + mod[0:1]).astype(BF16)
    for j in range(0, COL_SMALL, INPROJ_CHUNK):
        proj_ref[:, j:j + INPROJ_CHUNK] = _dot(hb, w_ref[:, j:j + INPROJ_CHUNK]).astype(BF16)
    small_ref[...] = _dot(hb, w_ref[:, COL_SMALL:])


def _stream_maps(ctx_tiles, lat_tiles_per_mod):
    def ctx_row(i):
        return (jnp.minimum(i, ctx_tiles - 1), 0)

    def lat_row(i):
        return (jnp.maximum(i - ctx_tiles, 0), 0)

    def mod_row(i):
        return (jnp.where(i < ctx_tiles, 0, 1 + (i - ctx_tiles) // lat_tiles_per_mod), 0, 0)

    return ctx_row, lat_row, mod_row


def _inproj(x_ctx, x_lat, pos, mod, norm_w, w_r, lat_seq):
    tm = INPROJ_TILE
    ctx_tiles = x_ctx.shape[0] // tm
    n_total = x_ctx.shape[0] + x_lat.shape[0]
    n_pos = pos.shape[0] // tm
    ctx_row, lat_row, mod_row = _stream_maps(ctx_tiles, lat_seq // tm)
    return pl.pallas_call(
        functools.partial(_inproj_kernel, ctx_tiles=ctx_tiles),
        grid=(n_total // tm,),
        in_specs=[pl.BlockSpec((tm, D_MODEL), ctx_row),
                  pl.BlockSpec((tm, D_MODEL), lat_row),
                  pl.BlockSpec((tm, D_MODEL), lambda i: (jnp.maximum(i - ctx_tiles, 0) % n_pos, 0)),
                  pl.BlockSpec((1, N_MOD, D_MODEL), mod_row),
                  pl.BlockSpec((1, D_MODEL), lambda i: (0, 0)),
                  pl.BlockSpec((D_MODEL, PROJ_COLS), lambda i: (0, 0), pipeline_mode=pl.Buffered(1))],
        out_specs=[pl.BlockSpec((tm, COL_SMALL), lambda i: (i, 0)),
                   pl.BlockSpec((tm, LANES), lambda i: (i, 0)),
                   pl.BlockSpec((tm, D_MODEL), lambda i: (i, 0))],
        out_shape=[jax.ShapeDtypeStruct((n_total, COL_SMALL), BF16),
                   jax.ShapeDtypeStruct((n_total, LANES), F32),
                   jax.ShapeDtypeStruct((n_total, D_MODEL), F32)],
        compiler_params=_cparams(("arbitrary",)),
        name="inproj",
    )(x_ctx, x_lat, pos, mod, norm_w, w_r)


CONV_ROWS = 128
CONV_HALO = SUBLANES
CONV_UNROLL = 2
CONV_UNROLL_MAX_COLS = 256


def _conv_silu(src_ref, w_ref, b_ref, pad_ref, store, blk_len, seg_len):
    n_ch = src_ref.shape[1]
    n_seg = blk_len // seg_len
    chunks_per_seg = seg_len // CONV_ROWS
    zeros = jnp.zeros((CONV_HALO, n_ch), F32)
    for s in range(n_seg + 1):
        gap = s * (seg_len + CONV_HALO)
        pad_ref[gap:gap + CONV_HALO, :] = zeros

    def window_start(i):
        return pl.multiple_of(i * CONV_ROWS + (i // chunks_per_seg) * CONV_HALO, CONV_HALO)

    def copy_body(i, carry):
        r0 = pl.multiple_of(i * CONV_ROWS, CONV_ROWS)
        pad_ref[pl.ds(window_start(i) + CONV_HALO, CONV_ROWS), :] = src_ref[pl.ds(r0, CONV_ROWS), :].astype(F32)
        return carry

    lax.fori_loop(0, blk_len // CONV_ROWS, copy_body, 0)
    win = CONV_ROWS + 2 * CONV_HALO
    w = w_ref[...]
    b = b_ref[...]

    unroll = CONV_UNROLL if n_ch <= CONV_UNROLL_MAX_COLS else 1

    def body(i2, carry):
        for sub in range(unroll):
            i = i2 * unroll + sub
            v = pad_ref[pl.ds(window_start(i), win), :]
            acc = jnp.zeros((CONV_ROWS, n_ch), F32) + b
            for k in range(CONV_W):
                shift = (CONV_W // 2 - k) % win
                sh = v if shift == 0 else pltpu.roll(v, shift, 0)
                acc = acc + w[k:k + 1, :] * sh[CONV_HALO:CONV_HALO + CONV_ROWS, :]
            store(pl.ds(pl.multiple_of(i * CONV_ROWS, CONV_ROWS), CONV_ROWS), _silu(acc))
        return carry

    lax.fori_loop(0, blk_len // (CONV_ROWS * unroll), body, 0)


def _conv_pad_rows(blk_len, seg_len):
    return blk_len + (blk_len // seg_len + 1) * CONV_HALO


SSD_GROUP_HEADS = SSD_HEADS // SSD_GROUPS
SSD_GROUP_COLS = SSD_GROUP_HEADS * SSD_HEAD_DIM
SSD_GATE_LANES = 2 * SSD_GROUP_HEADS
SSD_UNROLL = 2


def _ssd_kernel(*refs, seq_len, has_init, emit_state):
    refs = list(refs)
    (x_ref, bc_ref, z_ref, sm_ref, cwx_ref, cbx_ref, cwbc_ref, cbbc_ref, gp_ref, dexp_ref,
     sel_ref, exp_ref) = refs[:12]
    refs = refs[12:]
    if has_init:
        h0f_ref, h0b_ref = refs[:2]
        refs = refs[2:]
    y_ref = refs.pop(0)
    if emit_state:
        hf_ref, hb_ref = refs[:2]
        refs = refs[2:]
    xpad, bcpad, xc, bcc, dt3, da3, yacc, hst = refs

    q = SSD_CHUNK
    n_chunks = seq_len // q
    n_gl = SSD_GATE_LANES
    lane = lax.broadcasted_iota(jnp.int32, (q, LANES), 1)
    gate_lane = lane < n_gl

    def sum_pieces(m):
        return m + pltpu.roll(m, LANES - n_gl, 1) + pltpu.roll(m, LANES - 2 * n_gl, 1)

    def pack_pieces(v):
        v = jnp.where(gate_lane, v, 0.0)
        hi = v.astype(BF16).astype(F32)
        rest = v - hi
        mid = rest.astype(BF16).astype(F32)
        lo = rest - mid
        return (hi + pltpu.roll(mid, n_gl, 1) + pltpu.roll(lo, 2 * n_gl, 1)).astype(BF16)

    def store_x(rows, val):
        xc[rows, :] = val
        yacc[rows, :] = val * dexp_ref[...]

    _conv_silu(x_ref, cwx_ref, cbx_ref, xpad, store_x, seq_len, seq_len)

    def store_bc(rows, val):
        bcc[rows, :] = val

    _conv_silu(bc_ref, cwbc_ref, cbbc_ref, bcpad, store_bc, seq_len, seq_len)

    gp = gp_ref[...]
    bias_row = gp[0:1, :]
    aneg_sel = _dot(-jnp.exp(gp), sel_ref[0, 0].astype(F32), HI)[1:2, :]

    def gate_body(i, carry):
        for sub in range(SSD_UNROLL):
            rows = pl.ds(pl.multiple_of((i * SSD_UNROLL + sub) * q, q), q)
            dt_full = _softplus(sm_ref[rows, :].astype(F32) + bias_row)
            hi = dt_full.astype(BF16)
            rest = dt_full - hi.astype(F32)
            mid = rest.astype(BF16)
            lo = (rest - mid.astype(F32)).astype(BF16)
            dt_sel = sum_pieces(_dot(hi, sel_ref[0, 0]) + _dot(mid, sel_ref[0, 1])
                                + _dot(lo, sel_ref[0, 2]))
            dt3[rows, :] = pack_pieces(dt_sel)
            da3[rows, :] = pack_pieces(dt_sel * aneg_sel)
        return carry

    lax.fori_loop(0, n_chunks // SSD_UNROLL, gate_body, 0)

    if has_init:
        hst[0] = h0f_ref[0]
        hst[1] = h0b_ref[0]
    else:
        hst[...] = jnp.zeros(hst.shape, F32)

    ri = lax.broadcasted_iota(jnp.int32, (q, q), 0)
    ci = lax.broadcasted_iota(jnp.int32, (q, q), 1)
    low_half = lane < SSD_HEAD_DIM
    masks = [ri >= ci, ri <= ci]
    tris = [jnp.where(m, 1.0, 0.0).astype(BF16) for m in masks]

    def chunk(d, c):
        rows = pl.ds(pl.multiple_of(c * q, q), q)
        mask = masks[d]
        xck = xc[rows, :]
        bcck = bcc[rows, :]
        b_c = bcck[:, :SSD_STATE].astype(BF16)
        c_c = bcck[:, SSD_STATE:].astype(BF16)
        cum = sum_pieces(_dot(tris[d], da3[rows, :]))
        cum_t = cum.T
        dt_x = _dot(dt3[rows, :], exp_ref[d])
        ac_x = _dot(pack_pieces(cum), exp_ref[d])
        ae_x = ac_x[q - 1:q, :] if d == 0 else ac_x[0:1, :]
        xdt = xck * dt_x
        cb = _dot_nt(c_c, b_c)
        lane0 = d * SSD_GROUP_HEADS
        parts = []
        for hp in range(SSD_GROUP_HEADS // 2):
            ms = []
            for j in (2 * hp, 2 * hp + 1):
                col = cum[:, lane0 + j:lane0 + j + 1]
                row = cum_t[lane0 + j:lane0 + j + 1, :]
                dec = jnp.exp(jnp.where(mask, col - row, NEG_BIG))
                ms.append((cb * dec).astype(BF16))
            x2 = xdt[:, hp * LANES:(hp + 1) * LANES]
            w_lo = jnp.where(low_half, x2, 0.0)
            w_hi = jnp.where(low_half, 0.0, x2)
            parts.append(_dot(jnp.concatenate(ms, axis=1),
                              jnp.concatenate([w_lo, w_hi], axis=0).astype(BF16)))
        y_diag = jnp.concatenate(parts, axis=1)
        hs = hst[d]
        y_off = _dot(c_c, hs.astype(BF16)) * jnp.exp(ac_x)
        yacc[rows, :] += y_diag + y_off
        hst[d] = hs * jnp.exp(ae_x) + _dot_tn(b_c, (xdt * jnp.exp(ae_x - ac_x)).astype(BF16))

    def step(s, carry):
        for sub in range(SSD_UNROLL):
            chunk(0, SSD_UNROLL * s + sub)
            chunk(1, n_chunks - 1 - SSD_UNROLL * s - sub)
        return carry

    lax.fori_loop(0, n_chunks // SSD_UNROLL, step, 0)

    def out_body(i, carry):
        rows = pl.ds(pl.multiple_of(i * q, q), q)
        y_ref[rows, :] = yacc[rows, :] * _silu(z_ref[rows, :].astype(F32))
        return carry

    lax.fori_loop(0, n_chunks, out_body, 0)
    if emit_state:
        hf_ref[0] = hst[0]
        hb_ref[0] = hst[1]


def _ssd(proj, consts, n_seq, seq_len, row_off, init, emit_state):
    rb = row_off // seq_len
    cw = SSD_GROUP_COLS
    has_init = init is not None

    def col(block_cols, base):
        return base // block_cols

    in_specs = [
        pl.BlockSpec((seq_len, cw), lambda b, g: (rb + b, col(cw, COL_X) + g)),
        pl.BlockSpec((seq_len, LANES), lambda b, g: (rb + b, col(LANES, COL_BC) + g)),
        pl.BlockSpec((seq_len, cw), lambda b, g: (rb + b, col(cw, COL_Z_S) + g)),
        pl.BlockSpec((seq_len, LANES), lambda b, g: (rb + b, 0)),
        pl.BlockSpec((SUBLANES, cw), lambda b, g: (0, g)),
        pl.BlockSpec((1, cw), lambda b, g: (0, g)),
        pl.BlockSpec((SUBLANES, LANES), lambda b, g: (0, g)),
        pl.BlockSpec((1, LANES), lambda b, g: (0, g)),
        pl.BlockSpec((SUBLANES, LANES), lambda b, g: (0, 0)),
        pl.BlockSpec((1, cw), lambda b, g: (0, g)),
        pl.BlockSpec((1, 3, LANES, LANES), lambda b, g: (g, 0, 0, 0)),
        pl.BlockSpec((2, LANES, cw), lambda b, g: (0, 0, 0)),
    ]
    proj_main, small = proj
    args = [proj_main, proj_main, proj_main, small, consts["cw_x"], consts["cb_x"], consts["cw_bc"], consts["cb_bc"],
            consts["gate_params"], consts["d_exp"], consts["ssd_sel"], consts["ssd_exp"]]
    if has_init:
        in_specs += [pl.BlockSpec((1, SSD_STATE, cw), lambda b, g: (b, 0, g))] * 2
        args += list(init)
    out_specs = [pl.BlockSpec((seq_len, cw), lambda b, g: (b, g))]
    out_shape = [jax.ShapeDtypeStruct((n_seq * seq_len, SSD_WIDTH), F32)]
    if emit_state:
        out_specs += [pl.BlockSpec((1, SSD_STATE, cw), lambda b, g: (b, 0, g))] * 2
        out_shape += [jax.ShapeDtypeStruct((n_seq, SSD_STATE, SSD_WIDTH), F32)] * 2
    scratch = [
        pltpu.VMEM((seq_len + 2 * CONV_HALO, cw), F32),
        pltpu.VMEM((seq_len + 2 * CONV_HALO, LANES), F32),
        pltpu.VMEM((seq_len, cw), F32),
        pltpu.VMEM((seq_len, LANES), F32),
        pltpu.VMEM((seq_len, LANES), BF16),
        pltpu.VMEM((seq_len, LANES), BF16),
        pltpu.VMEM((seq_len, cw), F32),
        pltpu.VMEM((2, SSD_STATE, cw), F32),
    ]
    return pl.pallas_call(
        functools.partial(_ssd_kernel, seq_len=seq_len, has_init=has_init, emit_state=emit_state),
        grid=(n_seq, SSD_GROUPS),
        in_specs=in_specs,
        out_specs=out_specs,
        out_shape=out_shape,
        scratch_shapes=scratch,
        compiler_params=_cparams(("arbitrary", "arbitrary")),
        name="ssd_scan",
    )(*args)


DN_UNIT = 2 * DN_CHUNK
DN_W2_ROWS = 2 * DN_CHUNK
DN_M2_ROWS = DN_CHUNK + DN_HEAD_DIM
DN_HB = 2
DN_UG = 4
DN_INST = DN_UG * DN_HB * 2
DN_GATE_LANES = 16


def _dn_kernel(*refs, blk_len, seg_len, has_init, emit_state):
    refs = list(refs)
    (q_ref, k_ref, v_ref, z_ref, sm_ref, cwq_ref, cbq_ref, cwk_ref, cbk_ref, cwv_ref, cbv_ref,
     gp_ref, nw_ref, sel_ref) = refs[:14]
    refs = refs[14:]
    if has_init:
        s0_refs = refs[:2]
        refs = refs[2:]
    y_ref = refs.pop(0)
    if emit_state:
        s_out_refs = refs[:2]
        refs = refs[2:]
    (pad, qs, ks, vs, gfull, cumf, cumb, oacc, sst, kk_s, qk_s, a_s, p_s, t_s, rhs_s,
     w2, m2, us, cds, vp_s, op_s) = refs

    cq = DN_CHUNK
    hd = DN_HEAD_DIM
    unit = DN_UNIT
    n_units = blk_len // unit
    seg_units = seg_len // unit
    run_units = min(seg_units, DN_UG)
    chains_per_dir = DN_UG // run_units
    scale = DN_HEAD_DIM ** -0.5

    def l2n(v):
        return v * lax.rsqrt(jnp.sum(v * v, axis=-1, keepdims=True) + EPS)

    def per_head(fn, val):
        return jnp.concatenate([fn(val[:, j * hd:(j + 1) * hd]) for j in range(DN_HB)], axis=1)

    def store_q(rows, val):
        qs[rows, :] = per_head(lambda v: l2n(v) * scale, val)

    def store_k(rows, val):
        ks[rows, :] = per_head(l2n, val)

    def store_v(rows, val):
        vs[rows, :] = val
        oacc[rows, :] = jnp.zeros_like(val)

    _conv_silu(q_ref, cwq_ref, cbq_ref, pad, store_q, blk_len, seg_len)
    _conv_silu(k_ref, cwk_ref, cbk_ref, pad, store_k, blk_len, seg_len)
    _conv_silu(v_ref, cwv_ref, cbv_ref, pad, store_v, blk_len, seg_len)

    ri = lax.broadcasted_iota(jnp.int32, (unit, unit), 0)
    ci = lax.broadcasted_iota(jnp.int32, (unit, unit), 1)
    same = (ri // cq) == (ci // cq)
    incl = [jnp.logical_and(same, ri >= ci), jnp.logical_and(same, ri <= ci)]
    strict = [jnp.logical_and(same, ri > ci), jnp.logical_and(same, ri < ci)]
    tri = [jnp.where(m, 1.0, 0.0).astype(BF16) for m in incl]
    col_in_chunk = [ci < cq, ci >= cq]
    eye = jnp.where(ri == ci, 1.0, 0.0)

    def level_mask(sz):
        return jnp.logical_and((ri // (2 * sz)) == (ci // (2 * sz)), (ri // sz) != (ci // sz))

    gp = gp_ref[...]
    bias_row = gp[0:1, :]
    aneg_row = -jnp.exp(gp[1:2, :])
    lane = lax.broadcasted_iota(jnp.int32, (CONV_ROWS, LANES), 1)
    is_decay_lane = lane < 2 * SSD_HEADS + 2 * DN_HEADS

    def sum_parts(m):
        return m + pltpu.roll(m, LANES - 4, 1) + pltpu.roll(m, LANES - 8, 1)

    def gate_body(i, carry):
        for sub in range(DN_UG):
            rows = pl.ds(pl.multiple_of((i * DN_UG + sub) * unit, unit), unit)
            sm = sm_ref[rows, :].astype(F32)
            comb = jnp.where(is_decay_lane, aneg_row * _softplus(sm + bias_row), jax.nn.sigmoid(sm))
            hi = comb.astype(BF16)
            rest = comb - hi.astype(F32)
            mid = rest.astype(BF16)
            lo = (rest - mid.astype(F32)).astype(BF16)
            g3 = _dot(hi, sel_ref[0, 0]) + _dot(mid, sel_ref[0, 1]) + _dot(lo, sel_ref[0, 2])
            g3b = g3.astype(BF16)
            gfull[rows, :] = sum_parts(g3)
            cumf[rows, :] = sum_parts(_dot(tri[0], g3b))
            cumb[rows, :] = sum_parts(_dot(tri[1], g3b))
        return carry

    lax.fori_loop(0, n_units // DN_UG, gate_body, 0)

    def chain_id(j, d, s):
        return (j * 2 + d) * chains_per_dir + s

    def init_state(j, d):
        return s0_refs[d][0, j] if has_init else jnp.zeros((hd, hd), F32)

    for j in range(DN_HB):
        for d in range(2):
            for s in range(chains_per_dir):
                sst[chain_id(j, d, s)] = init_state(j, d)

    def inst(ul, j, d):
        return (ul * DN_HB + j) * 2 + d

    def unit_of(k, ul, d):
        u = k * DN_UG + ul
        return u if d == 0 else n_units - 1 - u

    zero_half = jnp.zeros((cq, hd), BF16)

    def solve_batch(k):
        for ul in range(DN_UG):
            for d in range(2):
                rows = pl.ds(pl.multiple_of(unit_of(k, ul, d) * unit, unit), unit)
                for j in range(DN_HB):
                    hs = slice(j * hd, (j + 1) * hd)
                    knb = ks[rows, hs].astype(BF16)
                    i = inst(ul, j, d)
                    kk_s[i] = _dot_nt(knb, knb)
                    qk_s[i] = _dot_nt(qs[rows, hs].astype(BF16), knb)
        for ul in range(DN_UG):
            for d in range(2):
                rows = pl.ds(pl.multiple_of(unit_of(k, ul, d) * unit, unit), unit)
                gf = gfull[rows, :]
                cum = (cumf if d == 0 else cumb)[rows, :]
                cum_t = cum.T
                tot = cumf[rows, :] + cumb[rows, :] - gf
                for j in range(DN_HB):
                    hs = slice(j * hd, (j + 1) * hd)
                    lane0 = DN_GATE_LANES * j
                    i = inst(ul, j, d)
                    qn = qs[rows, hs]
                    kn = ks[rows, hs]
                    gcol = cum[:, lane0 + d:lane0 + d + 1]
                    grow = cum_t[lane0 + d:lane0 + d + 1, :]
                    bcol = gf[:, lane0 + 2 + d:lane0 + 3 + d]
                    gtot = tot[:, lane0 + d:lane0 + d + 1]
                    dec = jnp.exp(jnp.where(incl[d], gcol - grow, NEG_BIG))
                    a_mat = jnp.where(strict[d], kk_s[i] * dec * bcol, 0.0)
                    a_s[i] = a_mat.astype(BF16)
                    t_s[i] = eye - jnp.where(level_mask(1), a_mat, 0.0)
                    eg = jnp.exp(gcol)
                    rhs_s[i] = jnp.concatenate([vs[rows, hs] * bcol, kn * (bcol * eg)], axis=1)
                    qk = (qk_s[i] * dec).astype(BF16)
                    qd = (qn * eg).astype(BF16)
                    kd_t = (kn * jnp.exp(gtot - gcol)).T
                    cd = jnp.exp(gtot)
                    for c in (0, 1):
                        rs = slice(c * cq, (c + 1) * cq)
                        w2[i, c * DN_W2_ROWS + cq:(c + 1) * DN_W2_ROWS, :] = qd[rs]
                        m2[i, c * DN_M2_ROWS:c * DN_M2_ROWS + cq, :] = qk[rs]
                        m2[i, c * DN_M2_ROWS + cq:(c + 1) * DN_M2_ROWS, :] = jnp.where(
                            col_in_chunk[c], kd_t, 0.0).astype(BF16)
                        cds[i, c * SUBLANES:(c + 1) * SUBLANES, :] = jnp.broadcast_to(
                            cd[c * cq:c * cq + SUBLANES], (SUBLANES, hd))
        sz = 2
        while sz < cq:
            mask = level_mask(sz)
            for i in range(DN_INST):
                p_s[i] = _dot(t_s[i].astype(BF16), jnp.where(mask, a_s[i], 0.0)).astype(BF16)
            for i in range(DN_INST):
                t = t_s[i]
                t_s[i] = t - _dot(p_s[i], t.astype(BF16))
            sz *= 2
        for i in range(DN_INST):
            rhs = rhs_s[i]
            sol = rhs + _dot((t_s[i] - eye).astype(BF16), rhs.astype(BF16))
            us[i] = sol[:, :hd]
            w = sol[:, hd:].astype(BF16)
            for c in (0, 1):
                w2[i, c * DN_W2_ROWS:c * DN_W2_ROWS + cq, :] = w[c * cq:(c + 1) * cq]

    def recur_batch(k):
        for t in range(2 * run_units):
            plan = []
            for j in range(DN_HB):
                for d in range(2):
                    for s in range(chains_per_dir):
                        ul = s * run_units + t // 2
                        c = t % 2 if d == 0 else 1 - t % 2
                        plan.append((chain_id(j, d, s), inst(ul, j, d), j, d, c,
                                     unit_of(k, ul, d) * unit + c * cq))
            for ch, i, j, d, c, pos in plan:
                first = (pos % seg_len == 0) if d == 0 else ((pos + cq) % seg_len == 0)
                s_val = jnp.where(first, init_state(j, d), sst[ch])
                sst[ch] = s_val
                tt = _dot(w2[i, c * DN_W2_ROWS:(c + 1) * DN_W2_ROWS, :], s_val.astype(BF16))
                v_new = (us[i, c * cq:(c + 1) * cq, :] - tt[0:cq]).astype(BF16)
                vp_s[ch] = jnp.concatenate([v_new, zero_half] if c == 0 else [zero_half, v_new], axis=0)
                op_s[ch] = tt[cq:2 * cq]
            for ch, i, j, d, c, pos in plan:
                t2 = _dot(m2[i, c * DN_M2_ROWS:(c + 1) * DN_M2_ROWS, :], vp_s[ch])
                orows = pl.ds(pl.multiple_of(pos, cq), cq)
                oacc[orows, j * hd:(j + 1) * hd] += op_s[ch] + t2[0:cq]
                s_new = sst[ch] * cds[i, c * SUBLANES:c * SUBLANES + 1, :] + t2[cq:cq + hd]
                sst[ch] = s_new
                if emit_state and t == 2 * run_units - 1:
                    s_out_refs[d][pos // seg_len, j] = s_new

    def batch(k, carry):
        solve_batch(k)
        recur_batch(k)
        return carry

    lax.fori_loop(0, n_units // DN_UG, batch, 0)

    def out_body(i, carry):
        rows = pl.ds(pl.multiple_of(i * CONV_ROWS, CONV_ROWS), CONV_ROWS)
        for j in range(DN_HB):
            hs = slice(j * hd, (j + 1) * hd)
            o = oacc[rows, hs]
            y_ref[rows, hs] = o * _rms_scale(o) * nw_ref[...] * _silu(z_ref[rows, hs].astype(F32))
        return carry

    lax.fori_loop(0, blk_len // CONV_ROWS, out_body, 0)


def _dn(proj, consts, n_blk, blk_len, seg_len, row_off, init, emit_state):
    rb = row_off // blk_len
    hd = DN_HEAD_DIM
    has_init = init is not None
    n_seg = blk_len // seg_len
    assert CONV_ROWS == DN_UNIT and (blk_len // DN_UNIT) % DN_UG == 0
    seg_units = seg_len // DN_UNIT
    assert DN_UG % seg_units == 0 or seg_units % DN_UG == 0
    n_chains = DN_HB * 2 * (DN_UG // min(seg_units, DN_UG))
    bw = DN_HB * hd
    n_hp = DN_HEADS // DN_HB
    cq0 = COL_QKV // bw
    in_specs = [
        pl.BlockSpec((blk_len, bw), lambda b, h: (rb + b, cq0 + h)),
        pl.BlockSpec((blk_len, bw), lambda b, h: (rb + b, cq0 + n_hp + h)),
        pl.BlockSpec((blk_len, bw), lambda b, h: (rb + b, cq0 + 2 * n_hp + h)),
        pl.BlockSpec((blk_len, bw), lambda b, h: (rb + b, COL_Z_D // bw + h)),
        pl.BlockSpec((blk_len, LANES), lambda b, h: (rb + b, 0)),
        pl.BlockSpec((SUBLANES, bw), lambda b, h: (0, h)),
        pl.BlockSpec((1, bw), lambda b, h: (0, h)),
        pl.BlockSpec((SUBLANES, bw), lambda b, h: (0, n_hp + h)),
        pl.BlockSpec((1, bw), lambda b, h: (0, n_hp + h)),
        pl.BlockSpec((SUBLANES, bw), lambda b, h: (0, 2 * n_hp + h)),
        pl.BlockSpec((1, bw), lambda b, h: (0, 2 * n_hp + h)),
        pl.BlockSpec((SUBLANES, LANES), lambda b, h: (0, 0)),
        pl.BlockSpec((1, hd), lambda b, h: (0, 0)),
        pl.BlockSpec((1, 3, LANES, LANES), lambda b, h: (h, 0, 0, 0)),
    ]
    proj_main, small = proj
    args = [proj_main, proj_main, proj_main, proj_main, small,
            consts["cw_dn"], consts["cb_dn"], consts["cw_dn"], consts["cb_dn"], consts["cw_dn"],
            consts["cb_dn"], consts["gate_params"], consts["dn_norm_w"], consts["dn_sel"]]
    if has_init:
        assert n_seg == 1
        in_specs += [pl.BlockSpec((1, DN_HB, hd, hd), lambda b, h: (b, h, 0, 0))] * 2
        args += list(init)
    out_specs = [pl.BlockSpec((blk_len, bw), lambda b, h: (b, h))]
    out_shape = [jax.ShapeDtypeStruct((n_blk * blk_len, DN_WIDTH), F32)]
    if emit_state:
        out_specs += [pl.BlockSpec((n_seg, DN_HB, hd, hd), lambda b, h: (b, h, 0, 0))] * 2
        out_shape += [jax.ShapeDtypeStruct((n_blk * n_seg, DN_HEADS, hd, hd), F32)] * 2
    scratch = [
        pltpu.VMEM((_conv_pad_rows(blk_len, seg_len), bw), F32),
        pltpu.VMEM((blk_len, bw), F32),
        pltpu.VMEM((blk_len, bw), F32),
        pltpu.VMEM((blk_len, bw), F32),
        pltpu.VMEM((blk_len, LANES), F32),
        pltpu.VMEM((blk_len, LANES), F32),
        pltpu.VMEM((blk_len, LANES), F32),
        pltpu.VMEM((blk_len, bw), F32),
        pltpu.VMEM((n_chains, hd, hd), F32),
        pltpu.VMEM((DN_INST, DN_UNIT, DN_UNIT), F32),
        pltpu.VMEM((DN_INST, DN_UNIT, DN_UNIT), F32),
        pltpu.VMEM((DN_INST, DN_UNIT, DN_UNIT), BF16),
        pltpu.VMEM((DN_INST, DN_UNIT, DN_UNIT), BF16),
        pltpu.VMEM((DN_INST, DN_UNIT, DN_UNIT), F32),
        pltpu.VMEM((DN_INST, DN_UNIT, 2 * hd), F32),
        pltpu.VMEM((DN_INST, 2 * DN_W2_ROWS, hd), BF16),
        pltpu.VMEM((DN_INST, 2 * DN_M2_ROWS, hd), BF16),
        pltpu.VMEM((DN_INST, DN_UNIT, hd), F32),
        pltpu.VMEM((DN_INST, 2 * SUBLANES, hd), F32),
        pltpu.VMEM((n_chains, DN_UNIT, hd), BF16),
        pltpu.VMEM((n_chains, DN_CHUNK, hd), F32),
    ]
    return pl.pallas_call(
        functools.partial(_dn_kernel, blk_len=blk_len, seg_len=seg_len, has_init=has_init,
                          emit_state=emit_state),
        grid=(n_blk, n_hp),
        in_specs=in_specs,
        out_specs=out_specs,
        out_shape=out_shape,
        scratch_shapes=scratch,
        compiler_params=_cparams(("arbitrary", "arbitrary")),
        name="dn_scan",
    )(*args)


ROUTE_ID_LANE = 0
ROUTE_GATE_LANE = 2


def _outproj_kernel(ysc_ref, ysl_ref, ydc_ref, ydl_ref, x0_ref, mod_ref, snw_ref, n2w_ref, wo_ref,
                    wr_ref, br_ref, x1_ref, h2_ref, route_ref, *, ctx_tiles):
    is_ctx = pl.program_id(0) < ctx_tiles
    ys = jnp.where(is_ctx, ysc_ref[...], ysl_ref[...])
    ysn = (ys * _rms_scale(ys) * snw_ref[...]).astype(BF16)
    ydn = jnp.where(is_ctx, ydc_ref[...], ydl_ref[...]).astype(BF16)
    m = _dot(ysn, wo_ref[0:SSD_WIDTH, :]) + _dot(ydn, wo_ref[SSD_WIDTH:, :])
    mod = mod_ref[0]
    x1 = x0_ref[...] + mod[2:3] * m
    x1_ref[...] = x1
    h2 = (x1 * _rms_scale(x1) * n2w_ref[...]) * (1.0 + mod[4:5]) + mod[3:4]
    h2_ref[...] = _pack_rows(h2)
    logits = _dot(h2.astype(BF16), wr_ref[...]) + br_ref[...]
    ln = lax.broadcasted_iota(jnp.int32, logits.shape, 1)
    is_g = ln < N_EXPERT_GROUPS
    gl = jnp.where(is_g, logits, NEG_BIG)
    gmax = jnp.max(gl, axis=-1, keepdims=True)
    gidx = jnp.min(jnp.where(gl == gmax, ln, LANES), axis=-1, keepdims=True)
    gw = 1.0 / jnp.sum(jnp.where(is_g, jnp.exp(gl - gmax), 0.0), axis=-1, keepdims=True)
    lo = N_EXPERT_GROUPS + gidx * EXPERTS_PER_GROUP
    in_grp = (ln >= lo) & (ln < lo + EXPERTS_PER_GROUP)
    el = jnp.where(in_grp, logits, NEG_BIG)
    m1 = jnp.max(el, axis=-1, keepdims=True)
    i1 = jnp.min(jnp.where(el == m1, ln, LANES), axis=-1, keepdims=True)
    el2 = jnp.where(ln == i1, NEG_BIG, el)
    m2 = jnp.max(el2, axis=-1, keepdims=True)
    i2 = jnp.min(jnp.where(el2 == m2, ln, LANES), axis=-1, keepdims=True)
    t = jnp.exp(m2 - m1)
    g1 = gw / (1.0 + t)
    g2 = gw * t / (1.0 + t)
    id1 = (i1 - N_EXPERT_GROUPS).astype(F32)
    id2 = (i2 - N_EXPERT_GROUPS).astype(F32)
    route = jnp.where(ln == 0, id1, jnp.where(ln == 1, id2, jnp.where(ln == 2, g1,
                                                                      jnp.where(ln == 3, g2, 0.0))))
    route_ref[...] = route


def _outproj(yssd_ctx, yssd_lat, ydn_ctx, ydn_lat, x0, mod, consts, lat_seq):
    n_total = x0.shape[0]
    tm = INPROJ_TILE
    ctx_tiles = yssd_ctx.shape[0] // tm
    ctx_row, lat_row, mod_row = _stream_maps(ctx_tiles, lat_seq // tm)
    row = lambda i: (i, 0)
    const2 = lambda i: (0, 0)
    return pl.pallas_call(
        functools.partial(_outproj_kernel, ctx_tiles=ctx_tiles),
        grid=(n_total // tm,),
        in_specs=[pl.BlockSpec((tm, SSD_WIDTH), ctx_row),
                  pl.BlockSpec((tm, SSD_WIDTH), lat_row),
                  pl.BlockSpec((tm, DN_WIDTH), ctx_row),
                  pl.BlockSpec((tm, DN_WIDTH), lat_row),
                  pl.BlockSpec((tm, D_MODEL), row),
                  pl.BlockSpec((1, N_MOD, D_MODEL), mod_row),
                  pl.BlockSpec((1, SSD_WIDTH), const2),
                  pl.BlockSpec((1, D_MODEL), const2),
                  pl.BlockSpec((SSD_WIDTH + DN_WIDTH, D_MODEL), const2),
                  pl.BlockSpec((D_MODEL, LANES), const2),
                  pl.BlockSpec((1, LANES), const2)],
        out_specs=[pl.BlockSpec((tm, D_MODEL), row),
                   pl.BlockSpec((tm, PACK_COLS), row),
                   pl.BlockSpec((tm, LANES), row)],
        out_shape=[jax.ShapeDtypeStruct((n_total, D_MODEL), F32),
                   jax.ShapeDtypeStruct((n_total, PACK_COLS), jnp.uint32),
                   jax.ShapeDtypeStruct((n_total, LANES), F32)],
        compiler_params=_cparams(("arbitrary",)),
        name="outproj_router",
    )(yssd_ctx, yssd_lat, ydn_ctx, ydn_lat, x0, mod, consts["ssd_norm_w"], consts["norm2_w"],
      consts["w_out"], consts["w_router"], consts["b_router"])


def _rank_kernel(route_ref, rank_ref, count_ref, carry):
    i = pl.program_id(0)

    @pl.when(i == 0)
    def _():
        carry[...] = jnp.zeros(carry.shape, F32)

    route = route_ref[...]
    n = route.shape[0]
    ln = lax.broadcasted_iota(jnp.int32, route.shape, 1)
    id1 = route[:, 0:1].astype(jnp.int32)
    id2 = route[:, 1:2].astype(jnp.int32)
    hit1 = ln == id1
    hit2 = ln == id2
    onehot = jnp.where(hit1, 1.0, jnp.where(hit2, 1.0, 0.0))
    ri = lax.broadcasted_iota(jnp.int32, (n, n), 0)
    ci = lax.broadcasted_iota(jnp.int32, (n, n), 1)
    before = (ri > ci).astype(BF16)
    tot = _dot(before, onehot.astype(BF16)) + carry[0:1, :]
    r1 = jnp.sum(jnp.where(hit1, tot, 0.0), axis=-1, keepdims=True)
    r2 = jnp.sum(jnp.where(hit2, tot, 0.0), axis=-1, keepdims=True)
    info = jnp.where(ln < 2, route, jnp.where(ln == 2, r1, jnp.where(ln == 3, r2, 0.0)))
    rank_ref[0] = info.T[0:SUBLANES, :]
    new = carry[...] + jnp.sum(onehot, axis=0, keepdims=True)
    carry[...] = new
    count_ref[...] = new


def _ranks(route):
    n_total = route.shape[0]
    tm = ROW_TILE
    return pl.pallas_call(
        _rank_kernel,
        grid=(n_total // tm,),
        in_specs=[pl.BlockSpec((tm, LANES), lambda i: (i, 0))],
        out_specs=[pl.BlockSpec((1, SUBLANES, tm), lambda i: (i, 0, 0)),
                   pl.BlockSpec((SUBLANES, LANES), lambda i: (0, 0))],
        out_shape=[jax.ShapeDtypeStruct((n_total // tm, SUBLANES, tm), F32),
                   jax.ShapeDtypeStruct((SUBLANES, LANES), F32)],
        scratch_shapes=[pltpu.VMEM((SUBLANES, LANES), F32)],
        compiler_params=_cparams(("arbitrary",)),
        name="slot_ranks",
    )(route)


def _scatter_kernel(pos_ref, h_ref, xs_in_ref, xs_ref, sem):
    del xs_in_ref
    n = h_ref.shape[0]

    def body(tb, carry):
        for u in range(DMA_UNROLL):
            t = tb * DMA_UNROLL + u
            for k in range(2):
                p = pos_ref[0, 0, k * n + t]
                pltpu.make_async_copy(h_ref.at[pl.ds(t, 1), :], xs_ref.at[pl.ds(p, 1), :],
                                      sem).start(priority=k)
        return carry

    lax.fori_loop(0, n // DMA_UNROLL, body, 0)
    for _ in range(2):
        pltpu.make_async_copy(h_ref, xs_ref.at[pl.ds(0, n), :], sem).wait()


def _scatter_rows(pos3, h2, n_sorted):
    n_total = h2.shape[0]
    tm = ROW_TILE
    zeros = jnp.zeros((n_sorted, PACK_COLS), jnp.uint32)
    return pl.pallas_call(
        _scatter_kernel,
        grid=(n_total // tm,),
        in_specs=[pl.BlockSpec((1, 1, 2 * tm), lambda i: (i, 0, 0), memory_space=pltpu.SMEM),
                  pl.BlockSpec((tm, PACK_COLS), lambda i: (i, 0)),
                  pl.BlockSpec(memory_space=pl.ANY)],
        out_specs=pl.BlockSpec(memory_space=pl.ANY),
        out_shape=jax.ShapeDtypeStruct((n_sorted, PACK_COLS), jnp.uint32),
        scratch_shapes=[pltpu.SemaphoreType.DMA(())],
        input_output_aliases={2: 0},
        compiler_params=_cparams(("arbitrary",)),
        name="moe_scatter",
    )(pos3, h2, zeros)


def _ffn_kernel(te_ref, tv_ref, x_ref, wg_ref, wu_ref, wd_ref, y_ref, wgb, wub, wdb):
    i = pl.program_id(0)
    prev = te_ref[jnp.maximum(i - 1, 0)]

    @pl.when(jnp.logical_or(i == 0, te_ref[i] != prev))
    def _():
        wgb[...] = wg_ref[0].astype(BF16)
        wub[...] = wu_ref[0].astype(BF16)
        wdb[...] = wd_ref[0].astype(BF16)

    @pl.when(tv_ref[i] > 0)
    def _():
        x_lo, x_hi = _unpack_rows(x_ref[...])
        xb = jnp.concatenate([x_lo.astype(BF16), x_hi.astype(BF16)], axis=1)
        a = _dot(xb, wgb[...])
        u = _dot(xb, wub[...])
        act = (_silu(a) * u).astype(BF16)
        y_ref[...] = _pack_rows(_dot(act, wdb[...]))

    @pl.when(tv_ref[i] == 0)
    def _():
        y_ref[...] = jnp.zeros(y_ref.shape, jnp.uint32)


def _grouped_ffn(tile_expert, tile_valid, xs, w_gate, w_up, w_down):
    n_sorted = xs.shape[0]
    tm = MOE_TILE
    grid_spec = pltpu.PrefetchScalarGridSpec(
        num_scalar_prefetch=2,
        grid=(n_sorted // tm,),
        in_specs=[pl.BlockSpec((tm, PACK_COLS), lambda i, te, tv: (i, 0)),
                  pl.BlockSpec((1, D_MODEL, D_FF), lambda i, te, tv: (te[i], 0, 0)),
                  pl.BlockSpec((1, D_MODEL, D_FF), lambda i, te, tv: (te[i], 0, 0)),
                  pl.BlockSpec((1, D_FF, D_MODEL), lambda i, te, tv: (te[i], 0, 0))],
        out_specs=pl.BlockSpec((tm, PACK_COLS), lambda i, te, tv: (i, 0)),
        scratch_shapes=[pltpu.VMEM((D_MODEL, D_FF), BF16),
                        pltpu.VMEM((D_MODEL, D_FF), BF16),
                        pltpu.VMEM((D_FF, D_MODEL), BF16)],
    )
    return pl.pallas_call(
        _ffn_kernel,
        grid_spec=grid_spec,
        out_shape=jax.ShapeDtypeStruct((n_sorted, PACK_COLS), jnp.uint32),
        compiler_params=_cparams(("arbitrary",)),
        name="moe_ffn",
    )(tile_expert, tile_valid, xs, w_gate, w_up, w_down)


def _combine_kernel(pos_ref, route_ref, x1_ref, mod_ref, fw_ref, ys_ref, out_ref, buf, sem):
    n = x1_ref.shape[0]

    def body(tb, carry):
        for u in range(DMA_UNROLL):
            t = tb * DMA_UNROLL + u
            for k in range(2):
                p = pos_ref[0, 0, k * n + t]
                pltpu.make_async_copy(ys_ref.at[pl.ds(p, 1), :], buf.at[k, pl.ds(t, 1), :],
                                      sem).start(priority=k)
        return carry

    lax.fori_loop(0, n // DMA_UNROLL, body, 0)
    for k in range(2):
        pltpu.make_async_copy(ys_ref.at[pl.ds(0, n), :], buf.at[k], sem).wait()
    route = route_ref[...]
    g1 = route[:, ROUTE_GATE_LANE:ROUTE_GATE_LANE + 1]
    g2 = route[:, ROUTE_GATE_LANE + 1:ROUTE_GATE_LANE + 2]
    y1_lo, y1_hi = _unpack_rows(buf[0])
    y2_lo, y2_hi = _unpack_rows(buf[1])
    moe = jnp.concatenate([g1 * y1_lo + g2 * y2_lo, g1 * y1_hi + g2 * y2_hi], axis=1)
    x2 = x1_ref[...] + mod_ref[0][5:6] * moe
    out_ref[...] = x2 * _rms_scale(x2) * fw_ref[...]


def _combine(pos3, route, x1, mod, final_w, ys, n_rows, row_off, tokens_per_mod, mod_off):
    tm = ROW_TILE
    off = row_off // tm
    per = tokens_per_mod // tm
    return pl.pallas_call(
        _combine_kernel,
        grid=(n_rows // tm,),
        in_specs=[pl.BlockSpec((1, 1, 2 * tm), lambda i: (off + i, 0, 0), memory_space=pltpu.SMEM),
                  pl.BlockSpec((tm, LANES), lambda i: (off + i, 0)),
                  pl.BlockSpec((tm, D_MODEL), lambda i: (off + i, 0)),
                  pl.BlockSpec((1, N_MOD, D_MODEL), lambda i: (mod_off + i // per, 0, 0)),
                  pl.BlockSpec((1, D_MODEL), lambda i: (0, 0)),
                  pl.BlockSpec(memory_space=pl.ANY)],
        out_specs=pl.BlockSpec((tm, D_MODEL), lambda i: (i, 0)),
        out_shape=jax.ShapeDtypeStruct((n_rows, D_MODEL), F32),
        scratch_shapes=[pltpu.VMEM((2, tm, PACK_COLS), jnp.uint32), pltpu.SemaphoreType.DMA(())],
        compiler_params=_cparams(("arbitrary",)),
        name="moe_combine",
    )(pos3, route, x1, mod, final_w, ys)


def _grid_sincos_2d(n_tokens):
    rows = n_tokens // GRID_W
    quarter = D_MODEL // 4
    omega = (1.0 / (np.float32(POS_BASE) ** (np.arange(quarter, dtype=np.float32) / np.float32(quarter)))).astype(np.float32)
    r = np.repeat(np.arange(rows, dtype=np.float32), GRID_W)
    c = np.tile(np.arange(GRID_W, dtype=np.float32), rows)
    ar = (r[:, None] * omega).astype(np.float32)
    ac = (c[:, None] * omega).astype(np.float32)
    return jnp.asarray(np.concatenate([np.sin(ar), np.cos(ar), np.sin(ac), np.cos(ac)], axis=-1).astype(np.float32))


def _pad_rows(a, n):
    return jnp.concatenate([a, jnp.zeros((n - a.shape[0],) + a.shape[1:], a.dtype)], axis=0)


def _selection_constants():
    gh = SSD_GROUP_HEADS
    ssd_sel = np.zeros((SSD_GROUPS, 3, LANES, LANES), np.float32)
    for g in range(SSD_GROUPS):
        for part in range(3):
            for d in range(2):
                for j in range(gh):
                    ssd_sel[g, part, d * SSD_HEADS + g * gh + j, SSD_GATE_LANES * part + d * gh + j] = 1.0
    ssd_exp = np.zeros((2, LANES, SSD_GROUP_COLS), np.float32)
    for d in range(2):
        for part in range(3):
            for c in range(SSD_GROUP_COLS):
                ssd_exp[d, SSD_GATE_LANES * part + d * gh + c // SSD_HEAD_DIM, c] = 1.0
    dn_sel = np.zeros((DN_HEADS // DN_HB, 3, LANES, LANES), np.float32)
    base = 2 * SSD_HEADS
    for h in range(DN_HEADS):
        dst = DN_GATE_LANES * (h % DN_HB)
        for part in range(3):
            for d in range(2):
                dn_sel[h // DN_HB, part, base + d * DN_HEADS + h, dst + 4 * part + d] = 1.0
                dn_sel[h // DN_HB, part, base + 2 * DN_HEADS + d * DN_HEADS + h, dst + 4 * part + 2 + d] = 1.0
    return ssd_sel, ssd_exp, dn_sel


def _ssd_consts(conv_w, conv_b, ssd_d):
    bcw, bcb = [], []
    for g in range(SSD_GROUPS):
        for base in (SSD_WIDTH, SSD_WIDTH + SSD_GROUPS * SSD_STATE):
            sl = slice(base + g * SSD_STATE, base + (g + 1) * SSD_STATE)
            bcw.append(conv_w[:, sl])
            bcb.append(conv_b[sl])
    ssd_sel, ssd_exp, _ = _selection_constants()
    return {
        "cw_x": _pad_rows(conv_w[:, 0:SSD_WIDTH], SUBLANES),
        "cb_x": conv_b[0:SSD_WIDTH].reshape(1, -1),
        "cw_bc": _pad_rows(jnp.concatenate(bcw, axis=1), SUBLANES),
        "cb_bc": jnp.concatenate(bcb).reshape(1, -1),
        "d_exp": jnp.repeat(ssd_d, SSD_HEAD_DIM).reshape(1, -1),
        "ssd_sel": jnp.asarray(ssd_sel, dtype=BF16),
        "ssd_exp": jnp.asarray(ssd_exp, dtype=BF16),
    }


def _dn_consts(conv_w, conv_b, ssd_dt_bias, ssd_a_log, dn_dt_bias, dn_a_log, dn_norm_w):
    n_ssd = 2 * SSD_HEADS
    n_dn = 2 * DN_HEADS
    gate_params = jnp.zeros((SUBLANES, LANES), F32)
    gate_params = gate_params.at[0, 0:n_ssd].set(ssd_dt_bias.reshape(-1))
    gate_params = gate_params.at[0, n_ssd:n_ssd + n_dn].set(dn_dt_bias.reshape(-1))
    gate_params = gate_params.at[1, 0:n_ssd].set(ssd_a_log.reshape(-1))
    gate_params = gate_params.at[1, n_ssd:n_ssd + n_dn].set(dn_a_log.reshape(-1))
    dn_sel = _selection_constants()[2]
    return {
        "cw_dn": _pad_rows(conv_w, SUBLANES),
        "cb_dn": conv_b.reshape(1, -1),
        "gate_params": gate_params,
        "dn_sel": jnp.asarray(dn_sel, dtype=BF16),
        "dn_norm_w": dn_norm_w.reshape(1, -1),
    }


def kernel(x_prompt, x_sample, state_ssd_fwd, state_ssd_bwd, state_dn_fwd, state_dn_bwd, c, c_ctx, w_ada, b_ada, norm1_w, w_in, conv_ssd_w, conv_ssd_b, conv_dn_w, conv_dn_b, ssd_dt_bias, ssd_a_log, ssd_d, ssd_norm_w, dn_dt_bias, dn_a_log, dn_norm_w, w_out, norm2_w, w_router_group, b_router_group, w_router_expert, b_router_expert, w_gate, w_up, w_down, final_norm_w):
    n_b, seq, _ = x_prompt.shape
    dec_b, dec_seq, _ = x_sample.shape
    n_ctx = n_b * seq
    n_lat = dec_b * dec_seq
    n_total = n_ctx + n_lat
    layer = 0

    wi = w_in[layer]
    xbc0 = SSD_WIDTH
    bm0 = xbc0 + SSD_WIDTH
    cm0 = bm0 + SSD_GROUPS * SSD_STATE
    dt0 = cm0 + SSD_GROUPS * SSD_STATE
    qkv0 = dt0 + 2 * SSD_HEADS
    zd0 = qkv0 + 3 * DN_WIDTH
    ad0 = zd0 + DN_WIDTH
    bd0 = ad0 + 2 * DN_HEADS
    bc_cols = []
    for g in range(SSD_GROUPS):
        bc_cols += [wi[:, bm0 + g * SSD_STATE:bm0 + (g + 1) * SSD_STATE],
                    wi[:, cm0 + g * SSD_STATE:cm0 + (g + 1) * SSD_STATE]]
    w_r = jnp.concatenate(
        [wi[:, 0:SSD_WIDTH], wi[:, xbc0:bm0]] + bc_cols +
        [wi[:, qkv0:zd0], wi[:, zd0:ad0], wi[:, dt0:qkv0], wi[:, ad0:bd0], wi[:, bd0:bd0 + 2 * DN_HEADS],
         jnp.zeros((D_MODEL, PROJ_COLS - COL_SMALL - 2 * SSD_HEADS - 4 * DN_HEADS), F32)],
        axis=1).astype(BF16)

    w_router = jnp.concatenate(
        [w_router_group[layer], w_router_expert[layer],
         jnp.zeros((D_MODEL, LANES - N_EXPERT_GROUPS - N_EXPERTS), F32)], axis=1)
    b_router = jnp.concatenate(
        [b_router_group[layer], b_router_expert[layer],
         jnp.zeros((LANES - N_EXPERT_GROUPS - N_EXPERTS,), F32)]).reshape(1, LANES)
    consts = {
        "ssd_norm_w": ssd_norm_w[layer].reshape(1, -1),
        "norm2_w": norm2_w[layer].reshape(1, -1),
        "w_out": w_out[layer].astype(BF16),
        "w_router": w_router.astype(BF16),
        "b_router": b_router,
    }
    consts.update(_ssd_consts(conv_ssd_w[layer], conv_ssd_b[layer], ssd_d[layer]))
    consts.update(_dn_consts(conv_dn_w[layer], conv_dn_b[layer], ssd_dt_bias[layer], ssd_a_log[layer],
                             dn_dt_bias[layer], dn_a_log[layer], dn_norm_w[layer]))

    cond = _pad_rows(jnp.concatenate([c_ctx[None, :], c], axis=0), SUBLANES)
    mod = _adaln(cond, w_ada[layer], b_ada[layer]).reshape(SUBLANES, N_MOD, D_MODEL)

    pos = _grid_sincos_2d(dec_seq)
    n1w = norm1_w[layer].reshape(1, -1)
    proj_main, small, x0 = _inproj(x_prompt.reshape(n_ctx, D_MODEL), x_sample.reshape(n_lat, D_MODEL), pos,
                                   mod, n1w, w_r, dec_seq)
    proj = (proj_main, small)

    def ssd_state_in(s):
        return s[:, layer].transpose(0, 3, 1, 2).reshape(dec_b, SSD_STATE, SSD_WIDTH)

    yssd_ctx, hf, hb = _ssd(proj, consts, n_b, seq, 0, None, True)
    yssd_lat = _ssd(proj, consts, dec_b, dec_seq, n_ctx,
                    (ssd_state_in(state_ssd_fwd), ssd_state_in(state_ssd_bwd)), False)[0]
    ydn_ctx, sf, sb = _dn(proj, consts, n_ctx // dec_seq, dec_seq, seq, 0, None, True)
    ydn_lat = _dn(proj, consts, dec_b, dec_seq, dec_seq, n_ctx,
                  (state_dn_fwd[:, layer], state_dn_bwd[:, layer]), False)[0]

    x1, h2, route = _outproj(yssd_ctx, yssd_lat, ydn_ctx, ydn_lat, x0, mod, consts, dec_seq)

    slots, counts = _ranks(route)
    counts = counts[0, :N_EXPERTS].astype(jnp.int32)
    padded = ((counts + MOE_TILE - 1) // MOE_TILE) * MOE_TILE
    ends = jnp.cumsum(padded)
    offsets = ends - padded
    n_sorted = 2 * n_total + N_EXPERTS * MOE_TILE
    ids = slots[:, 0:2, :].astype(jnp.int32)
    expert_ids = jnp.arange(N_EXPERTS, dtype=jnp.int32)
    slot_base = jnp.sum(jnp.where(ids[..., None] == expert_ids, offsets, 0), axis=-1)
    pos_slots = slot_base + slots[:, 2:4, :].astype(jnp.int32)
    pos3 = pos_slots.reshape(n_total // ROW_TILE, 1, 2 * ROW_TILE)
    tile_start = jnp.arange(n_sorted // MOE_TILE, dtype=jnp.int32) * MOE_TILE
    tile_expert = jnp.minimum(jnp.sum(tile_start[:, None] >= ends[None, :], axis=1), N_EXPERTS - 1).astype(jnp.int32)
    tile_valid = (tile_start < ends[-1]).astype(jnp.int32)

    xs = _scatter_rows(pos3, h2, n_sorted)
    ys = _grouped_ffn(tile_expert, tile_valid, xs, w_gate[layer], w_up[layer], w_down[layer])
    fw = final_norm_w.reshape(1, -1)
    y_prompt = _combine(pos3, route, x1, mod, fw, ys, n_ctx, 0, n_ctx, 0)
    y_sample = _combine(pos3, route, x1, mod, fw, ys, n_lat, n_ctx, dec_seq, 1)

    def ssd_state_out(s):
        return s.reshape(n_b, SSD_STATE, SSD_HEADS, SSD_HEAD_DIM).transpose(0, 2, 3, 1)[:, None]

    return (y_prompt.reshape(n_b, seq, D_MODEL), y_sample.reshape(dec_b, dec_seq, D_MODEL),
            ssd_state_out(hf), ssd_state_out(hb), sf[:, None], sb[:, None])
```

```python
import functools

import numpy as np
import jax
import jax.numpy as jnp
from jax import lax
from jax.experimental import pallas as pl
from jax.experimental.pallas import tpu as pltpu

F32 = jnp.float32
BF16 = jnp.bfloat16
HI = lax.Precision.HIGHEST

D_MODEL = 1024
EPS = 1e-6
GRID_W = 64
POS_BASE = 10000.0
CONV_W = 5
N_MOD = 6
SSD_HEADS = 16
SSD_HEAD_DIM = 64
SSD_WIDTH = SSD_HEADS * SSD_HEAD_DIM
SSD_GROUPS = 2
SSD_STATE = 64
SSD_CHUNK = 128
DN_HEADS = 8
DN_HEAD_DIM = 128
DN_WIDTH = DN_HEADS * DN_HEAD_DIM
DN_CHUNK = 64
N_EXPERT_GROUPS = 4
EXPERTS_PER_GROUP = 8
N_EXPERTS = N_EXPERT_GROUPS * EXPERTS_PER_GROUP
D_FF = 512

LANES = 128
SUBLANES = 8
VMEM_LIMIT = 56 * 1024 * 1024

COL_Z_S = 0
COL_X = 1024
COL_BC = 2048
COL_QKV = 2304
COL_Z_D = 5376
COL_SMALL = 6400
PROJ_COLS = 6528

ROW_TILE = 256
MOE_TILE = 512
DMA_UNROLL = 8
NEG_BIG = -1e30


def _dot(a, b, prec=None):
    return jnp.dot(a, b, preferred_element_type=F32, precision=prec)


def _dot_nt(a, b, prec=None):
    return lax.dot_general(a, b, (((1,), (1,)), ((), ())), preferred_element_type=F32, precision=prec)


def _dot_tn(a, b, prec=None):
    return lax.dot_general(a, b, (((0,), (0,)), ((), ())), preferred_element_type=F32, precision=prec)


def _silu(x):
    return x * jax.nn.sigmoid(x)


def _softplus(x):
    return jnp.maximum(x, 0.0) + jnp.log1p(jnp.exp(-jnp.abs(x)))


def _rms_scale(x):
    return lax.rsqrt(jnp.mean(x * x, axis=-1, keepdims=True) + EPS)


PACK_COLS = D_MODEL // 2
HIGH_HALF = 0xFFFF0000


def _pack_rows(x):
    lo = pltpu.bitcast(x[:, :PACK_COLS].astype(BF16).astype(F32), jnp.uint32)
    hi = pltpu.bitcast(x[:, PACK_COLS:].astype(BF16).astype(F32), jnp.uint32)
    return (lo >> 16) | (hi & jnp.uint32(HIGH_HALF))


def _unpack_rows(u):
    return pltpu.bitcast(u << 16, F32), pltpu.bitcast(u & jnp.uint32(HIGH_HALF), F32)


def _cparams(sem, vmem=VMEM_LIMIT):
    return pltpu.CompilerParams(dimension_semantics=sem, vmem_limit_bytes=vmem)


def _ada_kernel(c_ref, w_ref, b_ref, o_ref):
    c = c_ref[...]
    o_ref[...] = _dot(_silu(c), w_ref[...], HI) + b_ref[...]


def _adaln(cond, w_ada, b_ada):
    n_out = N_MOD * D_MODEL
    tn = 1536
    return pl.pallas_call(
        _ada_kernel,
        grid=(n_out // tn,),
        in_specs=[pl.BlockSpec((SUBLANES, D_MODEL), lambda j: (0, 0)),
                  pl.BlockSpec((D_MODEL, tn), lambda j: (0, j)),
                  pl.BlockSpec((1, tn), lambda j: (0, j))],
        out_specs=pl.BlockSpec((SUBLANES, tn), lambda j: (0, j)),
        out_shape=jax.ShapeDtypeStruct((SUBLANES, n_out), F32),
        compiler_params=_cparams(("arbitrary",)),
        name="adaln",
    )(cond, w_ada, b_ada.reshape(1, n_out))


INPROJ_TILE = 512
INPROJ_CHUNK = 1280


def _inproj_kernel(xc_ref, xl_ref, pos_ref, mod_ref, nw_ref, w_ref, proj_ref, small_ref, x0_ref, *,
                   ctx_tiles):
    is_ctx = pl.program_id(0) < ctx_tiles
    x = jnp.where(is_ctx, xc_ref[...], xl_ref[...] + pos_ref[...])
    x0_ref[...] = x
    mod = mod_ref[0]
    hb = ((x * _rms_scale(x) * nw_ref[...]) * (1.0 + mod[1:2]) + mod[0:1]).astype(BF16)
    for j in range(0, COL_SMALL, INPROJ_CHUNK):
        proj_ref[:, j:j + INPROJ_CHUNK] = _dot(hb, w_ref[:, j:j + INPROJ_CHUNK]).astype(BF16)
    small_ref[...] = _dot(hb, w_ref[:, COL_SMALL:])


def _stream_maps(ctx_tiles, lat_tiles_per_mod):
    def ctx_row(i):
        return (jnp.minimum(i, ctx_tiles - 1), 0)

    def lat_row(i):
        return (jnp.maximum(i - ctx_tiles, 0), 0)

    def mod_row(i):
        return (jnp.where(i < ctx_tiles, 0, 1 + (i - ctx_tiles) // lat_tiles_per_mod), 0, 0)

    return ctx_row, lat_row, mod_row


def _inproj(x_ctx, x_lat, pos, mod, norm_w, w_r, lat_seq):
    tm = INPROJ_TILE
    ctx_tiles = x_ctx.shape[0] // tm
    n_total = x_ctx.shape[0] + x_lat.shape[0]
    n_pos = pos.shape[0] // tm
    ctx_row, lat_row, mod_row = _stream_maps(ctx_tiles, lat_seq // tm)
    return pl.pallas_call(
        functools.partial(_inproj_kernel, ctx_tiles=ctx_tiles),
        grid=(n_total // tm,),
        in_specs=[pl.BlockSpec((tm, D_MODEL), ctx_row),
                  pl.BlockSpec((tm, D_MODEL), lat_row),
                  pl.BlockSpec((tm, D_MODEL), lambda i: (jnp.maximum(i - ctx_tiles, 0) % n_pos, 0)),
                  pl.BlockSpec((1, N_MOD, D_MODEL), mod_row),
                  pl.BlockSpec((1, D_MODEL), lambda i: (0, 0)),
                  pl.BlockSpec((D_MODEL, PROJ_COLS), lambda i: (0, 0), pipeline_mode=pl.Buffered(1))],
        out_specs=[pl.BlockSpec((tm, COL_SMALL), lambda i: (i, 0)),
                   pl.BlockSpec((tm, LANES), lambda i: (i, 0)),
                   pl.BlockSpec((tm, D_MODEL), lambda i: (i, 0))],
        out_shape=[jax.ShapeDtypeStruct((n_total, COL_SMALL), BF16),
                   jax.ShapeDtypeStruct((n_total, LANES), F32),
                   jax.ShapeDtypeStruct((n_total, D_MODEL), F32)],
        compiler_params=_cparams(("arbitrary",)),
        name="inproj",
    )(x_ctx, x_lat, pos, mod, norm_w, w_r)


CONV_ROWS = 128
CONV_HALO = SUBLANES
CONV_UNROLL = 2
CONV_UNROLL_MAX_COLS = 256


def _conv_silu(src_ref, w_ref, b_ref, pad_ref, store, blk_len, seg_len):
    n_ch = src_ref.shape[1]
    n_seg = blk_len // seg_len
    chunks_per_seg = seg_len // CONV_ROWS
    zeros = jnp.zeros((CONV_HALO, n_ch), F32)
    for s in range(n_seg + 1):
        gap = s * (seg_len + CONV_HALO)
        pad_ref[gap:gap + CONV_HALO, :] = zeros

    def window_start(i):
        return pl.multiple_of(i * CONV_ROWS + (i // chunks_per_seg) * CONV_HALO, CONV_HALO)

    def copy_body(i, carry):
        r0 = pl.multiple_of(i * CONV_ROWS, CONV_ROWS)
        pad_ref[pl.ds(window_start(i) + CONV_HALO, CONV_ROWS), :] = src_ref[pl.ds(r0, CONV_ROWS), :].astype(F32)
        return carry

    lax.fori_loop(0, blk_len // CONV_ROWS, copy_body, 0)
    win = CONV_ROWS + 2 * CONV_HALO
    w = w_ref[...]
    b = b_ref[...]

    unroll = CONV_UNROLL if n_ch <= CONV_UNROLL_MAX_COLS else 1

    def body(i2, carry):
        for sub in range(unroll):
            i = i2 * unroll + sub
            v = pad_ref[pl.ds(window_start(i), win), :]
            acc = jnp.zeros((CONV_ROWS, n_ch), F32) + b
            for k in range(CONV_W):
                shift = (CONV_W // 2 - k) % win
                sh = v if shift == 0 else pltpu.roll(v, shift, 0)
                acc = acc + w[k:k + 1, :] * sh[CONV_HALO:CONV_HALO + CONV_ROWS, :]
            store(pl.ds(pl.multiple_of(i * CONV_ROWS, CONV_ROWS), CONV_ROWS), _silu(acc))
        return carry

    lax.fori_loop(0, blk_len // (CONV_ROWS * unroll), body, 0)


def _conv_pad_rows(blk_len, seg_len):
    return blk_len + (blk_len // seg_len + 1) * CONV_HALO


SSD_GROUP_HEADS = SSD_HEADS // SSD_GROUPS
SSD_GROUP_COLS = SSD_GROUP_HEADS * SSD_HEAD_DIM
SSD_GATE_LANES = 2 * SSD_GROUP_HEADS
SSD_UNROLL = 2


def _ssd_kernel(*refs, seq_len, has_init, emit_state):
    refs = list(refs)
    (x_ref, bc_ref, z_ref, sm_ref, cwx_ref, cbx_ref, cwbc_ref, cbbc_ref, gp_ref, dexp_ref,
     sel_ref, exp_ref) = refs[:12]
    refs = refs[12:]
    if has_init:
        h0f_ref, h0b_ref = refs[:2]
        refs = refs[2:]
    y_ref = refs.pop(0)
    if emit_state:
        hf_ref, hb_ref = refs[:2]
        refs = refs[2:]
    xpad, bcpad, xc, bcc, dt3, da3, yacc, hst = refs

    q = SSD_CHUNK
    n_chunks = seq_len // q
    n_gl = SSD_GATE_LANES
    lane = lax.broadcasted_iota(jnp.int32, (q, LANES), 1)
    gate_lane = lane < n_gl

    def sum_pieces(m):
        return m + pltpu.roll(m, LANES - n_gl, 1) + pltpu.roll(m, LANES - 2 * n_gl, 1)

    def pack_pieces(v):
        v = jnp.where(gate_lane, v, 0.0)
        hi = v.astype(BF16).astype(F32)
        rest = v - hi
        mid = rest.astype(BF16).astype(F32)
        lo = rest - mid
        return (hi + pltpu.roll(mid, n_gl, 1) + pltpu.roll(lo, 2 * n_gl, 1)).astype(BF16)

    def store_x(rows, val):
        xc[rows, :] = val
        yacc[rows, :] = val * dexp_ref[...]

    _conv_silu(x_ref, cwx_ref, cbx_ref, xpad, store_x, seq_len, seq_len)

    def store_bc(rows, val):
        bcc[rows, :] = val

    _conv_silu(bc_ref, cwbc_ref, cbbc_ref, bcpad, store_bc, seq_len, seq_len)

    gp = gp_ref[...]
    bias_row = gp[0:1, :]
    aneg_sel = _dot(-jnp.exp(gp), sel_ref[0, 0].astype(F32), HI)[1:2, :]

    def gate_body(i, carry):
        for sub in range(SSD_UNROLL):
            rows = pl.ds(pl.multiple_of((i * SSD_UNROLL + sub) * q, q), q)
            dt_full = _softplus(sm_ref[rows, :].astype(F32) + bias_row)
            hi = dt_full.astype(BF16)
            rest = dt_full - hi.astype(F32)
            mid = rest.astype(BF16)
            lo = (rest - mid.astype(F32)).astype(BF16)
            dt_sel = sum_pieces(_dot(hi, sel_ref[0, 0]) + _dot(mid, sel_ref[0, 1])
                                + _dot(lo, sel_ref[0, 2]))
            dt3[rows, :] = pack_pieces(dt_sel)
            da3[rows, :] = pack_pieces(dt_sel * aneg_sel)
        return carry

    lax.fori_loop(0, n_chunks // SSD_UNROLL, gate_body, 0)

    if has_init:
        hst[0] = h0f_ref[0]
        hst[1] = h0b_ref[0]
    else:
        hst[...] = jnp.zeros(hst.shape, F32)

    ri = lax.broadcasted_iota(jnp.int32, (q, q), 0)
    ci = lax.broadcasted_iota(jnp.int32, (q, q), 1)
    low_half = lane < SSD_HEAD_DIM
    masks = [ri >= ci, ri <= ci]
    tris = [jnp.where(m, 1.0, 0.0).astype(BF16) for m in masks]

    def chunk(d, c):
        rows = pl.ds(pl.multiple_of(c * q, q), q)
        mask = masks[d]
        xck = xc[rows, :]
        bcck = bcc[rows, :]
        b_c = bcck[:, :SSD_STATE].astype(BF16)
        c_c = bcck[:, SSD_STATE:].astype(BF16)
        cum = sum_pieces(_dot(tris[d], da3[rows, :]))
        cum_t = cum.T
        dt_x = _dot(dt3[rows, :], exp_ref[d])
        ac_x = _dot(pack_pieces(cum), exp_ref[d])
        ae_x = ac_x[q - 1:q, :] if d == 0 else ac_x[0:1, :]
        xdt = xck * dt_x
        cb = _dot_nt(c_c, b_c)
        lane0 = d * SSD_GROUP_HEADS
        parts = []
        for hp in range(SSD_GROUP_HEADS // 2):
            ms = []
            for j in (2 * hp, 2 * hp + 1):
                col = cum[:, lane0 + j:lane0 + j + 1]
                row = cum_t[lane0 + j:lane0 + j + 1, :]
                dec = jnp.exp(jnp.where(mask, col - row, NEG_BIG))
                ms.append((cb * dec).astype(BF16))
            x2 = xdt[:, hp * LANES:(hp + 1) * LANES]
            w_lo = jnp.where(low_half, x2, 0.0)
            w_hi = jnp.where(low_half, 0.0, x2)
            parts.append(_dot(jnp.concatenate(ms, axis=1),
                              jnp.concatenate([w_lo, w_hi], axis=0).astype(BF16)))
        y_diag = jnp.concatenate(parts, axis=1)
        hs = hst[d]
        y_off = _dot(c_c, hs.astype(BF16)) * jnp.exp(ac_x)
        yacc[rows, :] += y_diag + y_off
        hst[d] = hs * jnp.exp(ae_x) + _dot_tn(b_c, (xdt * jnp.exp(ae_x - ac_x)).astype(BF16))

    def step(s, carry):
        for sub in range(SSD_UNROLL):
            chunk(0, SSD_UNROLL * s + sub)
            chunk(1, n_chunks - 1 - SSD_UNROLL * s - sub)
        return carry

    lax.fori_loop(0, n_chunks // SSD_UNROLL, step, 0)

    def out_body(i, carry):
        rows = pl.ds(pl.multiple_of(i * q, q), q)
        y_ref[rows, :] = yacc[rows, :] * _silu(z_ref[rows, :].astype(F32))
        return carry

    lax.fori_loop(0, n_chunks, out_body, 0)
    if emit_state:
        hf_ref[0] = hst[0]
        hb_ref[0] = hst[1]


def _ssd(proj, consts, n_seq, seq_len, row_off, init, emit_state):
    rb = row_off // seq_len
    cw = SSD_GROUP_COLS
    has_init = init is not None

    def col(block_cols, base):
        return base // block_cols

    in_specs = [
        pl.BlockSpec((seq_len, cw), lambda b, g: (rb + b, col(cw, COL_X) + g)),
        pl.BlockSpec((seq_len, LANES), lambda b, g: (rb + b, col(LANES, COL_BC) + g)),
        pl.BlockSpec((seq_len, cw), lambda b, g: (rb + b, col(cw, COL_Z_S) + g)),
        pl.BlockSpec((seq_len, LANES), lambda b, g: (rb + b, 0)),
        pl.BlockSpec((SUBLANES, cw), lambda b, g: (0, g)),
        pl.BlockSpec((1, cw), lambda b, g: (0, g)),
        pl.BlockSpec((SUBLANES, LANES), lambda b, g: (0, g)),
        pl.BlockSpec((1, LANES), lambda b, g: (0, g)),
        pl.BlockSpec((SUBLANES, LANES), lambda b, g: (0, 0)),
        pl.BlockSpec((1, cw), lambda b, g: (0, g)),
        pl.BlockSpec((1, 3, LANES, LANES), lambda b, g: (g, 0, 0, 0)),
        pl.BlockSpec((2, LANES, cw), lambda b, g: (0, 0, 0)),
    ]
    proj_main, small = proj
    args = [proj_main, proj_main, proj_main, small, consts["cw_x"], consts["cb_x"], consts["cw_bc"], consts["cb_bc"],
            consts["gate_params"], consts["d_exp"], consts["ssd_sel"], consts["ssd_exp"]]
    if has_init:
        in_specs += [pl.BlockSpec((1, SSD_STATE, cw), lambda b, g: (b, 0, g))] * 2
        args += list(init)
    out_specs = [pl.BlockSpec((seq_len, cw), lambda b, g: (b, g))]
    out_shape = [jax.ShapeDtypeStruct((n_seq * seq_len, SSD_WIDTH), F32)]
    if emit_state:
        out_specs += [pl.BlockSpec((1, SSD_STATE, cw), lambda b, g: (b, 0, g))] * 2
        out_shape += [jax.ShapeDtypeStruct((n_seq, SSD_STATE, SSD_WIDTH), F32)] * 2
    scratch = [
        pltpu.VMEM((seq_len + 2 * CONV_HALO, cw), F32),
        pltpu.VMEM((seq_len + 2 * CONV_HALO, LANES), F32),
        pltpu.VMEM((seq_len, cw), F32),
        pltpu.VMEM((seq_len, LANES), F32),
        pltpu.VMEM((seq_len, LANES), BF16),
        pltpu.VMEM((seq_len, LANES), BF16),
        pltpu.VMEM((seq_len, cw), F32),
        pltpu.VMEM((2, SSD_STATE, cw), F32),
    ]
    return pl.pallas_call(
        functools.partial(_ssd_kernel, seq_len=seq_len, has_init=has_init, emit_state=emit_state),
        grid=(n_seq, SSD_GROUPS),
        in_specs=in_specs,
        out_specs=out_specs,
        out_shape=out_shape,
        scratch_shapes=scratch,
        compiler_params=_cparams(("arbitrary", "arbitrary")),
        name="ssd_scan",
    )(*args)


DN_UNIT = 2 * DN_CHUNK
DN_W2_ROWS = 2 * DN_CHUNK
DN_M2_ROWS = DN_CHUNK + DN_HEAD_DIM
DN_HB = 2
DN_UG = 4
DN_INST = DN_UG * DN_HB * 2
DN_GATE_LANES = 16


def _dn_kernel(*refs, blk_len, seg_len, has_init, emit_state):
    refs = list(refs)
    (q_ref, k_ref, v_ref, z_ref, sm_ref, cwq_ref, cbq_ref, cwk_ref, cbk_ref, cwv_ref, cbv_ref,
     gp_ref, nw_ref, sel_ref) = refs[:14]
    refs = refs[14:]
    if has_init:
        s0_refs = refs[:2]
        refs = refs[2:]
    y_ref = refs.pop(0)
    if emit_state:
        s_out_refs = refs[:2]
        refs = refs[2:]
    (pad, qs, ks, vs, gfull, cumf, cumb, oacc, sst, kk_s, qk_s, a_s, p_s, t_s, rhs_s,
     w2, m2, us, cds, vp_s, op_s) = refs

    cq = DN_CHUNK
    hd = DN_HEAD_DIM
    unit = DN_UNIT
    n_units = blk_len // unit
    seg_units = seg_len // unit
    run_units = min(seg_units, DN_UG)
    chains_per_dir = DN_UG // run_units
    scale = DN_HEAD_DIM ** -0.5

    def l2n(v):
        return v * lax.rsqrt(jnp.sum(v * v, axis=-1, keepdims=True) + EPS)

    def per_head(fn, val):
        return jnp.concatenate([fn(val[:, j * hd:(j + 1) * hd]) for j in range(DN_HB)], axis=1)

    def store_q(rows, val):
        qs[rows, :] = per_head(lambda v: l2n(v) * scale, val)

    def store_k(rows, val):
        ks[rows, :] = per_head(l2n, val)

    def store_v(rows, val):
        vs[rows, :] = val
        oacc[rows, :] = jnp.zeros_like(val)

    _conv_silu(q_ref, cwq_ref, cbq_ref, pad, store_q, blk_len, seg_len)
    _conv_silu(k_ref, cwk_ref, cbk_ref, pad, store_k, blk_len, seg_len)
    _conv_silu(v_ref, cwv_ref, cbv_ref, pad, store_v, blk_len, seg_len)

    ri = lax.broadcasted_iota(jnp.int32, (unit, unit), 0)
    ci = lax.broadcasted_iota(jnp.int32, (unit, unit), 1)
    same = (ri // cq) == (ci // cq)
    incl = [jnp.logical_and(same, ri >= ci), jnp.logical_and(same, ri <= ci)]
    strict = [jnp.logical_and(same, ri > ci), jnp.logical_and(same, ri < ci)]
    tri = [jnp.where(m, 1.0, 0.0).astype(BF16) for m in incl]
    col_in_chunk = [ci < cq, ci >= cq]
    eye = jnp.where(ri == ci, 1.0, 0.0)

    def level_mask(sz):
        return jnp.logical_and((ri // (2 * sz)) == (ci // (2 * sz)), (ri // sz) != (ci // sz))

    gp = gp_ref[...]
    bias_row = gp[0:1, :]
    aneg_row = -jnp.exp(gp[1:2, :])
    lane = lax.broadcasted_iota(jnp.int32, (CONV_ROWS, LANES), 1)
    is_decay_lane = lane < 2 * SSD_HEADS + 2 * DN_HEADS

    def sum_parts(m):
        return m + pltpu.roll(m, LANES - 4, 1) + pltpu.roll(m, LANES - 8, 1)

    def gate_body(i, carry):
        for sub in range(DN_UG):
            rows = pl.ds(pl.multiple_of((i * DN_UG + sub) * unit, unit), unit)
            sm = sm_ref[rows, :].astype(F32)
            comb = jnp.where(is_decay_lane, aneg_row * _softplus(sm + bias_row), jax.nn.sigmoid(sm))
            hi = comb.astype(BF16)
            rest = comb - hi.astype(F32)
            mid = rest.astype(BF16)
            lo = (rest - mid.astype(F32)).astype(BF16)
            g3 = _dot(hi, sel_ref[0, 0]) + _dot(mid, sel_ref[0, 1]) + _dot(lo, sel_ref[0, 2])
            g3b = g3.astype(BF16)
            gfull[rows, :] = sum_parts(g3)
            cumf[rows, :] = sum_parts(_dot(tri[0], g3b))
            cumb[rows, :] = sum_parts(_dot(tri[1], g3b))
        return carry

    lax.fori_loop(0, n_units // DN_UG, gate_body, 0)

    def chain_id(j, d, s):
        return (j * 2 + d) * chains_per_dir + s

    def init_state(j, d):
        return s0_refs[d][0, j] if has_init else jnp.zeros((hd, hd), F32)

    for j in range(DN_HB):
        for d in range(2):
            for s in range(chains_per_dir):
                sst[chain_id(j, d, s)] = init_state(j, d)

    def inst(ul, j, d):
        return (ul * DN_HB + j) * 2 + d

    def unit_of(k, ul, d):
        u = k * DN_UG + ul
        return u if d == 0 else n_units - 1 - u

    zero_half = jnp.zeros((cq, hd), BF16)

    def solve_batch(k):
        for ul in range(DN_UG):
            for d in range(2):
                rows = pl.ds(pl.multiple_of(unit_of(k, ul, d) * unit, unit), unit)
                for j in range(DN_HB):
                    hs = slice(j * hd, (j + 1) * hd)
                    knb = ks[rows, hs].astype(BF16)
                    i = inst(ul, j, d)
                    kk_s[i] = _dot_nt(knb, knb)
                    qk_s[i] = _dot_nt(qs[rows, hs].astype(BF16), knb)
        for ul in range(DN_UG):
            for d in range(2):
                rows = pl.ds(pl.multiple_of(unit_of(k, ul, d) * unit, unit), unit)
                gf = gfull[rows, :]
                cum = (cumf if d == 0 else cumb)[rows, :]
                cum_t = cum.T
                tot = cumf[rows, :] + cumb[rows, :] - gf
                for j in range(DN_HB):
                    hs = slice(j * hd, (j + 1) * hd)
                    lane0 = DN_GATE_LANES * j
                    i = inst(ul, j, d)
                    qn = qs[rows, hs]
                    kn = ks[rows, hs]
                    gcol = cum[:, lane0 + d:lane0 + d + 1]
                    grow = cum_t[lane0 + d:lane0 + d + 1, :]
                    bcol = gf[:, lane0 + 2 + d:lane0 + 3 + d]
                    gtot = tot[:, lane0 + d:lane0 + d + 1]
                    dec = jnp.exp(jnp.where(incl[d], gcol - grow, NEG_BIG))
                    a_mat = jnp.where(strict[d], kk_s[i] * dec * bcol, 0.0)
                    a_s[i] = a_mat.astype(BF16)
                    t_s[i] = eye - jnp.where(level_mask(1), a_mat, 0.0)
                    eg = jnp.exp(gcol)
                    rhs_s[i] = jnp.concatenate([vs[rows, hs] * bcol, kn * (bcol * eg)], axis=1)
                    qk = (qk_s[i] * dec).astype(BF16)
                    qd = (qn * eg).astype(BF16)
                    kd_t = (kn * jnp.exp(gtot - gcol)).T
                    cd = jnp.exp(gtot)
                    for c in (0, 1):
                        rs = slice(c * cq, (c + 1) * cq)
                        w2[i, c * DN_W2_ROWS + cq:(c + 1) * DN_W2_ROWS, :] = qd[rs]
                        m2[i, c * DN_M2_ROWS:c * DN_M2_ROWS + cq, :] = qk[rs]
                        m2[i, c * DN_M2_ROWS + cq:(c + 1) * DN_M2_ROWS, :] = jnp.where(
                            col_in_chunk[c], kd_t, 0.0).astype(BF16)
                        cds[i, c * SUBLANES:(c + 1) * SUBLANES, :] = jnp.broadcast_to(
                            cd[c * cq:c * cq + SUBLANES], (SUBLANES, hd))
        sz = 2
        while sz < cq:
            mask = level_mask(sz)
            for i in range(DN_INST):
                p_s[i] = _dot(t_s[i].astype(BF16), jnp.where(mask, a_s[i], 0.0)).astype(BF16)
            for i in range(DN_INST):
                t = t_s[i]
                t_s[i] = t - _dot(p_s[i], t.astype(BF16))
            sz *= 2
        for i in range(DN_INST):
            rhs = rhs_s[i]
            sol = rhs + _dot((t_s[i] - eye).astype(BF16), rhs.astype(BF16))
            us[i] = sol[:, :hd]
            w = sol[:, hd:].astype(BF16)
            for c in (0, 1):
                w2[i, c * DN_W2_ROWS:c * DN_W2_ROWS + cq, :] = w[c * cq:(c + 1) * cq]

    def recur_batch(k):
        for t in range(2 * run_units):
            plan = []
            for j in range(DN_HB):
                for d in range(2):
                    for s in range(chains_per_dir):
                        ul = s * run_units + t // 2
                        c = t % 2 if d == 0 else 1 - t % 2
                        plan.append((chain_id(j, d, s), inst(ul, j, d), j, d, c,
                                     unit_of(k, ul, d) * unit + c * cq))
            for ch, i, j, d, c, pos in plan:
                first = (pos % seg_len == 0) if d == 0 else ((pos + cq) % seg_len == 0)
                s_val = jnp.where(first, init_state(j, d), sst[ch])
                sst[ch] = s_val
                tt = _dot(w2[i, c * DN_W2_ROWS:(c + 1) * DN_W2_ROWS, :], s_val.astype(BF16))
                v_new = (us[i, c * cq:(c + 1) * cq, :] - tt[0:cq]).astype(BF16)
                vp_s[ch] = jnp.concatenate([v_new, zero_half] if c == 0 else [zero_half, v_new], axis=0)
                op_s[ch] = tt[cq:2 * cq]
            for ch, i, j, d, c, pos in plan:
                t2 = _dot(m2[i, c * DN_M2_ROWS:(c + 1) * DN_M2_ROWS, :], vp_s[ch])
                orows = pl.ds(pl.multiple_of(pos, cq), cq)
                oacc[orows, j * hd:(j + 1) * hd] += op_s[ch] + t2[0:cq]
                s_new = sst[ch] * cds[i, c * SUBLANES:c * SUBLANES + 1, :] + t2[cq:cq + hd]
                sst[ch] = s_new
                if emit_state and t == 2 * run_units - 1:
                    s_out_refs[d][pos // seg_len, j] = s_new

    def batch(k, carry):
        solve_batch(k)
        recur_batch(k)
        return carry

    lax.fori_loop(0, n_units // DN_UG, batch, 0)

    def out_body(i, carry):
        for sub in range(CONV_UNROLL):
            rows = pl.ds(pl.multiple_of((i * CONV_UNROLL + sub) * CONV_ROWS, CONV_ROWS), CONV_ROWS)
            for j in range(DN_HB):
                hs = slice(j * hd, (j + 1) * hd)
                o = oacc[rows, hs]
                y_ref[rows, hs] = o * _rms_scale(o) * nw_ref[...] * _silu(z_ref[rows, hs].astype(F32))
        return carry

    lax.fori_loop(0, blk_len // (CONV_ROWS * CONV_UNROLL), out_body, 0)


def _dn(proj, consts, n_blk, blk_len, seg_len, row_off, init, emit_state):
    rb = row_off // blk_len
    hd = DN_HEAD_DIM
    has_init = init is not None
    n_seg = blk_len // seg_len
    assert CONV_ROWS == DN_UNIT and (blk_len // DN_UNIT) % DN_UG == 0
    seg_units = seg_len // DN_UNIT
    assert DN_UG % seg_units == 0 or seg_units % DN_UG == 0
    n_chains = DN_HB * 2 * (DN_UG // min(seg_units, DN_UG))
    bw = DN_HB * hd
    n_hp = DN_HEADS // DN_HB
    cq0 = COL_QKV // bw
    in_specs = [
        pl.BlockSpec((blk_len, bw), lambda b, h: (rb + b, cq0 + h)),
        pl.BlockSpec((blk_len, bw), lambda b, h: (rb + b, cq0 + n_hp + h)),
        pl.BlockSpec((blk_len, bw), lambda b, h: (rb + b, cq0 + 2 * n_hp + h)),
        pl.BlockSpec((blk_len, bw), lambda b, h: (rb + b, COL_Z_D // bw + h)),
        pl.BlockSpec((blk_len, LANES), lambda b, h: (rb + b, 0)),
        pl.BlockSpec((SUBLANES, bw), lambda b, h: (0, h)),
        pl.BlockSpec((1, bw), lambda b, h: (0, h)),
        pl.BlockSpec((SUBLANES, bw), lambda b, h: (0, n_hp + h)),
        pl.BlockSpec((1, bw), lambda b, h: (0, n_hp + h)),
        pl.BlockSpec((SUBLANES, bw), lambda b, h: (0, 2 * n_hp + h)),
        pl.BlockSpec((1, bw), lambda b, h: (0, 2 * n_hp + h)),
        pl.BlockSpec((SUBLANES, LANES), lambda b, h: (0, 0)),
        pl.BlockSpec((1, hd), lambda b, h: (0, 0)),
        pl.BlockSpec((1, 3, LANES, LANES), lambda b, h: (h, 0, 0, 0)),
    ]
    proj_main, small = proj
    args = [proj_main, proj_main, proj_main, proj_main, small,
            consts["cw_dn"], consts["cb_dn"], consts["cw_dn"], consts["cb_dn"], consts["cw_dn"],
            consts["cb_dn"], consts["gate_params"], consts["dn_norm_w"], consts["dn_sel"]]
    if has_init:
        assert n_seg == 1
        in_specs += [pl.BlockSpec((1, DN_HB, hd, hd), lambda b, h: (b, h, 0, 0))] * 2
        args += list(init)
    out_specs = [pl.BlockSpec((blk_len, bw), lambda b, h: (b, h))]
    out_shape = [jax.ShapeDtypeStruct((n_blk * blk_len, DN_WIDTH), F32)]
    if emit_state:
        out_specs += [pl.BlockSpec((n_seg, DN_HB, hd, hd), lambda b, h: (b, h, 0, 0))] * 2
        out_shape += [jax.ShapeDtypeStruct((n_blk * n_seg, DN_HEADS, hd, hd), F32)] * 2
    scratch = [
        pltpu.VMEM((_conv_pad_rows(blk_len, seg_len), bw), F32),
        pltpu.VMEM((blk_len, bw), F32),
        pltpu.VMEM((blk_len, bw), F32),
        pltpu.VMEM((blk_len, bw), F32),
        pltpu.VMEM((blk_len, LANES), F32),
        pltpu.VMEM((blk_len, LANES), F32),
        pltpu.VMEM((blk_len, LANES), F32),
        pltpu.VMEM((blk_len, bw), F32),
        pltpu.VMEM((n_chains, hd, hd), F32),
        pltpu.VMEM((DN_INST, DN_UNIT, DN_UNIT), F32),
        pltpu.VMEM((DN_INST, DN_UNIT, DN_UNIT), F32),
        pltpu.VMEM((DN_INST, DN_UNIT, DN_UNIT), BF16),
        pltpu.VMEM((DN_INST, DN_UNIT, DN_UNIT), BF16),
        pltpu.VMEM((DN_INST, DN_UNIT, DN_UNIT), F32),
        pltpu.VMEM((DN_INST, DN_UNIT, 2 * hd), F32),
        pltpu.VMEM((DN_INST, 2 * DN_W2_ROWS, hd), BF16),
        pltpu.VMEM((DN_INST, 2 * DN_M2_ROWS, hd), BF16),
        pltpu.VMEM((DN_INST, DN_UNIT, hd), F32),
        pltpu.VMEM((DN_INST, 2 * SUBLANES, hd), F32),
        pltpu.VMEM((n_chains, DN_UNIT, hd), BF16),
        pltpu.VMEM((n_chains, DN_CHUNK, hd), F32),
    ]
    return pl.pallas_call(
        functools.partial(_dn_kernel, blk_len=blk_len, seg_len=seg_len, has_init=has_init,
                          emit_state=emit_state),
        grid=(n_blk, n_hp),
        in_specs=in_specs,
        out_specs=out_specs,
        out_shape=out_shape,
        scratch_shapes=scratch,
        compiler_params=_cparams(("arbitrary", "arbitrary")),
        name="dn_scan",
    )(*args)


ROUTE_ID_LANE = 0
ROUTE_GATE_LANE = 2


def _outproj_kernel(ysc_ref, ysl_ref, ydc_ref, ydl_ref, x0_ref, mod_ref, snw_ref, n2w_ref, wo_ref,
                    wr_ref, br_ref, x1_ref, h2_ref, route_ref, *, ctx_tiles):
    is_ctx = pl.program_id(0) < ctx_tiles
    ys = jnp.where(is_ctx, ysc_ref[...], ysl_ref[...])
    ysn = (ys * _rms_scale(ys) * snw_ref[...]).astype(BF16)
    ydn = jnp.where(is_ctx, ydc_ref[...], ydl_ref[...]).astype(BF16)
    m = _dot(ysn, wo_ref[0:SSD_WIDTH, :]) + _dot(ydn, wo_ref[SSD_WIDTH:, :])
    mod = mod_ref[0]
    x1 = x0_ref[...] + mod[2:3] * m
    x1_ref[...] = x1
    h2 = (x1 * _rms_scale(x1) * n2w_ref[...]) * (1.0 + mod[4:5]) + mod[3:4]
    h2_ref[...] = _pack_rows(h2)
    logits = _dot(h2.astype(BF16), wr_ref[...]) + br_ref[...]
    ln = lax.broadcasted_iota(jnp.int32, logits.shape, 1)
    is_g = ln < N_EXPERT_GROUPS
    gl = jnp.where(is_g, logits, NEG_BIG)
    gmax = jnp.max(gl, axis=-1, keepdims=True)
    gidx = jnp.min(jnp.where(gl == gmax, ln, LANES), axis=-1, keepdims=True)
    gw = 1.0 / jnp.sum(jnp.where(is_g, jnp.exp(gl - gmax), 0.0), axis=-1, keepdims=True)
    lo = N_EXPERT_GROUPS + gidx * EXPERTS_PER_GROUP
    in_grp = (ln >= lo) & (ln < lo + EXPERTS_PER_GROUP)
    el = jnp.where(in_grp, logits, NEG_BIG)
    m1 = jnp.max(el, axis=-1, keepdims=True)
    i1 = jnp.min(jnp.where(el == m1, ln, LANES), axis=-1, keepdims=True)
    el2 = jnp.where(ln == i1, NEG_BIG, el)
    m2 = jnp.max(el2, axis=-1, keepdims=True)
    i2 = jnp.min(jnp.where(el2 == m2, ln, LANES), axis=-1, keepdims=True)
    t = jnp.exp(m2 - m1)
    g1 = gw / (1.0 + t)
    g2 = gw * t / (1.0 + t)
    id1 = (i1 - N_EXPERT_GROUPS).astype(F32)
    id2 = (i2 - N_EXPERT_GROUPS).astype(F32)
    route = jnp.where(ln == 0, id1, jnp.where(ln == 1, id2, jnp.where(ln == 2, g1,
                                                                      jnp.where(ln == 3, g2, 0.0))))
    route_ref[...] = route


def _outproj(yssd_ctx, yssd_lat, ydn_ctx, ydn_lat, x0, mod, consts, lat_seq):
    n_total = x0.shape[0]
    tm = INPROJ_TILE
    ctx_tiles = yssd_ctx.shape[0] // tm
    ctx_row, lat_row, mod_row = _stream_maps(ctx_tiles, lat_seq // tm)
    row = lambda i: (i, 0)
    const2 = lambda i: (0, 0)
    return pl.pallas_call(
        functools.partial(_outproj_kernel, ctx_tiles=ctx_tiles),
        grid=(n_total // tm,),
        in_specs=[pl.BlockSpec((tm, SSD_WIDTH), ctx_row),
                  pl.BlockSpec((tm, SSD_WIDTH), lat_row),
                  pl.BlockSpec((tm, DN_WIDTH), ctx_row),
                  pl.BlockSpec((tm, DN_WIDTH), lat_row),
                  pl.BlockSpec((tm, D_MODEL), row),
                  pl.BlockSpec((1, N_MOD, D_MODEL), mod_row),
                  pl.BlockSpec((1, SSD_WIDTH), const2),
                  pl.BlockSpec((1, D_MODEL), const2),
                  pl.BlockSpec((SSD_WIDTH + DN_WIDTH, D_MODEL), const2),
                  pl.BlockSpec((D_MODEL, LANES), const2),
                  pl.BlockSpec((1, LANES), const2)],
        out_specs=[pl.BlockSpec((tm, D_MODEL), row),
                   pl.BlockSpec((tm, PACK_COLS), row),
                   pl.BlockSpec((tm, LANES), row)],
        out_shape=[jax.ShapeDtypeStruct((n_total, D_MODEL), F32),
                   jax.ShapeDtypeStruct((n_total, PACK_COLS), jnp.uint32),
                   jax.ShapeDtypeStruct((n_total, LANES), F32)],
        compiler_params=_cparams(("arbitrary",)),
        name="outproj_router",
    )(yssd_ctx, yssd_lat, ydn_ctx, ydn_lat, x0, mod, consts["ssd_norm_w"], consts["norm2_w"],
      consts["w_out"], consts["w_router"], consts["b_router"])


def _rank_kernel(route_ref, rank_ref, count_ref, carry):
    i = pl.program_id(0)

    @pl.when(i == 0)
    def _():
        carry[...] = jnp.zeros(carry.shape, F32)

    route = route_ref[...]
    n = route.shape[0]
    ln = lax.broadcasted_iota(jnp.int32, route.shape, 1)
    id1 = route[:, 0:1].astype(jnp.int32)
    id2 = route[:, 1:2].astype(jnp.int32)
    hit1 = ln == id1
    hit2 = ln == id2
    onehot = jnp.where(hit1, 1.0, jnp.where(hit2, 1.0, 0.0))
    ri = lax.broadcasted_iota(jnp.int32, (n, n), 0)
    ci = lax.broadcasted_iota(jnp.int32, (n, n), 1)
    before = (ri > ci).astype(BF16)
    tot = _dot(before, onehot.astype(BF16)) + carry[0:1, :]
    r1 = jnp.sum(jnp.where(hit1, tot, 0.0), axis=-1, keepdims=True)
    r2 = jnp.sum(jnp.where(hit2, tot, 0.0), axis=-1, keepdims=True)
    info = jnp.where(ln < 2, route, jnp.where(ln == 2, r1, jnp.where(ln == 3, r2, 0.0)))
    rank_ref[0] = info.T[0:SUBLANES, :]
    new = carry[...] + jnp.sum(onehot, axis=0, keepdims=True)
    carry[...] = new
    count_ref[...] = new


def _ranks(route):
    n_total = route.shape[0]
    tm = ROW_TILE
    return pl.pallas_call(
        _rank_kernel,
        grid=(n_total // tm,),
        in_specs=[pl.BlockSpec((tm, LANES), lambda i: (i, 0))],
        out_specs=[pl.BlockSpec((1, SUBLANES, tm), lambda i: (i, 0, 0)),
                   pl.BlockSpec((SUBLANES, LANES), lambda i: (0, 0))],
        out_shape=[jax.ShapeDtypeStruct((n_total // tm, SUBLANES, tm), F32),
                   jax.ShapeDtypeStruct((SUBLANES, LANES), F32)],
        scratch_shapes=[pltpu.VMEM((SUBLANES, LANES), F32)],
        compiler_params=_cparams(("arbitrary",)),
        name="slot_ranks",
    )(route)


def _scatter_kernel(pos_ref, h_ref, xs_in_ref, xs_ref, sem):
    del xs_in_ref
    n = h_ref.shape[0]

    def body(tb, carry):
        for u in range(DMA_UNROLL):
            t = tb * DMA_UNROLL + u
            for k in range(2):
                p = pos_ref[0, 0, k * n + t]
                pltpu.make_async_copy(h_ref.at[pl.ds(t, 1), :], xs_ref.at[pl.ds(p, 1), :],
                                      sem).start(priority=k)
        return carry

    lax.fori_loop(0, n // DMA_UNROLL, body, 0)
    for _ in range(2):
        pltpu.make_async_copy(h_ref, xs_ref.at[pl.ds(0, n), :], sem).wait()


def _scatter_rows(pos3, h2, n_sorted):
    n_total = h2.shape[0]
    tm = ROW_TILE
    zeros = jnp.zeros((n_sorted, PACK_COLS), jnp.uint32)
    return pl.pallas_call(
        _scatter_kernel,
        grid=(n_total // tm,),
        in_specs=[pl.BlockSpec((1, 1, 2 * tm), lambda i: (i, 0, 0), memory_space=pltpu.SMEM),
                  pl.BlockSpec((tm, PACK_COLS), lambda i: (i, 0)),
                  pl.BlockSpec(memory_space=pl.ANY)],
        out_specs=pl.BlockSpec(memory_space=pl.ANY),
        out_shape=jax.ShapeDtypeStruct((n_sorted, PACK_COLS), jnp.uint32),
        scratch_shapes=[pltpu.SemaphoreType.DMA(())],
        input_output_aliases={2: 0},
        compiler_params=_cparams(("arbitrary",)),
        name="moe_scatter",
    )(pos3, h2, zeros)


def _ffn_kernel(te_ref, tv_ref, x_ref, wg_ref, wu_ref, wd_ref, y_ref, wgb, wub, wdb):
    i = pl.program_id(0)
    prev = te_ref[jnp.maximum(i - 1, 0)]

    @pl.when(jnp.logical_or(i == 0, te_ref[i] != prev))
    def _():
        wgb[...] = wg_ref[0].astype(BF16)
        wub[...] = wu_ref[0].astype(BF16)
        wdb[...] = wd_ref[0].astype(BF16)

    @pl.when(tv_ref[i] > 0)
    def _():
        x_lo, x_hi = _unpack_rows(x_ref[...])
        xb = jnp.concatenate([x_lo.astype(BF16), x_hi.astype(BF16)], axis=1)
        a = _dot(xb, wgb[...])
        u = _dot(xb, wub[...])
        act = (_silu(a) * u).astype(BF16)
        y_ref[...] = _pack_rows(_dot(act, wdb[...]))

    @pl.when(tv_ref[i] == 0)
    def _():
        y_ref[...] = jnp.zeros(y_ref.shape, jnp.uint32)


def _grouped_ffn(tile_expert, tile_valid, xs, w_gate, w_up, w_down):
    n_sorted = xs.shape[0]
    tm = MOE_TILE
    grid_spec = pltpu.PrefetchScalarGridSpec(
        num_scalar_prefetch=2,
        grid=(n_sorted // tm,),
        in_specs=[pl.BlockSpec((tm, PACK_COLS), lambda i, te, tv: (i, 0)),
                  pl.BlockSpec((1, D_MODEL, D_FF), lambda i, te, tv: (te[i], 0, 0)),
                  pl.BlockSpec((1, D_MODEL, D_FF), lambda i, te, tv: (te[i], 0, 0)),
                  pl.BlockSpec((1, D_FF, D_MODEL), lambda i, te, tv: (te[i], 0, 0))],
        out_specs=pl.BlockSpec((tm, PACK_COLS), lambda i, te, tv: (i, 0)),
        scratch_shapes=[pltpu.VMEM((D_MODEL, D_FF), BF16),
                        pltpu.VMEM((D_MODEL, D_FF), BF16),
                        pltpu.VMEM((D_FF, D_MODEL), BF16)],
    )
    return pl.pallas_call(
        _ffn_kernel,
        grid_spec=grid_spec,
        out_shape=jax.ShapeDtypeStruct((n_sorted, PACK_COLS), jnp.uint32),
        compiler_params=_cparams(("arbitrary",)),
        name="moe_ffn",
    )(tile_expert, tile_valid, xs, w_gate, w_up, w_down)


def _combine_kernel(pos_ref, route_ref, x1_ref, mod_ref, fw_ref, ys_ref, out_ref, buf, sem):
    n = x1_ref.shape[0]

    def body(tb, carry):
        for u in range(DMA_UNROLL):
            t = tb * DMA_UNROLL + u
            for k in range(2):
                p = pos_ref[0, 0, k * n + t]
                pltpu.make_async_copy(ys_ref.at[pl.ds(p, 1), :], buf.at[k, pl.ds(t, 1), :],
                                      sem).start(priority=k)
        return carry

    lax.fori_loop(0, n // DMA_UNROLL, body, 0)
    for k in range(2):
        pltpu.make_async_copy(ys_ref.at[pl.ds(0, n), :], buf.at[k], sem).wait()
    route = route_ref[...]
    g1 = route[:, ROUTE_GATE_LANE:ROUTE_GATE_LANE + 1]
    g2 = route[:, ROUTE_GATE_LANE + 1:ROUTE_GATE_LANE + 2]
    y1_lo, y1_hi = _unpack_rows(buf[0])
    y2_lo, y2_hi = _unpack_rows(buf[1])
    moe = jnp.concatenate([g1 * y1_lo + g2 * y2_lo, g1 * y1_hi + g2 * y2_hi], axis=1)
    x2 = x1_ref[...] + mod_ref[0][5:6] * moe
    out_ref[...] = x2 * _rms_scale(x2) * fw_ref[...]


def _combine(pos3, route, x1, mod, final_w, ys, n_rows, row_off, tokens_per_mod, mod_off):
    tm = ROW_TILE
    off = row_off // tm
    per = tokens_per_mod // tm
    return pl.pallas_call(
        _combine_kernel,
        grid=(n_rows // tm,),
        in_specs=[pl.BlockSpec((1, 1, 2 * tm), lambda i: (off + i, 0, 0), memory_space=pltpu.SMEM),
                  pl.BlockSpec((tm, LANES), lambda i: (off + i, 0)),
                  pl.BlockSpec((tm, D_MODEL), lambda i: (off + i, 0)),
                  pl.BlockSpec((1, N_MOD, D_MODEL), lambda i: (mod_off + i // per, 0, 0)),
                  pl.BlockSpec((1, D_MODEL), lambda i: (0, 0)),
                  pl.BlockSpec(memory_space=pl.ANY)],
        out_specs=pl.BlockSpec((tm, D_MODEL), lambda i: (i, 0)),
        out_shape=jax.ShapeDtypeStruct((n_rows, D_MODEL), F32),
        scratch_shapes=[pltpu.VMEM((2, tm, PACK_COLS), jnp.uint32), pltpu.SemaphoreType.DMA(())],
        compiler_params=_cparams(("arbitrary",)),
        name="moe_combine",
    )(pos3, route, x1, mod, final_w, ys)


def _grid_sincos_2d(n_tokens):
    rows = n_tokens // GRID_W
    quarter = D_MODEL // 4
    omega = (1.0 / (np.float32(POS_BASE) ** (np.arange(quarter, dtype=np.float32) / np.float32(quarter)))).astype(np.float32)
    r = np.repeat(np.arange(rows, dtype=np.float32), GRID_W)
    c = np.tile(np.arange(GRID_W, dtype=np.float32), rows)
    ar = (r[:, None] * omega).astype(np.float32)
    ac = (c[:, None] * omega).astype(np.float32)
    return jnp.asarray(np.concatenate([np.sin(ar), np.cos(ar), np.sin(ac), np.cos(ac)], axis=-1).astype(np.float32))


def _pad_rows(a, n):
    return jnp.concatenate([a, jnp.zeros((n - a.shape[0],) + a.shape[1:], a.dtype)], axis=0)


def _selection_constants():
    gh = SSD_GROUP_HEADS
    ssd_sel = np.zeros((SSD_GROUPS, 3, LANES, LANES), np.float32)
    for g in range(SSD_GROUPS):
        for part in range(3):
            for d in range(2):
                for j in range(gh):
                    ssd_sel[g, part, d * SSD_HEADS + g * gh + j, SSD_GATE_LANES * part + d * gh + j] = 1.0
    ssd_exp = np.zeros((2, LANES, SSD_GROUP_COLS), np.float32)
    for d in range(2):
        for part in range(3):
            for c in range(SSD_GROUP_COLS):
                ssd_exp[d, SSD_GATE_LANES * part + d * gh + c // SSD_HEAD_DIM, c] = 1.0
    dn_sel = np.zeros((DN_HEADS // DN_HB, 3, LANES, LANES), np.float32)
    base = 2 * SSD_HEADS
    for h in range(DN_HEADS):
        dst = DN_GATE_LANES * (h % DN_HB)
        for part in range(3):
            for d in range(2):
                dn_sel[h // DN_HB, part, base + d * DN_HEADS + h, dst + 4 * part + d] = 1.0
                dn_sel[h // DN_HB, part, base + 2 * DN_HEADS + d * DN_HEADS + h, dst + 4 * part + 2 + d] = 1.0
    return ssd_sel, ssd_exp, dn_sel


def _ssd_consts(conv_w, conv_b, ssd_d):
    bcw, bcb = [], []
    for g in range(SSD_GROUPS):
        for base in (SSD_WIDTH, SSD_WIDTH + SSD_GROUPS * SSD_STATE):
            sl = slice(base + g * SSD_STATE, base + (g + 1) * SSD_STATE)
            bcw.append(conv_w[:, sl])
            bcb.append(conv_b[sl])
    ssd_sel, ssd_exp, _ = _selection_constants()
    return {
        "cw_x": _pad_rows(conv_w[:, 0:SSD_WIDTH], SUBLANES),
        "cb_x": conv_b[0:SSD_WIDTH].reshape(1, -1),
        "cw_bc": _pad_rows(jnp.concatenate(bcw, axis=1), SUBLANES),
        "cb_bc": jnp.concatenate(bcb).reshape(1, -1),
        "d_exp": jnp.repeat(ssd_d, SSD_HEAD_DIM).reshape(1, -1),
        "ssd_sel": jnp.asarray(ssd_sel, dtype=BF16),
        "ssd_exp": jnp.asarray(ssd_exp, dtype=BF16),
    }


def _dn_consts(conv_w, conv_b, ssd_dt_bias, ssd_a_log, dn_dt_bias, dn_a_log, dn_norm_w):
    n_ssd = 2 * SSD_HEADS
    n_dn = 2 * DN_HEADS
    gate_params = jnp.zeros((SUBLANES, LANES), F32)
    gate_params = gate_params.at[0, 0:n_ssd].set(ssd_dt_bias.reshape(-1))
    gate_params = gate_params.at[0, n_ssd:n_ssd + n_dn].set(dn_dt_bias.reshape(-1))
    gate_params = gate_params.at[1, 0:n_ssd].set(ssd_a_log.reshape(-1))
    gate_params = gate_params.at[1, n_ssd:n_ssd + n_dn].set(dn_a_log.reshape(-1))
    dn_sel = _selection_constants()[2]
    return {
        "cw_dn": _pad_rows(conv_w, SUBLANES),
        "cb_dn": conv_b.reshape(1, -1),
        "gate_params": gate_params,
        "dn_sel": jnp.asarray(dn_sel, dtype=BF16),
        "dn_norm_w": dn_norm_w.reshape(1, -1),
    }


def kernel(x_prompt, x_sample, state_ssd_fwd, state_ssd_bwd, state_dn_fwd, state_dn_bwd, c, c_ctx, w_ada, b_ada, norm1_w, w_in, conv_ssd_w, conv_ssd_b, conv_dn_w, conv_dn_b, ssd_dt_bias, ssd_a_log, ssd_d, ssd_norm_w, dn_dt_bias, dn_a_log, dn_norm_w, w_out, norm2_w, w_router_group, b_router_group, w_router_expert, b_router_expert, w_gate, w_up, w_down, final_norm_w):
    n_b, seq, _ = x_prompt.shape
    dec_b, dec_seq, _ = x_sample.shape
    n_ctx = n_b * seq
    n_lat = dec_b * dec_seq
    n_total = n_ctx + n_lat
    layer = 0

    wi = w_in[layer]
    xbc0 = SSD_WIDTH
    bm0 = xbc0 + SSD_WIDTH
    cm0 = bm0 + SSD_GROUPS * SSD_STATE
    dt0 = cm0 + SSD_GROUPS * SSD_STATE
    qkv0 = dt0 + 2 * SSD_HEADS
    zd0 = qkv0 + 3 * DN_WIDTH
    ad0 = zd0 + DN_WIDTH
    bd0 = ad0 + 2 * DN_HEADS
    bc_cols = []
    for g in range(SSD_GROUPS):
        bc_cols += [wi[:, bm0 + g * SSD_STATE:bm0 + (g + 1) * SSD_STATE],
                    wi[:, cm0 + g * SSD_STATE:cm0 + (g + 1) * SSD_STATE]]
    w_r = jnp.concatenate(
        [wi[:, 0:SSD_WIDTH], wi[:, xbc0:bm0]] + bc_cols +
        [wi[:, qkv0:zd0], wi[:, zd0:ad0], wi[:, dt0:qkv0], wi[:, ad0:bd0], wi[:, bd0:bd0 + 2 * DN_HEADS],
         jnp.zeros((D_MODEL, PROJ_COLS - COL_SMALL - 2 * SSD_HEADS - 4 * DN_HEADS), F32)],
        axis=1).astype(BF16)

    w_router = jnp.concatenate(
        [w_router_group[layer], w_router_expert[layer],
         jnp.zeros((D_MODEL, LANES - N_EXPERT_GROUPS - N_EXPERTS), F32)], axis=1)
    b_router = jnp.concatenate(
        [b_router_group[layer], b_router_expert[layer],
         jnp.zeros((LANES - N_EXPERT_GROUPS - N_EXPERTS,), F32)]).reshape(1, LANES)
    consts = {
        "ssd_norm_w": ssd_norm_w[layer].reshape(1, -1),
        "norm2_w": norm2_w[layer].reshape(1, -1),
        "w_out": w_out[layer].astype(BF16),
        "w_router": w_router.astype(BF16),
        "b_router": b_router,
    }
    consts.update(_ssd_consts(conv_ssd_w[layer], conv_ssd_b[layer], ssd_d[layer]))
    consts.update(_dn_consts(conv_dn_w[layer], conv_dn_b[layer], ssd_dt_bias[layer], ssd_a_log[layer],
                             dn_dt_bias[layer], dn_a_log[layer], dn_norm_w[layer]))

    cond = _pad_rows(jnp.concatenate([c_ctx[None, :], c], axis=0), SUBLANES)
    mod = _adaln(cond, w_ada[layer], b_ada[layer]).reshape(SUBLANES, N_MOD, D_MODEL)

    pos = _grid_sincos_2d(dec_seq)
    n1w = norm1_w[layer].reshape(1, -1)
    proj_main, small, x0 = _inproj(x_prompt.reshape(n_ctx, D_MODEL), x_sample.reshape(n_lat, D_MODEL), pos,
                                   mod, n1w, w_r, dec_seq)
    proj = (proj_main, small)

    def ssd_state_in(s):
        return s[:, layer].transpose(0, 3, 1, 2).reshape(dec_b, SSD_STATE, SSD_WIDTH)

    yssd_ctx, hf, hb = _ssd(proj, consts, n_b, seq, 0, None, True)
    yssd_lat = _ssd(proj, consts, dec_b, dec_seq, n_ctx,
                    (ssd_state_in(state_ssd_fwd), ssd_state_in(state_ssd_bwd)), False)[0]
    ydn_ctx, sf, sb = _dn(proj, consts, n_ctx // dec_seq, dec_seq, seq, 0, None, True)
    ydn_lat = _dn(proj, consts, dec_b, dec_seq, dec_seq, n_ctx,
                  (state_dn_fwd[:, layer], state_dn_bwd[:, layer]), False)[0]

    x1, h2, route = _outproj(yssd_ctx, yssd_lat, ydn_ctx, ydn_lat, x0, mod, consts, dec_seq)

    slots, counts = _ranks(route)
    counts = counts[0, :N_EXPERTS].astype(jnp.int32)
    padded = ((counts + MOE_TILE - 1) // MOE_TILE) * MOE_TILE
    ends = jnp.cumsum(padded)
    offsets = ends - padded
    n_sorted = 2 * n_total + N_EXPERTS * MOE_TILE
    ids = slots[:, 0:2, :].astype(jnp.int32)
    expert_ids = jnp.arange(N_EXPERTS, dtype=jnp.int32)
    slot_base = jnp.sum(jnp.where(ids[..., None] == expert_ids, offsets, 0), axis=-1)
    pos_slots = slot_base + slots[:, 2:4, :].astype(jnp.int32)
    pos3 = pos_slots.reshape(n_total // ROW_TILE, 1, 2 * ROW_TILE)
    tile_start = jnp.arange(n_sorted // MOE_TILE, dtype=jnp.int32) * MOE_TILE
    tile_expert = jnp.minimum(jnp.sum(tile_start[:, None] >= ends[None, :], axis=1), N_EXPERTS - 1).astype(jnp.int32)
    tile_valid = (tile_start < ends[-1]).astype(jnp.int32)

    xs = _scatter_rows(pos3, h2, n_sorted)
    ys = _grouped_ffn(tile_expert, tile_valid, xs, w_gate[layer], w_up[layer], w_down[layer])
    fw = final_norm_w.reshape(1, -1)
    y_prompt = _combine(pos3, route, x1, mod, fw, ys, n_ctx, 0, n_ctx, 0)
    y_sample = _combine(pos3, route, x1, mod, fw, ys, n_lat, n_ctx, dec_seq, 1)

    def ssd_state_out(s):
        return s.reshape(n_b, SSD_STATE, SSD_HEADS, SSD_HEAD_DIM).transpose(0, 2, 3, 1)[:, None]

    return (y_prompt.reshape(n_b, seq, D_MODEL), y_sample.reshape(dec_b, dec_seq, D_MODEL),
            ssd_state_out(hf), ssd_state_out(hb), sf[:, None], sb[:, None])
```

```python
import functools

import numpy as np
import jax
import jax.numpy as jnp
from jax import lax
from jax.experimental import pallas as pl
from jax.experimental.pallas import tpu as pltpu

F32 = jnp.float32
BF16 = jnp.bfloat16
HI = lax.Precision.HIGHEST

D_MODEL = 1024
EPS = 1e-6
GRID_W = 64
POS_BASE = 10000.0
CONV_W = 5
N_MOD = 6
SSD_HEADS = 16
SSD_HEAD_DIM = 64
SSD_WIDTH = SSD_HEADS * SSD_HEAD_DIM
SSD_GROUPS = 2
SSD_STATE = 64
SSD_CHUNK = 128
DN_HEADS = 8
DN_HEAD_DIM = 128
DN_WIDTH = DN_HEADS * DN_HEAD_DIM
DN_CHUNK = 64
N_EXPERT_GROUPS = 4
EXPERTS_PER_GROUP = 8
N_EXPERTS = N_EXPERT_GROUPS * EXPERTS_PER_GROUP
D_FF = 512

LANES = 128
SUBLANES = 8
VMEM_LIMIT = 56 * 1024 * 1024

COL_Z_S = 0
COL_X = 1024
COL_BC = 2048
COL_QKV = 2304
COL_Z_D = 5376
COL_SMALL = 6400
PROJ_COLS = 6528

ROW_TILE = 512
MOE_TILE = 512
DMA_UNROLL = 8
NEG_BIG = -1e30


def _dot(a, b, prec=None):
    return jnp.dot(a, b, preferred_element_type=F32, precision=prec)


def _dot_nt(a, b, prec=None):
    return lax.dot_general(a, b, (((1,), (1,)), ((), ())), preferred_element_type=F32, precision=prec)


def _dot_tn(a, b, prec=None):
    return lax.dot_general(a, b, (((0,), (0,)), ((), ())), preferred_element_type=F32, precision=prec)


def _silu(x):
    return x * jax.nn.sigmoid(x)


def _softplus(x):
    return jnp.maximum(x, 0.0) + jnp.log1p(jnp.exp(-jnp.abs(x)))


def _rms_scale(x):
    return lax.rsqrt(jnp.mean(x * x, axis=-1, keepdims=True) + EPS)


PACK_COLS = D_MODEL // 2
HIGH_HALF = 0xFFFF0000


def _pack_rows(x):
    lo = pltpu.bitcast(x[:, :PACK_COLS].astype(BF16).astype(F32), jnp.uint32)
    hi = pltpu.bitcast(x[:, PACK_COLS:].astype(BF16).astype(F32), jnp.uint32)
    return (lo >> 16) | (hi & jnp.uint32(HIGH_HALF))


def _unpack_rows(u):
    return pltpu.bitcast(u << 16, F32), pltpu.bitcast(u & jnp.uint32(HIGH_HALF), F32)


def _cparams(sem, vmem=VMEM_LIMIT):
    return pltpu.CompilerParams(dimension_semantics=sem, vmem_limit_bytes=vmem)


def _ada_kernel(c_ref, w_ref, b_ref, o_ref):
    c = c_ref[...]
    o_ref[...] = _dot(_silu(c), w_ref[...], HI) + b_ref[...]


def _adaln(cond, w_ada, b_ada):
    n_out = N_MOD * D_MODEL
    tn = 1536
    return pl.pallas_call(
        _ada_kernel,
        grid=(n_out // tn,),
        in_specs=[pl.BlockSpec((SUBLANES, D_MODEL), lambda j: (0, 0)),
                  pl.BlockSpec((D_MODEL, tn), lambda j: (0, j)),
                  pl.BlockSpec((1, tn), lambda j: (0, j))],
        out_specs=pl.BlockSpec((SUBLANES, tn), lambda j: (0, j)),
        out_shape=jax.ShapeDtypeStruct((SUBLANES, n_out), F32),
        compiler_params=_cparams(("arbitrary",)),
        name="adaln",
    )(cond, w_ada, b_ada.reshape(1, n_out))


INPROJ_TILE = 512
INPROJ_CHUNK = 1280


def _inproj_kernel(xc_ref, xl_ref, pos_ref, mod_ref, nw_ref, w_ref, proj_ref, small_ref, x0_ref, *,
                   ctx_tiles):
    is_ctx = pl.program_id(0) < ctx_tiles
    x = jnp.where(is_ctx, xc_ref[...], xl_ref[...] + pos_ref[...])
    x0_ref[...] = x
    mod = mod_ref[0]
    hb = ((x * _rms_scale(x) * nw_ref[...]) * (1.0 + mod[1:2]) + mod[0:1]).astype(BF16)
    for j in range(0, COL_SMALL, INPROJ_CHUNK):
        proj_ref[:, j:j + INPROJ_CHUNK] = _dot(hb, w_ref[:, j:j + INPROJ_CHUNK]).astype(BF16)
    small_ref[...] = _dot(hb, w_ref[:, COL_SMALL:])


def _stream_maps(ctx_tiles, lat_tiles_per_mod):
    def ctx_row(i):
        return (jnp.minimum(i, ctx_tiles - 1), 0)

    def lat_row(i):
        return (jnp.maximum(i - ctx_tiles, 0), 0)

    def mod_row(i):
        return (jnp.where(i < ctx_tiles, 0, 1 + (i - ctx_tiles) // lat_tiles_per_mod), 0, 0)

    return ctx_row, lat_row, mod_row


def _inproj(x_ctx, x_lat, pos, mod, norm_w, w_r, lat_seq):
    tm = INPROJ_TILE
    ctx_tiles = x_ctx.shape[0] // tm
    n_total = x_ctx.shape[0] + x_lat.shape[0]
    n_pos = pos.shape[0] // tm
    ctx_row, lat_row, mod_row = _stream_maps(ctx_tiles, lat_seq // tm)
    return pl.pallas_call(
        functools.partial(_inproj_kernel, ctx_tiles=ctx_tiles),
        grid=(n_total // tm,),
        in_specs=[pl.BlockSpec((tm, D_MODEL), ctx_row),
                  pl.BlockSpec((tm, D_MODEL), lat_row),
                  pl.BlockSpec((tm, D_MODEL), lambda i: (jnp.maximum(i - ctx_tiles, 0) % n_pos, 0)),
                  pl.BlockSpec((1, N_MOD, D_MODEL), mod_row),
                  pl.BlockSpec((1, D_MODEL), lambda i: (0, 0)),
                  pl.BlockSpec((D_MODEL, PROJ_COLS), lambda i: (0, 0), pipeline_mode=pl.Buffered(1))],
        out_specs=[pl.BlockSpec((tm, COL_SMALL), lambda i: (i, 0)),
                   pl.BlockSpec((tm, LANES), lambda i: (i, 0)),
                   pl.BlockSpec((tm, D_MODEL), lambda i: (i, 0))],
        out_shape=[jax.ShapeDtypeStruct((n_total, COL_SMALL), BF16),
                   jax.ShapeDtypeStruct((n_total, LANES), F32),
                   jax.ShapeDtypeStruct((n_total, D_MODEL), F32)],
        compiler_params=_cparams(("arbitrary",)),
        name="inproj",
    )(x_ctx, x_lat, pos, mod, norm_w, w_r)


CONV_ROWS = 128
CONV_HALO = SUBLANES
CONV_UNROLL = 2
CONV_UNROLL_MAX_COLS = 256


def _conv_silu(src_ref, w_ref, b_ref, pad_ref, store, blk_len, seg_len):
    n_ch = src_ref.shape[1]
    n_seg = blk_len // seg_len
    chunks_per_seg = seg_len // CONV_ROWS
    zeros = jnp.zeros((CONV_HALO, n_ch), F32)
    for s in range(n_seg + 1):
        gap = s * (seg_len + CONV_HALO)
        pad_ref[gap:gap + CONV_HALO, :] = zeros

    def window_start(i):
        return pl.multiple_of(i * CONV_ROWS + (i // chunks_per_seg) * CONV_HALO, CONV_HALO)

    def copy_body(i, carry):
        r0 = pl.multiple_of(i * CONV_ROWS, CONV_ROWS)
        pad_ref[pl.ds(window_start(i) + CONV_HALO, CONV_ROWS), :] = src_ref[pl.ds(r0, CONV_ROWS), :].astype(F32)
        return carry

    lax.fori_loop(0, blk_len // CONV_ROWS, copy_body, 0)
    win = CONV_ROWS + 2 * CONV_HALO
    w = w_ref[...]
    b = b_ref[...]

    unroll = CONV_UNROLL if n_ch <= CONV_UNROLL_MAX_COLS else 1

    def body(i2, carry):
        for sub in range(unroll):
            i = i2 * unroll + sub
            v = pad_ref[pl.ds(window_start(i), win), :]
            acc = jnp.zeros((CONV_ROWS, n_ch), F32) + b
            for k in range(CONV_W):
                shift = (CONV_W // 2 - k) % win
                sh = v if shift == 0 else pltpu.roll(v, shift, 0)
                acc = acc + w[k:k + 1, :] * sh[CONV_HALO:CONV_HALO + CONV_ROWS, :]
            store(pl.ds(pl.multiple_of(i * CONV_ROWS, CONV_ROWS), CONV_ROWS), _silu(acc))
        return carry

    lax.fori_loop(0, blk_len // (CONV_ROWS * unroll), body, 0)


def _conv_pad_rows(blk_len, seg_len):
    return blk_len + (blk_len // seg_len + 1) * CONV_HALO


SSD_GROUP_HEADS = SSD_HEADS // SSD_GROUPS
SSD_GROUP_COLS = SSD_GROUP_HEADS * SSD_HEAD_DIM
SSD_GATE_LANES = 2 * SSD_GROUP_HEADS
SSD_UNROLL = 2


def _ssd_kernel(*refs, seq_len, has_init, emit_state):
    refs = list(refs)
    (x_ref, bc_ref, z_ref, sm_ref, cwx_ref, cbx_ref, cwbc_ref, cbbc_ref, gp_ref, dexp_ref,
     sel_ref, exp_ref) = refs[:12]
    refs = refs[12:]
    if has_init:
        h0f_ref, h0b_ref = refs[:2]
        refs = refs[2:]
    y_ref = refs.pop(0)
    if emit_state:
        hf_ref, hb_ref = refs[:2]
        refs = refs[2:]
    xpad, bcpad, xc, bcc, dt3, da3, yacc, hst = refs

    q = SSD_CHUNK
    n_chunks = seq_len // q
    n_gl = SSD_GATE_LANES
    lane = lax.broadcasted_iota(jnp.int32, (q, LANES), 1)
    gate_lane = lane < n_gl

    def sum_pieces(m):
        return m + pltpu.roll(m, LANES - n_gl, 1) + pltpu.roll(m, LANES - 2 * n_gl, 1)

    def pack_pieces(v):
        v = jnp.where(gate_lane, v, 0.0)
        hi = v.astype(BF16).astype(F32)
        rest = v - hi
        mid = rest.astype(BF16).astype(F32)
        lo = rest - mid
        return (hi + pltpu.roll(mid, n_gl, 1) + pltpu.roll(lo, 2 * n_gl, 1)).astype(BF16)

    def store_x(rows, val):
        xc[rows, :] = val
        yacc[rows, :] = val * dexp_ref[...]

    _conv_silu(x_ref, cwx_ref, cbx_ref, xpad, store_x, seq_len, seq_len)

    def store_bc(rows, val):
        bcc[rows, :] = val

    _conv_silu(bc_ref, cwbc_ref, cbbc_ref, bcpad, store_bc, seq_len, seq_len)

    gp = gp_ref[...]
    bias_row = gp[0:1, :]
    aneg_sel = _dot(-jnp.exp(gp), sel_ref[0, 0].astype(F32), HI)[1:2, :]

    def gate_body(i, carry):
        for sub in range(SSD_UNROLL):
            rows = pl.ds(pl.multiple_of((i * SSD_UNROLL + sub) * q, q), q)
            dt_full = _softplus(sm_ref[rows, :].astype(F32) + bias_row)
            hi = dt_full.astype(BF16)
            rest = dt_full - hi.astype(F32)
            mid = rest.astype(BF16)
            lo = (rest - mid.astype(F32)).astype(BF16)
            dt_sel = sum_pieces(_dot(hi, sel_ref[0, 0]) + _dot(mid, sel_ref[0, 1])
                                + _dot(lo, sel_ref[0, 2]))
            dt3[rows, :] = pack_pieces(dt_sel)
            da3[rows, :] = pack_pieces(dt_sel * aneg_sel)
        return carry

    lax.fori_loop(0, n_chunks // SSD_UNROLL, gate_body, 0)

    if has_init:
        hst[0] = h0f_ref[0]
        hst[1] = h0b_ref[0]
    else:
        hst[...] = jnp.zeros(hst.shape, F32)

    ri = lax.broadcasted_iota(jnp.int32, (q, q), 0)
    ci = lax.broadcasted_iota(jnp.int32, (q, q), 1)
    low_half = lane < SSD_HEAD_DIM
    masks = [ri >= ci, ri <= ci]
    tris = [jnp.where(m, 1.0, 0.0).astype(BF16) for m in masks]

    def chunk(d, c):
        rows = pl.ds(pl.multiple_of(c * q, q), q)
        mask = masks[d]
        xck = xc[rows, :]
        bcck = bcc[rows, :]
        b_c = bcck[:, :SSD_STATE].astype(BF16)
        c_c = bcck[:, SSD_STATE:].astype(BF16)
        cum = sum_pieces(_dot(tris[d], da3[rows, :]))
        cum_t = cum.T
        dt_x = _dot(dt3[rows, :], exp_ref[d])
        ac_x = _dot(pack_pieces(cum), exp_ref[d])
        ae_x = ac_x[q - 1:q, :] if d == 0 else ac_x[0:1, :]
        xdt = xck * dt_x
        cb = _dot_nt(c_c, b_c)
        lane0 = d * SSD_GROUP_HEADS
        parts = []
        for hp in range(SSD_GROUP_HEADS // 2):
            ms = []
            for j in (2 * hp, 2 * hp + 1):
                col = cum[:, lane0 + j:lane0 + j + 1]
                row = cum_t[lane0 + j:lane0 + j + 1, :]
                dec = jnp.exp(jnp.where(mask, col - row, NEG_BIG))
                ms.append((cb * dec).astype(BF16))
            x2 = xdt[:, hp * LANES:(hp + 1) * LANES]
            w_lo = jnp.where(low_half, x2, 0.0)
            w_hi = jnp.where(low_half, 0.0, x2)
            parts.append(_dot(jnp.concatenate(ms, axis=1),
                              jnp.concatenate([w_lo, w_hi], axis=0).astype(BF16)))
        y_diag = jnp.concatenate(parts, axis=1)
        hs = hst[d]
        y_off = _dot(c_c, hs.astype(BF16)) * jnp.exp(ac_x)
        yacc[rows, :] += y_diag + y_off
        hst[d] = hs * jnp.exp(ae_x) + _dot_tn(b_c, (xdt * jnp.exp(ae_x - ac_x)).astype(BF16))

    def step(s, carry):
        for sub in range(SSD_UNROLL):
            chunk(0, SSD_UNROLL * s + sub)
            chunk(1, n_chunks - 1 - SSD_UNROLL * s - sub)
        return carry

    lax.fori_loop(0, n_chunks // SSD_UNROLL, step, 0)

    def out_body(i, carry):
        rows = pl.ds(pl.multiple_of(i * q, q), q)
        y_ref[rows, :] = yacc[rows, :] * _silu(z_ref[rows, :].astype(F32))
        return carry

    lax.fori_loop(0, n_chunks, out_body, 0)
    if emit_state:
        hf_ref[0] = hst[0]
        hb_ref[0] = hst[1]


def _ssd(proj, consts, n_seq, seq_len, row_off, init, emit_state):
    rb = row_off // seq_len
    cw = SSD_GROUP_COLS
    has_init = init is not None

    def col(block_cols, base):
        return base // block_cols

    in_specs = [
        pl.BlockSpec((seq_len, cw), lambda b, g: (rb + b, col(cw, COL_X) + g)),
        pl.BlockSpec((seq_len, LANES), lambda b, g: (rb + b, col(LANES, COL_BC) + g)),
        pl.BlockSpec((seq_len, cw), lambda b, g: (rb + b, col(cw, COL_Z_S) + g)),
        pl.BlockSpec((seq_len, LANES), lambda b, g: (rb + b, 0)),
        pl.BlockSpec((SUBLANES, cw), lambda b, g: (0, g)),
        pl.BlockSpec((1, cw), lambda b, g: (0, g)),
        pl.BlockSpec((SUBLANES, LANES), lambda b, g: (0, g)),
        pl.BlockSpec((1, LANES), lambda b, g: (0, g)),
        pl.BlockSpec((SUBLANES, LANES), lambda b, g: (0, 0)),
        pl.BlockSpec((1, cw), lambda b, g: (0, g)),
        pl.BlockSpec((1, 3, LANES, LANES), lambda b, g: (g, 0, 0, 0)),
        pl.BlockSpec((2, LANES, cw), lambda b, g: (0, 0, 0)),
    ]
    proj_main, small = proj
    args = [proj_main, proj_main, proj_main, small, consts["cw_x"], consts["cb_x"], consts["cw_bc"], consts["cb_bc"],
            consts["gate_params"], consts["d_exp"], consts["ssd_sel"], consts["ssd_exp"]]
    if has_init:
        in_specs += [pl.BlockSpec((1, SSD_STATE, cw), lambda b, g: (b, 0, g))] * 2
        args += list(init)
    out_specs = [pl.BlockSpec((seq_len, cw), lambda b, g: (b, g))]
    out_shape = [jax.ShapeDtypeStruct((n_seq * seq_len, SSD_WIDTH), F32)]
    if emit_state:
        out_specs += [pl.BlockSpec((1, SSD_STATE, cw), lambda b, g: (b, 0, g))] * 2
        out_shape += [jax.ShapeDtypeStruct((n_seq, SSD_STATE, SSD_WIDTH), F32)] * 2
    scratch = [
        pltpu.VMEM((seq_len + 2 * CONV_HALO, cw), F32),
        pltpu.VMEM((seq_len + 2 * CONV_HALO, LANES), F32),
        pltpu.VMEM((seq_len, cw), F32),
        pltpu.VMEM((seq_len, LANES), F32),
        pltpu.VMEM((seq_len, LANES), BF16),
        pltpu.VMEM((seq_len, LANES), BF16),
        pltpu.VMEM((seq_len, cw), F32),
        pltpu.VMEM((2, SSD_STATE, cw), F32),
    ]
    return pl.pallas_call(
        functools.partial(_ssd_kernel, seq_len=seq_len, has_init=has_init, emit_state=emit_state),
        grid=(n_seq, SSD_GROUPS),
        in_specs=in_specs,
        out_specs=out_specs,
        out_shape=out_shape,
        scratch_shapes=scratch,
        compiler_params=_cparams(("arbitrary", "arbitrary")),
        name="ssd_scan",
    )(*args)


DN_UNIT = 2 * DN_CHUNK
DN_W2_ROWS = 2 * DN_CHUNK
DN_M2_ROWS = DN_CHUNK + DN_HEAD_DIM
DN_HB = 2
DN_UG = 4
DN_INST = DN_UG * DN_HB * 2
DN_GATE_LANES = 16


def _dn_kernel(*refs, blk_len, seg_len, has_init, emit_state):
    refs = list(refs)
    (q_ref, k_ref, v_ref, z_ref, sm_ref, cwq_ref, cbq_ref, cwk_ref, cbk_ref, cwv_ref, cbv_ref,
     gp_ref, nw_ref, sel_ref) = refs[:14]
    refs = refs[14:]
    if has_init:
        s0_refs = refs[:2]
        refs = refs[2:]
    y_ref = refs.pop(0)
    if emit_state:
        s_out_refs = refs[:2]
        refs = refs[2:]
    (pad, qs, ks, vs, gfull, cumf, cumb, oacc, sst, kk_s, qk_s, a_s, p_s, t_s, rhs_s,
     w2, m2, us, cds, vp_s, op_s) = refs

    cq = DN_CHUNK
    hd = DN_HEAD_DIM
    unit = DN_UNIT
    n_units = blk_len // unit
    seg_units = seg_len // unit
    run_units = min(seg_units, DN_UG)
    chains_per_dir = DN_UG // run_units
    scale = DN_HEAD_DIM ** -0.5

    def l2n(v):
        return v * lax.rsqrt(jnp.sum(v * v, axis=-1, keepdims=True) + EPS)

    def per_head(fn, val):
        return jnp.concatenate([fn(val[:, j * hd:(j + 1) * hd]) for j in range(DN_HB)], axis=1)

    def store_q(rows, val):
        qs[rows, :] = per_head(lambda v: l2n(v) * scale, val)

    def store_k(rows, val):
        ks[rows, :] = per_head(l2n, val)

    def store_v(rows, val):
        vs[rows, :] = val
        oacc[rows, :] = jnp.zeros_like(val)

    _conv_silu(q_ref, cwq_ref, cbq_ref, pad, store_q, blk_len, seg_len)
    _conv_silu(k_ref, cwk_ref, cbk_ref, pad, store_k, blk_len, seg_len)
    _conv_silu(v_ref, cwv_ref, cbv_ref, pad, store_v, blk_len, seg_len)

    ri = lax.broadcasted_iota(jnp.int32, (unit, unit), 0)
    ci = lax.broadcasted_iota(jnp.int32, (unit, unit), 1)
    same = (ri // cq) == (ci // cq)
    incl = [jnp.logical_and(same, ri >= ci), jnp.logical_and(same, ri <= ci)]
    strict = [jnp.logical_and(same, ri > ci), jnp.logical_and(same, ri < ci)]
    tri = [jnp.where(m, 1.0, 0.0).astype(BF16) for m in incl]
    col_in_chunk = [ci < cq, ci >= cq]
    eye = jnp.where(ri == ci, 1.0, 0.0)

    def level_mask(sz):
        return jnp.logical_and((ri // (2 * sz)) == (ci // (2 * sz)), (ri // sz) != (ci // sz))

    gp = gp_ref[...]
    bias_row = gp[0:1, :]
    aneg_row = -jnp.exp(gp[1:2, :])
    lane = lax.broadcasted_iota(jnp.int32, (CONV_ROWS, LANES), 1)
    is_decay_lane = lane < 2 * SSD_HEADS + 2 * DN_HEADS

    def sum_parts(m):
        return m + pltpu.roll(m, LANES - 4, 1) + pltpu.roll(m, LANES - 8, 1)

    def gate_body(i, carry):
        for sub in range(DN_UG):
            rows = pl.ds(pl.multiple_of((i * DN_UG + sub) * unit, unit), unit)
            sm = sm_ref[rows, :].astype(F32)
            comb = jnp.where(is_decay_lane, aneg_row * _softplus(sm + bias_row), jax.nn.sigmoid(sm))
            hi = comb.astype(BF16)
            rest = comb - hi.astype(F32)
            mid = rest.astype(BF16)
            lo = (rest - mid.astype(F32)).astype(BF16)
            g3 = _dot(hi, sel_ref[0, 0]) + _dot(mid, sel_ref[0, 1]) + _dot(lo, sel_ref[0, 2])
            g3b = g3.astype(BF16)
            gfull[rows, :] = sum_parts(g3)
            cumf[rows, :] = sum_parts(_dot(tri[0], g3b))
            cumb[rows, :] = sum_parts(_dot(tri[1], g3b))
        return carry

    lax.fori_loop(0, n_units // DN_UG, gate_body, 0)

    def chain_id(j, d, s):
        return (j * 2 + d) * chains_per_dir + s

    def init_state(j, d):
        return s0_refs[d][0, j] if has_init else jnp.zeros((hd, hd), F32)

    for j in range(DN_HB):
        for d in range(2):
            for s in range(chains_per_dir):
                sst[chain_id(j, d, s)] = init_state(j, d)

    def inst(ul, j, d):
        return (ul * DN_HB + j) * 2 + d

    def unit_of(k, ul, d):
        u = k * DN_UG + ul
        return u if d == 0 else n_units - 1 - u

    zero_half = jnp.zeros((cq, hd), BF16)

    def solve_batch(k):
        for ul in range(DN_UG):
            for d in range(2):
                rows = pl.ds(pl.multiple_of(unit_of(k, ul, d) * unit, unit), unit)
                for j in range(DN_HB):
                    hs = slice(j * hd, (j + 1) * hd)
                    knb = ks[rows, hs].astype(BF16)
                    i = inst(ul, j, d)
                    kk_s[i] = _dot_nt(knb, knb)
                    qk_s[i] = _dot_nt(qs[rows, hs].astype(BF16), knb)
        for ul in range(DN_UG):
            for d in range(2):
                rows = pl.ds(pl.multiple_of(unit_of(k, ul, d) * unit, unit), unit)
                gf = gfull[rows, :]
                cum = (cumf if d == 0 else cumb)[rows, :]
                cum_t = cum.T
                tot = cumf[rows, :] + cumb[rows, :] - gf
                for j in range(DN_HB):
                    hs = slice(j * hd, (j + 1) * hd)
                    lane0 = DN_GATE_LANES * j
                    i = inst(ul, j, d)
                    qn = qs[rows, hs]
                    kn = ks[rows, hs]
                    gcol = cum[:, lane0 + d:lane0 + d + 1]
                    grow = cum_t[lane0 + d:lane0 + d + 1, :]
                    bcol = gf[:, lane0 + 2 + d:lane0 + 3 + d]
                    gtot = tot[:, lane0 + d:lane0 + d + 1]
                    dec = jnp.exp(jnp.where(incl[d], gcol - grow, NEG_BIG))
                    a_mat = jnp.where(strict[d], kk_s[i] * dec * bcol, 0.0)
                    a_s[i] = a_mat.astype(BF16)
                    t_s[i] = eye - jnp.where(level_mask(1), a_mat, 0.0)
                    eg = jnp.exp(gcol)
                    rhs_s[i] = jnp.concatenate([vs[rows, hs] * bcol, kn * (bcol * eg)], axis=1)
                    qk = (qk_s[i] * dec).astype(BF16)
                    qd = (qn * eg).astype(BF16)
                    kd_t = (kn * jnp.exp(gtot - gcol)).T
                    cd = jnp.exp(gtot)
                    for c in (0, 1):
                        rs = slice(c * cq, (c + 1) * cq)
                        w2[i, c * DN_W2_ROWS + cq:(c + 1) * DN_W2_ROWS, :] = qd[rs]
                        m2[i, c * DN_M2_ROWS:c * DN_M2_ROWS + cq, :] = qk[rs]
                        m2[i, c * DN_M2_ROWS + cq:(c + 1) * DN_M2_ROWS, :] = jnp.where(
                            col_in_chunk[c], kd_t, 0.0).astype(BF16)
                        cds[i, c * SUBLANES:(c + 1) * SUBLANES, :] = jnp.broadcast_to(
                            cd[c * cq:c * cq + SUBLANES], (SUBLANES, hd))
        sz = 2
        while sz < cq:
            mask = level_mask(sz)
            for i in range(DN_INST):
                p_s[i] = _dot(t_s[i].astype(BF16), jnp.where(mask, a_s[i], 0.0)).astype(BF16)
            for i in range(DN_INST):
                t = t_s[i]
                t_s[i] = t - _dot(p_s[i], t.astype(BF16))
            sz *= 2
        for i in range(DN_INST):
            rhs = rhs_s[i]
            sol = rhs + _dot((t_s[i] - eye).astype(BF16), rhs.astype(BF16))
            us[i] = sol[:, :hd]
            w = sol[:, hd:].astype(BF16)
            for c in (0, 1):
                w2[i, c * DN_W2_ROWS:c * DN_W2_ROWS + cq, :] = w[c * cq:(c + 1) * cq]

    def recur_batch(k):
        for t in range(2 * run_units):
            plan = []
            for j in range(DN_HB):
                for d in range(2):
                    for s in range(chains_per_dir):
                        ul = s * run_units + t // 2
                        c = t % 2 if d == 0 else 1 - t % 2
                        plan.append((chain_id(j, d, s), inst(ul, j, d), j, d, c,
                                     unit_of(k, ul, d) * unit + c * cq))
            for ch, i, j, d, c, pos in plan:
                first = (pos % seg_len == 0) if d == 0 else ((pos + cq) % seg_len == 0)
                s_val = jnp.where(first, init_state(j, d), sst[ch])
                sst[ch] = s_val
                tt = _dot(w2[i, c * DN_W2_ROWS:(c + 1) * DN_W2_ROWS, :], s_val.astype(BF16))
                v_new = (us[i, c * cq:(c + 1) * cq, :] - tt[0:cq]).astype(BF16)
                vp_s[ch] = jnp.concatenate([v_new, zero_half] if c == 0 else [zero_half, v_new], axis=0)
                op_s[ch] = tt[cq:2 * cq]
            for ch, i, j, d, c, pos in plan:
                t2 = _dot(m2[i, c * DN_M2_ROWS:(c + 1) * DN_M2_ROWS, :], vp_s[ch])
                orows = pl.ds(pl.multiple_of(pos, cq), cq)
                oacc[orows, j * hd:(j + 1) * hd] += op_s[ch] + t2[0:cq]
                s_new = sst[ch] * cds[i, c * SUBLANES:c * SUBLANES + 1, :] + t2[cq:cq + hd]
                sst[ch] = s_new
                if emit_state and t == 2 * run_units - 1:
                    s_out_refs[d][pos // seg_len, j] = s_new

    def batch(k, carry):
        solve_batch(k)
        recur_batch(k)
        return carry

    lax.fori_loop(0, n_units // DN_UG, batch, 0)

    def out_body(i, carry):
        for sub in range(CONV_UNROLL):
            rows = pl.ds(pl.multiple_of((i * CONV_UNROLL + sub) * CONV_ROWS, CONV_ROWS), CONV_ROWS)
            for j in range(DN_HB):
                hs = slice(j * hd, (j + 1) * hd)
                o = oacc[rows, hs]
                y_ref[rows, hs] = o * _rms_scale(o) * nw_ref[...] * _silu(z_ref[rows, hs].astype(F32))
        return carry

    lax.fori_loop(0, blk_len // (CONV_ROWS * CONV_UNROLL), out_body, 0)


def _dn(proj, consts, n_blk, blk_len, seg_len, row_off, init, emit_state):
    rb = row_off // blk_len
    hd = DN_HEAD_DIM
    has_init = init is not None
    n_seg = blk_len // seg_len
    assert CONV_ROWS == DN_UNIT and (blk_len // DN_UNIT) % DN_UG == 0
    seg_units = seg_len // DN_UNIT
    assert DN_UG % seg_units == 0 or seg_units % DN_UG == 0
    n_chains = DN_HB * 2 * (DN_UG // min(seg_units, DN_UG))
    bw = DN_HB * hd
    n_hp = DN_HEADS // DN_HB
    cq0 = COL_QKV // bw
    in_specs = [
        pl.BlockSpec((blk_len, bw), lambda b, h: (rb + b, cq0 + h)),
        pl.BlockSpec((blk_len, bw), lambda b, h: (rb + b, cq0 + n_hp + h)),
        pl.BlockSpec((blk_len, bw), lambda b, h: (rb + b, cq0 + 2 * n_hp + h)),
        pl.BlockSpec((blk_len, bw), lambda b, h: (rb + b, COL_Z_D // bw + h)),
        pl.BlockSpec((blk_len, LANES), lambda b, h: (rb + b, 0)),
        pl.BlockSpec((SUBLANES, bw), lambda b, h: (0, h)),
        pl.BlockSpec((1, bw), lambda b, h: (0, h)),
        pl.BlockSpec((SUBLANES, bw), lambda b, h: (0, n_hp + h)),
        pl.BlockSpec((1, bw), lambda b, h: (0, n_hp + h)),
        pl.BlockSpec((SUBLANES, bw), lambda b, h: (0, 2 * n_hp + h)),
        pl.BlockSpec((1, bw), lambda b, h: (0, 2 * n_hp + h)),
        pl.BlockSpec((SUBLANES, LANES), lambda b, h: (0, 0)),
        pl.BlockSpec((1, hd), lambda b, h: (0, 0)),
        pl.BlockSpec((1, 3, LANES, LANES), lambda b, h: (h, 0, 0, 0)),
    ]
    proj_main, small = proj
    args = [proj_main, proj_main, proj_main, proj_main, small,
            consts["cw_dn"], consts["cb_dn"], consts["cw_dn"], consts["cb_dn"], consts["cw_dn"],
            consts["cb_dn"], consts["gate_params"], consts["dn_norm_w"], consts["dn_sel"]]
    if has_init:
        assert n_seg == 1
        in_specs += [pl.BlockSpec((1, DN_HB, hd, hd), lambda b, h: (b, h, 0, 0))] * 2
        args += list(init)
    out_specs = [pl.BlockSpec((blk_len, bw), lambda b, h: (b, h))]
    out_shape = [jax.ShapeDtypeStruct((n_blk * blk_len, DN_WIDTH), F32)]
    if emit_state:
        out_specs += [pl.BlockSpec((n_seg, DN_HB, hd, hd), lambda b, h: (b, h, 0, 0))] * 2
        out_shape += [jax.ShapeDtypeStruct((n_blk * n_seg, DN_HEADS, hd, hd), F32)] * 2
    scratch = [
        pltpu.VMEM((_conv_pad_rows(blk_len, seg_len), bw), F32),
        pltpu.VMEM((blk_len, bw), F32),
        pltpu.VMEM((blk_len, bw), F32),
        pltpu.VMEM((blk_len, bw), F32),
        pltpu.VMEM((blk_len, LANES), F32),
        pltpu.VMEM((blk_len, LANES), F32),
        pltpu.VMEM((blk_len, LANES), F32),
        pltpu.VMEM((blk_len, bw), F32),
        pltpu.VMEM((n_chains, hd, hd), F32),
        pltpu.VMEM((DN_INST, DN_UNIT, DN_UNIT), F32),
        pltpu.VMEM((DN_INST, DN_UNIT, DN_UNIT), F32),
        pltpu.VMEM((DN_INST, DN_UNIT, DN_UNIT), BF16),
        pltpu.VMEM((DN_INST, DN_UNIT, DN_UNIT), BF16),
        pltpu.VMEM((DN_INST, DN_UNIT, DN_UNIT), F32),
        pltpu.VMEM((DN_INST, DN_UNIT, 2 * hd), F32),
        pltpu.VMEM((DN_INST, 2 * DN_W2_ROWS, hd), BF16),
        pltpu.VMEM((DN_INST, 2 * DN_M2_ROWS, hd), BF16),
        pltpu.VMEM((DN_INST, DN_UNIT, hd), F32),
        pltpu.VMEM((DN_INST, 2 * SUBLANES, hd), F32),
        pltpu.VMEM((n_chains, DN_UNIT, hd), BF16),
        pltpu.VMEM((n_chains, DN_CHUNK, hd), F32),
    ]
    return pl.pallas_call(
        functools.partial(_dn_kernel, blk_len=blk_len, seg_len=seg_len, has_init=has_init,
                          emit_state=emit_state),
        grid=(n_blk, n_hp),
        in_specs=in_specs,
        out_specs=out_specs,
        out_shape=out_shape,
        scratch_shapes=scratch,
        compiler_params=_cparams(("arbitrary", "arbitrary")),
        name="dn_scan",
    )(*args)


ROUTE_ID_LANE = 0
ROUTE_GATE_LANE = 2


def _outproj_kernel(ysc_ref, ysl_ref, ydc_ref, ydl_ref, x0_ref, mod_ref, snw_ref, n2w_ref, wo_ref,
                    wr_ref, br_ref, x1_ref, h2_ref, route_ref, *, ctx_tiles):
    is_ctx = pl.program_id(0) < ctx_tiles
    ys = jnp.where(is_ctx, ysc_ref[...], ysl_ref[...])
    ysn = (ys * _rms_scale(ys) * snw_ref[...]).astype(BF16)
    ydn = jnp.where(is_ctx, ydc_ref[...], ydl_ref[...]).astype(BF16)
    m = _dot(ysn, wo_ref[0:SSD_WIDTH, :]) + _dot(ydn, wo_ref[SSD_WIDTH:, :])
    mod = mod_ref[0]
    x1 = x0_ref[...] + mod[2:3] * m
    x1_ref[...] = x1
    h2 = (x1 * _rms_scale(x1) * n2w_ref[...]) * (1.0 + mod[4:5]) + mod[3:4]
    h2_ref[...] = _pack_rows(h2)
    logits = _dot(h2.astype(BF16), wr_ref[...]) + br_ref[...]
    ln = lax.broadcasted_iota(jnp.int32, logits.shape, 1)
    is_g = ln < N_EXPERT_GROUPS
    gl = jnp.where(is_g, logits, NEG_BIG)
    gmax = jnp.max(gl, axis=-1, keepdims=True)
    gidx = jnp.min(jnp.where(gl == gmax, ln, LANES), axis=-1, keepdims=True)
    gw = 1.0 / jnp.sum(jnp.where(is_g, jnp.exp(gl - gmax), 0.0), axis=-1, keepdims=True)
    lo = N_EXPERT_GROUPS + gidx * EXPERTS_PER_GROUP
    in_grp = (ln >= lo) & (ln < lo + EXPERTS_PER_GROUP)
    el = jnp.where(in_grp, logits, NEG_BIG)
    m1 = jnp.max(el, axis=-1, keepdims=True)
    i1 = jnp.min(jnp.where(el == m1, ln, LANES), axis=-1, keepdims=True)
    el2 = jnp.where(ln == i1, NEG_BIG, el)
    m2 = jnp.max(el2, axis=-1, keepdims=True)
    i2 = jnp.min(jnp.where(el2 == m2, ln, LANES), axis=-1, keepdims=True)
    t = jnp.exp(m2 - m1)
    g1 = gw / (1.0 + t)
    g2 = gw * t / (1.0 + t)
    id1 = (i1 - N_EXPERT_GROUPS).astype(F32)
    id2 = (i2 - N_EXPERT_GROUPS).astype(F32)
    route = jnp.where(ln == 0, id1, jnp.where(ln == 1, id2, jnp.where(ln == 2, g1,
                                                                      jnp.where(ln == 3, g2, 0.0))))
    route_ref[...] = route


def _outproj(yssd_ctx, yssd_lat, ydn_ctx, ydn_lat, x0, mod, consts, lat_seq):
    n_total = x0.shape[0]
    tm = INPROJ_TILE
    ctx_tiles = yssd_ctx.shape[0] // tm
    ctx_row, lat_row, mod_row = _stream_maps(ctx_tiles, lat_seq // tm)
    row = lambda i: (i, 0)
    const2 = lambda i: (0, 0)
    return pl.pallas_call(
        functools.partial(_outproj_kernel, ctx_tiles=ctx_tiles),
        grid=(n_total // tm,),
        in_specs=[pl.BlockSpec((tm, SSD_WIDTH), ctx_row),
                  pl.BlockSpec((tm, SSD_WIDTH), lat_row),
                  pl.BlockSpec((tm, DN_WIDTH), ctx_row),
                  pl.BlockSpec((tm, DN_WIDTH), lat_row),
                  pl.BlockSpec((tm, D_MODEL), row),
                  pl.BlockSpec((1, N_MOD, D_MODEL), mod_row),
                  pl.BlockSpec((1, SSD_WIDTH), const2),
                  pl.BlockSpec((1, D_MODEL), const2),
                  pl.BlockSpec((SSD_WIDTH + DN_WIDTH, D_MODEL), const2),
                  pl.BlockSpec((D_MODEL, LANES), const2),
                  pl.BlockSpec((1, LANES), const2)],
        out_specs=[pl.BlockSpec((tm, D_MODEL), row),
                   pl.BlockSpec((tm, PACK_COLS), row),
                   pl.BlockSpec((tm, LANES), row)],
        out_shape=[jax.ShapeDtypeStruct((n_total, D_MODEL), F32),
                   jax.ShapeDtypeStruct((n_total, PACK_COLS), jnp.uint32),
                   jax.ShapeDtypeStruct((n_total, LANES), F32)],
        compiler_params=_cparams(("arbitrary",)),
        name="outproj_router",
    )(yssd_ctx, yssd_lat, ydn_ctx, ydn_lat, x0, mod, consts["ssd_norm_w"], consts["norm2_w"],
      consts["w_out"], consts["w_router"], consts["b_router"])


def _rank_kernel(route_ref, rank_ref, count_ref, carry):
    i = pl.program_id(0)

    @pl.when(i == 0)
    def _():
        carry[...] = jnp.zeros(carry.shape, F32)

    route = route_ref[...]
    n = route.shape[0]
    ln = lax.broadcasted_iota(jnp.int32, route.shape, 1)
    id1 = route[:, 0:1].astype(jnp.int32)
    id2 = route[:, 1:2].astype(jnp.int32)
    hit1 = ln == id1
    hit2 = ln == id2
    onehot = jnp.where(hit1, 1.0, jnp.where(hit2, 1.0, 0.0))
    ri = lax.broadcasted_iota(jnp.int32, (n, n), 0)
    ci = lax.broadcasted_iota(jnp.int32, (n, n), 1)
    before = (ri > ci).astype(BF16)
    tot = _dot(before, onehot.astype(BF16)) + carry[0:1, :]
    r1 = jnp.sum(jnp.where(hit1, tot, 0.0), axis=-1, keepdims=True)
    r2 = jnp.sum(jnp.where(hit2, tot, 0.0), axis=-1, keepdims=True)
    info = jnp.where(ln < 2, route, jnp.where(ln == 2, r1, jnp.where(ln == 3, r2, 0.0)))
    rank_ref[0] = info.T[0:SUBLANES, :]
    new = carry[...] + jnp.sum(onehot, axis=0, keepdims=True)
    carry[...] = new
    count_ref[...] = new


def _ranks(route):
    n_total = route.shape[0]
    tm = ROW_TILE
    return pl.pallas_call(
        _rank_kernel,
        grid=(n_total // tm,),
        in_specs=[pl.BlockSpec((tm, LANES), lambda i: (i, 0))],
        out_specs=[pl.BlockSpec((1, SUBLANES, tm), lambda i: (i, 0, 0)),
                   pl.BlockSpec((SUBLANES, LANES), lambda i: (0, 0))],
        out_shape=[jax.ShapeDtypeStruct((n_total // tm, SUBLANES, tm), F32),
                   jax.ShapeDtypeStruct((SUBLANES, LANES), F32)],
        scratch_shapes=[pltpu.VMEM((SUBLANES, LANES), F32)],
        compiler_params=_cparams(("arbitrary",)),
        name="slot_ranks",
    )(route)


def _scatter_kernel(pos_ref, h_ref, xs_in_ref, xs_ref, sem):
    del xs_in_ref
    n = h_ref.shape[0]

    def body(tb, carry):
        for u in range(DMA_UNROLL):
            t = tb * DMA_UNROLL + u
            for k in range(2):
                p = pos_ref[0, 0, k * n + t]
                pltpu.make_async_copy(h_ref.at[pl.ds(t, 1), :], xs_ref.at[pl.ds(p, 1), :],
                                      sem).start(priority=k)
        return carry

    lax.fori_loop(0, n // DMA_UNROLL, body, 0)
    for _ in range(2):
        pltpu.make_async_copy(h_ref, xs_ref.at[pl.ds(0, n), :], sem).wait()


def _scatter_rows(pos3, h2, n_sorted):
    n_total = h2.shape[0]
    tm = ROW_TILE
    zeros = jnp.zeros((n_sorted, PACK_COLS), jnp.uint32)
    return pl.pallas_call(
        _scatter_kernel,
        grid=(n_total // tm,),
        in_specs=[pl.BlockSpec((1, 1, 2 * tm), lambda i: (i, 0, 0), memory_space=pltpu.SMEM),
                  pl.BlockSpec((tm, PACK_COLS), lambda i: (i, 0)),
                  pl.BlockSpec(memory_space=pl.ANY)],
        out_specs=pl.BlockSpec(memory_space=pl.ANY),
        out_shape=jax.ShapeDtypeStruct((n_sorted, PACK_COLS), jnp.uint32),
        scratch_shapes=[pltpu.SemaphoreType.DMA(())],
        input_output_aliases={2: 0},
        compiler_params=_cparams(("arbitrary",)),
        name="moe_scatter",
    )(pos3, h2, zeros)


def _ffn_kernel(te_ref, tv_ref, x_ref, wg_ref, wu_ref, wd_ref, y_ref, wgb, wub, wdb):
    i = pl.program_id(0)
    prev = te_ref[jnp.maximum(i - 1, 0)]

    @pl.when(jnp.logical_or(i == 0, te_ref[i] != prev))
    def _():
        wgb[...] = wg_ref[0].astype(BF16)
        wub[...] = wu_ref[0].astype(BF16)
        wdb[...] = wd_ref[0].astype(BF16)

    @pl.when(tv_ref[i] > 0)
    def _():
        x_lo, x_hi = _unpack_rows(x_ref[...])
        xb = jnp.concatenate([x_lo.astype(BF16), x_hi.astype(BF16)], axis=1)
        a = _dot(xb, wgb[...])
        u = _dot(xb, wub[...])
        act = (_silu(a) * u).astype(BF16)
        y_ref[...] = _pack_rows(_dot(act, wdb[...]))

    @pl.when(tv_ref[i] == 0)
    def _():
        y_ref[...] = jnp.zeros(y_ref.shape, jnp.uint32)


def _grouped_ffn(tile_expert, tile_valid, xs, w_gate, w_up, w_down):
    n_sorted = xs.shape[0]
    tm = MOE_TILE
    grid_spec = pltpu.PrefetchScalarGridSpec(
        num_scalar_prefetch=2,
        grid=(n_sorted // tm,),
        in_specs=[pl.BlockSpec((tm, PACK_COLS), lambda i, te, tv: (i, 0)),
                  pl.BlockSpec((1, D_MODEL, D_FF), lambda i, te, tv: (te[i], 0, 0)),
                  pl.BlockSpec((1, D_MODEL, D_FF), lambda i, te, tv: (te[i], 0, 0)),
                  pl.BlockSpec((1, D_FF, D_MODEL), lambda i, te, tv: (te[i], 0, 0))],
        out_specs=pl.BlockSpec((tm, PACK_COLS), lambda i, te, tv: (i, 0)),
        scratch_shapes=[pltpu.VMEM((D_MODEL, D_FF), BF16),
                        pltpu.VMEM((D_MODEL, D_FF), BF16),
                        pltpu.VMEM((D_FF, D_MODEL), BF16)],
    )
    return pl.pallas_call(
        _ffn_kernel,
        grid_spec=grid_spec,
        out_shape=jax.ShapeDtypeStruct((n_sorted, PACK_COLS), jnp.uint32),
        compiler_params=_cparams(("arbitrary",)),
        name="moe_ffn",
    )(tile_expert, tile_valid, xs, w_gate, w_up, w_down)


def _combine_kernel(pos_ref, route_ref, x1_ref, mod_ref, fw_ref, ys_ref, out_ref, buf, sem):
    n = x1_ref.shape[0]
    route = route_ref[...]
    g1 = route[:, ROUTE_GATE_LANE:ROUTE_GATE_LANE + 1]
    g2 = route[:, ROUTE_GATE_LANE + 1:ROUTE_GATE_LANE + 2]

    def body(tb, carry):
        for u in range(DMA_UNROLL):
            t = tb * DMA_UNROLL + u
            for k in range(2):
                p = pos_ref[0, 0, k * n + t]
                pltpu.make_async_copy(ys_ref.at[pl.ds(p, 1), :], buf.at[k, pl.ds(t, 1), :],
                                      sem).start(priority=k)
        return carry

    lax.fori_loop(0, n // DMA_UNROLL, body, 0)
    for k in range(2):
        pltpu.make_async_copy(ys_ref.at[pl.ds(0, n), :], buf.at[k], sem).wait()
    y1_lo, y1_hi = _unpack_rows(buf[0])
    y2_lo, y2_hi = _unpack_rows(buf[1])
    moe = jnp.concatenate([g1 * y1_lo + g2 * y2_lo, g1 * y1_hi + g2 * y2_hi], axis=1)
    x2 = x1_ref[...] + mod_ref[0][5:6] * moe
    out_ref[...] = x2 * _rms_scale(x2) * fw_ref[...]


def _combine(pos3, route, x1, mod, final_w, ys, n_rows, row_off, tokens_per_mod, mod_off):
    tm = ROW_TILE
    off = row_off // tm
    per = tokens_per_mod // tm
    return pl.pallas_call(
        _combine_kernel,
        grid=(n_rows // tm,),
        in_specs=[pl.BlockSpec((1, 1, 2 * tm), lambda i: (off + i, 0, 0), memory_space=pltpu.SMEM),
                  pl.BlockSpec((tm, LANES), lambda i: (off + i, 0)),
                  pl.BlockSpec((tm, D_MODEL), lambda i: (off + i, 0)),
                  pl.BlockSpec((1, N_MOD, D_MODEL), lambda i: (mod_off + i // per, 0, 0)),
                  pl.BlockSpec((1, D_MODEL), lambda i: (0, 0)),
                  pl.BlockSpec(memory_space=pl.ANY)],
        out_specs=pl.BlockSpec((tm, D_MODEL), lambda i: (i, 0)),
        out_shape=jax.ShapeDtypeStruct((n_rows, D_MODEL), F32),
        scratch_shapes=[pltpu.VMEM((2, tm, PACK_COLS), jnp.uint32), pltpu.SemaphoreType.DMA(())],
        compiler_params=_cparams(("arbitrary",)),
        name="moe_combine",
    )(pos3, route, x1, mod, final_w, ys)


def _grid_sincos_2d(n_tokens):
    rows = n_tokens // GRID_W
    quarter = D_MODEL // 4
    omega = (1.0 / (np.float32(POS_BASE) ** (np.arange(quarter, dtype=np.float32) / np.float32(quarter)))).astype(np.float32)
    r = np.repeat(np.arange(rows, dtype=np.float32), GRID_W)
    c = np.tile(np.arange(GRID_W, dtype=np.float32), rows)
    ar = (r[:, None] * omega).astype(np.float32)
    ac = (c[:, None] * omega).astype(np.float32)
    return jnp.asarray(np.concatenate([np.sin(ar), np.cos(ar), np.sin(ac), np.cos(ac)], axis=-1).astype(np.float32))


def _pad_rows(a, n):
    return jnp.concatenate([a, jnp.zeros((n - a.shape[0],) + a.shape[1:], a.dtype)], axis=0)


def _selection_constants():
    gh = SSD_GROUP_HEADS
    ssd_sel = np.zeros((SSD_GROUPS, 3, LANES, LANES), np.float32)
    for g in range(SSD_GROUPS):
        for part in range(3):
            for d in range(2):
                for j in range(gh):
                    ssd_sel[g, part, d * SSD_HEADS + g * gh + j, SSD_GATE_LANES * part + d * gh + j] = 1.0
    ssd_exp = np.zeros((2, LANES, SSD_GROUP_COLS), np.float32)
    for d in range(2):
        for part in range(3):
            for c in range(SSD_GROUP_COLS):
                ssd_exp[d, SSD_GATE_LANES * part + d * gh + c // SSD_HEAD_DIM, c] = 1.0
    dn_sel = np.zeros((DN_HEADS // DN_HB, 3, LANES, LANES), np.float32)
    base = 2 * SSD_HEADS
    for h in range(DN_HEADS):
        dst = DN_GATE_LANES * (h % DN_HB)
        for part in range(3):
            for d in range(2):
                dn_sel[h // DN_HB, part, base + d * DN_HEADS + h, dst + 4 * part + d] = 1.0
                dn_sel[h // DN_HB, part, base + 2 * DN_HEADS + d * DN_HEADS + h, dst + 4 * part + 2 + d] = 1.0
    return ssd_sel, ssd_exp, dn_sel


def _ssd_consts(conv_w, conv_b, ssd_d):
    bcw, bcb = [], []
    for g in range(SSD_GROUPS):
        for base in (SSD_WIDTH, SSD_WIDTH + SSD_GROUPS * SSD_STATE):
            sl = slice(base + g * SSD_STATE, base + (g + 1) * SSD_STATE)
            bcw.append(conv_w[:, sl])
            bcb.append(conv_b[sl])
    ssd_sel, ssd_exp, _ = _selection_constants()
    return {
        "cw_x": _pad_rows(conv_w[:, 0:SSD_WIDTH], SUBLANES),
        "cb_x": conv_b[0:SSD_WIDTH].reshape(1, -1),
        "cw_bc": _pad_rows(jnp.concatenate(bcw, axis=1), SUBLANES),
        "cb_bc": jnp.concatenate(bcb).reshape(1, -1),
        "d_exp": jnp.repeat(ssd_d, SSD_HEAD_DIM).reshape(1, -1),
        "ssd_sel": jnp.asarray(ssd_sel, dtype=BF16),
        "ssd_exp": jnp.asarray(ssd_exp, dtype=BF16),
    }


def _dn_consts(conv_w, conv_b, ssd_dt_bias, ssd_a_log, dn_dt_bias, dn_a_log, dn_norm_w):
    n_ssd = 2 * SSD_HEADS
    n_dn = 2 * DN_HEADS
    gate_params = jnp.zeros((SUBLANES, LANES), F32)
    gate_params = gate_params.at[0, 0:n_ssd].set(ssd_dt_bias.reshape(-1))
    gate_params = gate_params.at[0, n_ssd:n_ssd + n_dn].set(dn_dt_bias.reshape(-1))
    gate_params = gate_params.at[1, 0:n_ssd].set(ssd_a_log.reshape(-1))
    gate_params = gate_params.at[1, n_ssd:n_ssd + n_dn].set(dn_a_log.reshape(-1))
    dn_sel = _selection_constants()[2]
    return {
        "cw_dn": _pad_rows(conv_w, SUBLANES),
        "cb_dn": conv_b.reshape(1, -1),
        "gate_params": gate_params,
        "dn_sel": jnp.asarray(dn_sel, dtype=BF16),
        "dn_norm_w": dn_norm_w.reshape(1, -1),
    }


def kernel(x_prompt, x_sample, state_ssd_fwd, state_ssd_bwd, state_dn_fwd, state_dn_bwd, c, c_ctx, w_ada, b_ada, norm1_w, w_in, conv_ssd_w, conv_ssd_b, conv_dn_w, conv_dn_b, ssd_dt_bias, ssd_a_log, ssd_d, ssd_norm_w, dn_dt_bias, dn_a_log, dn_norm_w, w_out, norm2_w, w_router_group, b_router_group, w_router_expert, b_router_expert, w_gate, w_up, w_down, final_norm_w):
    n_b, seq, _ = x_prompt.shape
    dec_b, dec_seq, _ = x_sample.shape
    n_ctx = n_b * seq
    n_lat = dec_b * dec_seq
    n_total = n_ctx + n_lat
    layer = 0

    wi = w_in[layer]
    xbc0 = SSD_WIDTH
    bm0 = xbc0 + SSD_WIDTH
    cm0 = bm0 + SSD_GROUPS * SSD_STATE
    dt0 = cm0 + SSD_GROUPS * SSD_STATE
    qkv0 = dt0 + 2 * SSD_HEADS
    zd0 = qkv0 + 3 * DN_WIDTH
    ad0 = zd0 + DN_WIDTH
    bd0 = ad0 + 2 * DN_HEADS
    bc_cols = []
    for g in range(SSD_GROUPS):
        bc_cols += [wi[:, bm0 + g * SSD_STATE:bm0 + (g + 1) * SSD_STATE],
                    wi[:, cm0 + g * SSD_STATE:cm0 + (g + 1) * SSD_STATE]]
    w_r = jnp.concatenate(
        [wi[:, 0:SSD_WIDTH], wi[:, xbc0:bm0]] + bc_cols +
        [wi[:, qkv0:zd0], wi[:, zd0:ad0], wi[:, dt0:qkv0], wi[:, ad0:bd0], wi[:, bd0:bd0 + 2 * DN_HEADS],
         jnp.zeros((D_MODEL, PROJ_COLS - COL_SMALL - 2 * SSD_HEADS - 4 * DN_HEADS), F32)],
        axis=1).astype(BF16)

    w_router = jnp.concatenate(
        [w_router_group[layer], w_router_expert[layer],
         jnp.zeros((D_MODEL, LANES - N_EXPERT_GROUPS - N_EXPERTS), F32)], axis=1)
    b_router = jnp.concatenate(
        [b_router_group[layer], b_router_expert[layer],
         jnp.zeros((LANES - N_EXPERT_GROUPS - N_EXPERTS,), F32)]).reshape(1, LANES)
    consts = {
        "ssd_norm_w": ssd_norm_w[layer].reshape(1, -1),
        "norm2_w": norm2_w[layer].reshape(1, -1),
        "w_out": w_out[layer].astype(BF16),
        "w_router": w_router.astype(BF16),
        "b_router": b_router,
    }
    consts.update(_ssd_consts(conv_ssd_w[layer], conv_ssd_b[layer], ssd_d[layer]))
    consts.update(_dn_consts(conv_dn_w[layer], conv_dn_b[layer], ssd_dt_bias[layer], ssd_a_log[layer],
                             dn_dt_bias[layer], dn_a_log[layer], dn_norm_w[layer]))

    cond = _pad_rows(jnp.concatenate([c_ctx[None, :], c], axis=0), SUBLANES)
    mod = _adaln(cond, w_ada[layer], b_ada[layer]).reshape(SUBLANES, N_MOD, D_MODEL)

    pos = _grid_sincos_2d(dec_seq)
    n1w = norm1_w[layer].reshape(1, -1)
    proj_main, small, x0 = _inproj(x_prompt.reshape(n_ctx, D_MODEL), x_sample.reshape(n_lat, D_MODEL), pos,
                                   mod, n1w, w_r, dec_seq)
    proj = (proj_main, small)

    def ssd_state_in(s):
        return s[:, layer].transpose(0, 3, 1, 2).reshape(dec_b, SSD_STATE, SSD_WIDTH)

    yssd_ctx, hf, hb = _ssd(proj, consts, n_b, seq, 0, None, True)
    yssd_lat = _ssd(proj, consts, dec_b, dec_seq, n_ctx,
                    (ssd_state_in(state_ssd_fwd), ssd_state_in(state_ssd_bwd)), False)[0]
    ydn_ctx, sf, sb = _dn(proj, consts, n_ctx // dec_seq, dec_seq, seq, 0, None, True)
    ydn_lat = _dn(proj, consts, dec_b, dec_seq, dec_seq, n_ctx,
                  (state_dn_fwd[:, layer], state_dn_bwd[:, layer]), False)[0]

    x1, h2, route = _outproj(yssd_ctx, yssd_lat, ydn_ctx, ydn_lat, x0, mod, consts, dec_seq)

    slots, counts = _ranks(route)
    counts = counts[0, :N_EXPERTS].astype(jnp.int32)
    padded = ((counts + MOE_TILE - 1) // MOE_TILE) * MOE_TILE
    ends = jnp.cumsum(padded)
    offsets = ends - padded
    n_sorted = 2 * n_total + N_EXPERTS * MOE_TILE
    ids = slots[:, 0:2, :].astype(jnp.int32)
    expert_ids = jnp.arange(N_EXPERTS, dtype=jnp.int32)
    slot_base = jnp.sum(jnp.where(ids[..., None] == expert_ids, offsets, 0), axis=-1)
    pos_slots = slot_base + slots[:, 2:4, :].astype(jnp.int32)
    pos3 = pos_slots.reshape(n_total // ROW_TILE, 1, 2 * ROW_TILE)
    tile_start = jnp.arange(n_sorted // MOE_TILE, dtype=jnp.int32) * MOE_TILE
    tile_expert = jnp.minimum(jnp.sum(tile_start[:, None] >= ends[None, :], axis=1), N_EXPERTS - 1).astype(jnp.int32)
    tile_valid = (tile_start < ends[-1]).astype(jnp.int32)

    xs = _scatter_rows(pos3, h2, n_sorted)
    ys = _grouped_ffn(tile_expert, tile_valid, xs, w_gate[layer], w_up[layer], w_down[layer])
    fw = final_norm_w.reshape(1, -1)
    y_prompt = _combine(pos3, route, x1, mod, fw, ys, n_ctx, 0, n_ctx, 0)
    y_sample = _combine(pos3, route, x1, mod, fw, ys, n_lat, n_ctx, dec_seq, 1)

    def ssd_state_out(s):
        return s.reshape(n_b, SSD_STATE, SSD_HEADS, SSD_HEAD_DIM).transpose(0, 2, 3, 1)[:, None]

    return (y_prompt.reshape(n_b, seq, D_MODEL), y_sample.reshape(dec_b, dec_seq, D_MODEL),
            ssd_state_out(hf), ssd_state_out(hb), sf[:, None], sb[:, None])
```
